```python
import math
import jax, jax.numpy as jnp
from jax import lax
import numpy as np

D_MODEL = 1024
BATCH = 16
SEQ = 2048
DEPTH = 2

D_MIX = 1024
HEAD_DIM = 64
ATT_HEADS = 6
ATT_WIDTH = ATT_HEADS * HEAD_DIM
DILATED_PAIRS = ((128, 1), (512, 4), (2048, 16))
ATT_BLOCK = 128
SSD_HEADS = 6
SSD_HEADDIM = 64
SSD_WIDTH = SSD_HEADS * SSD_HEADDIM
SSD_GROUPS = 2
SSD_STATE = 128
SSD_CONV = 4
SSD_CHUNK = 128
SSD_CONV_DIM = SSD_WIDTH + 2 * SSD_GROUPS * SSD_STATE
SGU_GROUPS = 4
SGU_GROUP_DIM = 64
SGU_WIDTH = SGU_GROUPS * SGU_GROUP_DIM
SGU_CHUNK = 128
D_FF = 2816
D_IN = 3 * ATT_WIDTH + SSD_WIDTH + SSD_CONV_DIM + SSD_HEADS + 2 * SGU_WIDTH
RMS_EPS = 1e-6
LN_EPS = 1e-5

kernel_name = 'hybrid_dilated_ssd_sgu_macaron'


def _rmsnorm(x, g):
    xf = x.astype(jnp.float32)
    y = xf * lax.rsqrt(jnp.mean(xf * xf, axis=-1, keepdims=True) + RMS_EPS)
    return (y * g.astype(jnp.float32)).astype(x.dtype)


def _swiglu(x, w_gate, w_up, w_down):
    return (jax.nn.silu(x @ w_gate) * (x @ w_up)) @ w_down


def _to_blocks(t, dil, n_blocks):
    b, s, h, e = t.shape
    L = s // dil
    t = t.reshape(b, L, dil, h, e).transpose(0, 2, 1, 3, 4)
    t = jnp.pad(t, ((0, 0), (0, 0), (0, n_blocks * ATT_BLOCK - L), (0, 0), (0, 0)))
    return t.reshape(b, dil, n_blocks, ATT_BLOCK, h, e)


def _with_prev_block(t):
    prev = jnp.pad(t, ((0, 0), (0, 0), (1, 0), (0, 0), (0, 0), (0, 0)))[:, :, :-1]
    return jnp.concatenate([prev, t], axis=3)


def _dilated_branch(q, k, v, window, dil):
    b, s, h, e = q.shape
    L = s // dil
    nb = -(-L // ATT_BLOCK)
    span = window // dil
    qb = _to_blocks(q, dil, nb)
    kk = _with_prev_block(_to_blocks(k, dil, nb))
    vv = _with_prev_block(_to_blocks(v, dil, nb))
    scores = jnp.einsum('bdnqhe,bdnkhe->bdnhqk', qb, kk,
                        preferred_element_type=jnp.float32) * (e ** -0.5)
    qi = jnp.arange(ATT_BLOCK)[:, None]
    kj = jnp.arange(2 * ATT_BLOCK)[None, :]
    dist = qi + ATT_BLOCK - kj
    band = (dist >= 0) & (dist <= span)
    valid_key = (jnp.arange(nb)[:, None, None] > 0) | (kj[None] >= ATT_BLOCK)
    mask = (band[None] & valid_key)[:, None]
    scores = jnp.where(mask, scores, -jnp.inf)
    m = jnp.max(scores, axis=-1, keepdims=True)
    p = jnp.exp(scores - m)
    den = jnp.sum(p, axis=-1)
    o = jnp.einsum('bdnhqk,bdnkhe->bdnqhe', p.astype(v.dtype), vv)
    o = o / den.transpose(0, 1, 2, 4, 3)[..., None]
    lse = (m[..., 0] + jnp.log(den)).transpose(0, 1, 2, 4, 3)
    o = o.reshape(b, dil, nb * ATT_BLOCK, h, e)[:, :, :L].transpose(0, 2, 1, 3, 4).reshape(b, s, h, e)
    lse = lse.reshape(b, dil, nb * ATT_BLOCK, h)[:, :, :L].transpose(0, 2, 1, 3).reshape(b, s, h)
    return o, lse


def _dilated_attention(q, k, v):
    outs, lses = [], []
    for window, dil in DILATED_PAIRS:
        o, lse = _dilated_branch(q, k, v, window, dil)
        outs.append(o)
        lses.append(lse)
    w = jax.nn.softmax(jnp.stack(lses, axis=0), axis=0)
    out = w[0][..., None] * outs[0]
    for i in range(1, len(outs)):
        out = out + w[i][..., None] * outs[i]
    return out.astype(q.dtype)


def _ssd_mixer(z, xbc, dt_raw, conv_w, conv_b, dt_bias, a_log, d_skip, norm_g):
    b, s, _ = xbc.shape
    G, J, P, N = SSD_GROUPS, SSD_HEADS // SSD_GROUPS, SSD_HEADDIM, SSD_STATE
    c, l = s // SSD_CHUNK, SSD_CHUNK
    xbc = lax.conv_general_dilated(xbc, conv_w[:, None, :].astype(xbc.dtype), window_strides=(1,),
                                   padding=[(SSD_CONV - 1, 0)],
                                   dimension_numbers=('NWC', 'WIO', 'NWC'),
                                   feature_group_count=SSD_CONV_DIM)
    xbc = jax.nn.silu(xbc + conv_b)
    xs, bm, cm = jnp.split(xbc, [SSD_WIDTH, SSD_WIDTH + G * N], axis=-1)
    dt = jax.nn.softplus(dt_raw.astype(jnp.float32) + dt_bias.astype(jnp.float32))
    a = dt * (-jnp.exp(a_log.astype(jnp.float32)))
    xs_h = xs.astype(jnp.float32).reshape(b, s, SSD_HEADS, P)
    X = (xs_h * dt[..., None]).reshape(b, c, l, G, J, P)
    Bc = bm.astype(jnp.float32).reshape(b, c, l, G, N)
    Cc = cm.astype(jnp.float32).reshape(b, c, l, G, N)
    a_cs = jnp.cumsum(a.reshape(b, c, l, G, J), axis=2)
    causal = jnp.tril(jnp.ones((l, l), dtype=bool))[:, :, None, None]
    seg = a_cs[:, :, :, None] - a_cs[:, :, None, :]
    decay_in = jnp.exp(jnp.where(causal, seg, -jnp.inf))
    cb = jnp.einsum('bclgn,bcsgn->bclsg', Cc, Bc)
    y_diag = jnp.einsum('bclsg,bclsgj,bcsgjp->bclgjp', cb, decay_in, X)
    decay_to_end = jnp.exp(a_cs[:, :, -1:] - a_cs)
    chunk_states = jnp.einsum('bclgn,bclgj,bclgjp->bcgjpn', Bc, decay_to_end, X)
    chunk_decay = jnp.exp(a_cs[:, :, -1])

    def step(h, inp):
        st, dec = inp
        return h * dec[..., None, None] + st, h

    h0 = jnp.zeros((b, G, J, P, N), jnp.float32)
    _, prev = lax.scan(step, h0, (chunk_states.transpose(1, 0, 2, 3, 4, 5),
                                  chunk_decay.transpose(1, 0, 2, 3)))
    prev = prev.transpose(1, 0, 2, 3, 4, 5)
    y_off = jnp.einsum('bclgn,bcgjpn,bclgj->bclgjp', Cc, prev, jnp.exp(a_cs))
    y = (y_diag + y_off).reshape(b, s, SSD_HEADS, P) + d_skip.astype(jnp.float32)[:, None] * xs_h
    y = y.reshape(b, s, G, J * P) * jax.nn.silu(z.astype(jnp.float32)).reshape(b, s, G, J * P)
    y = y * lax.rsqrt(jnp.mean(y * y, axis=-1, keepdims=True) + RMS_EPS)
    y = y.reshape(b, s, SSD_WIDTH) * norm_g.astype(jnp.float32)
    return y.astype(z.dtype)


def _sgu_mixer(uv, ln_g, ln_b, w_s, b_s):
    b, s, _ = uv.shape
    uv = jax.nn.gelu(uv, approximate=False)
    u, v = jnp.split(uv, 2, axis=-1)
    vf = v.astype(jnp.float32)
    mu = jnp.mean(vf, axis=-1, keepdims=True)
    var = jnp.mean(jnp.square(vf - mu), axis=-1, keepdims=True)
    vn = (vf - mu) * lax.rsqrt(var + LN_EPS) * ln_g.astype(jnp.float32) + ln_b.astype(jnp.float32)
    vc = vn.reshape(b, s // SGU_CHUNK, SGU_CHUNK, SGU_GROUPS, SGU_GROUP_DIM)
    tri = jnp.tril(jnp.ones((SGU_CHUNK, SGU_CHUNK), dtype=bool))[None]
    w_causal = jnp.where(tri, w_s.astype(jnp.float32), 0.0)
    mixed = jnp.einsum('gts,bnsgc->bntgc', w_causal, vc) + b_s.astype(jnp.float32).T[:, :, None]
    return (u.astype(jnp.float32) * mixed.reshape(b, s, SGU_WIDTH)).astype(uv.dtype)


def setup_inputs(seed: int = 0) -> dict:
    key = jax.random.key(seed)
    ks = jax.random.split(key, 32)
    f32 = jnp.float32
    nrm = lambda k, shape, scale: jax.random.normal(k, shape, f32) * scale
    gain = lambda k, shape: jnp.ones(shape, f32) + 0.02 * jax.random.normal(k, shape, f32)
    dt0 = jnp.exp(jax.random.uniform(ks[9], (DEPTH, SSD_HEADS), f32, math.log(1e-3), math.log(1e-1)))
    return {
        'x': jax.random.normal(ks[0], (BATCH, SEQ, D_MODEL), f32),
        'ffn1_norm': gain(ks[1], (DEPTH, D_MODEL)),
        'ffn1_w_gate': nrm(ks[2], (DEPTH, D_MODEL, D_FF), D_MODEL ** -0.5),
        'ffn1_w_up': nrm(ks[3], (DEPTH, D_MODEL, D_FF), D_MODEL ** -0.5),
        'ffn1_w_down': nrm(ks[4], (DEPTH, D_FF, D_MODEL), D_FF ** -0.5),
        'mix_norm': gain(ks[5], (DEPTH, D_MODEL)),
        'w_in': nrm(ks[6], (DEPTH, D_MODEL, D_IN), D_MODEL ** -0.5),
        'conv_w': nrm(ks[7], (DEPTH, SSD_CONV, SSD_CONV_DIM), SSD_CONV ** -0.5),
        'conv_b': nrm(ks[8], (DEPTH, SSD_CONV_DIM), 0.02),
        'dt_bias': dt0 + jnp.log(-jnp.expm1(-dt0)),
        'a_log': jnp.log(jax.random.uniform(ks[10], (DEPTH, SSD_HEADS), f32, 1.0, 16.0)),
        'd_skip': jnp.ones((DEPTH, SSD_HEADS), f32) + 0.1 * jax.random.normal(ks[11], (DEPTH, SSD_HEADS), f32),
        'ssd_norm': gain(ks[12], (DEPTH, SSD_WIDTH)),
        'sgu_ln_g': gain(ks[13], (DEPTH, SGU_WIDTH)),
        'sgu_ln_b': nrm(ks[14], (DEPTH, SGU_WIDTH), 0.02),
        'sgu_w': nrm(ks[15], (DEPTH, SGU_GROUPS, SGU_CHUNK, SGU_CHUNK), SGU_CHUNK ** -0.5),
        'sgu_b': jnp.ones((DEPTH, SGU_GROUPS, SGU_CHUNK), f32) + 0.1 * jax.random.normal(ks[16], (DEPTH, SGU_GROUPS, SGU_CHUNK), f32),
        'w_out': nrm(ks[17], (DEPTH, D_MIX, D_MODEL), D_MIX ** -0.5),
        'ffn2_norm': gain(ks[18], (DEPTH, D_MODEL)),
        'ffn2_w_gate': nrm(ks[19], (DEPTH, D_MODEL, D_FF), D_MODEL ** -0.5),
        'ffn2_w_up': nrm(ks[20], (DEPTH, D_MODEL, D_FF), D_MODEL ** -0.5),
        'ffn2_w_down': nrm(ks[21], (DEPTH, D_FF, D_MODEL), D_FF ** -0.5),
        'final_norm': gain(ks[22], (D_MODEL,)),
    }


def reference(x, ffn1_norm, ffn1_w_gate, ffn1_w_up, ffn1_w_down, mix_norm, w_in, conv_w, conv_b,
              dt_bias, a_log, d_skip, ssd_norm, sgu_ln_g, sgu_ln_b, sgu_w, sgu_b, w_out,
              ffn2_norm, ffn2_w_gate, ffn2_w_up, ffn2_w_down, final_norm):
    b, s, _ = x.shape
    widths = [ATT_WIDTH, ATT_WIDTH, ATT_WIDTH, SSD_WIDTH, SSD_CONV_DIM, SSD_HEADS]
    offsets = []
    acc = 0
    for wdt in widths:
        acc += wdt
        offsets.append(acc)
    for i in range(DEPTH):
        x = x + 0.5 * _swiglu(_rmsnorm(x, ffn1_norm[i]), ffn1_w_gate[i], ffn1_w_up[i], ffn1_w_down[i])
        h = _rmsnorm(x, mix_norm[i])
        proj = h @ w_in[i]
        q, k, v, z, xbc, dt_raw, uv = jnp.split(proj, offsets, axis=-1)
        hs = (b, s, ATT_HEADS, HEAD_DIM)
        y_att = _dilated_attention(q.reshape(hs), k.reshape(hs), v.reshape(hs)).reshape(b, s, ATT_WIDTH)
        y_ssd = _ssd_mixer(z, xbc, dt_raw, conv_w[i], conv_b[i], dt_bias[i], a_log[i], d_skip[i], ssd_norm[i])
        y_sgu = _sgu_mixer(uv, sgu_ln_g[i], sgu_ln_b[i], sgu_w[i], sgu_b[i])
        x = x + jnp.concatenate([y_att, y_ssd, y_sgu], axis=-1) @ w_out[i]
        x = x + 0.5 * _swiglu(_rmsnorm(x, ffn2_norm[i]), ffn2_w_gate[i], ffn2_w_up[i], ffn2_w_down[i])
    return _rmsnorm(x, final_norm)
```

```python
import functools
import math

import numpy as np
import jax
import jax.numpy as jnp
from jax import lax
from jax.experimental import pallas as pl
from jax.experimental.pallas import tpu as pltpu

F32 = jnp.float32
BF16 = jnp.bfloat16

D_MODEL = 1024
D_FF = 2816
HEAD_DIM = 64
ATT_HEADS = 6
ATT_WIDTH = ATT_HEADS * HEAD_DIM
DILATED_PAIRS = ((128, 1), (512, 4), (2048, 16))
SSD_HEADS = 6
SSD_HEADDIM = 64
SSD_WIDTH = SSD_HEADS * SSD_HEADDIM
SSD_GROUPS = 2
SSD_STATE = 128
SSD_CONV = 4
SSD_CHUNK = 128
SSD_CONV_DIM = SSD_WIDTH + 2 * SSD_GROUPS * SSD_STATE
SGU_GROUPS = 4
SGU_GROUP_DIM = 64
SGU_WIDTH = SGU_GROUPS * SGU_GROUP_DIM
SGU_CHUNK = 128
RMS_EPS = 1e-6
LN_EPS = 1e-5

LANES = 128
DT_PAD = LANES
QKV_WIDTH = 3 * ATT_WIDTH
UV_WIDTH = 2 * SGU_WIDTH
PROJ_WIDTH = QKV_WIDTH + SSD_WIDTH + SSD_CONV_DIM + UV_WIDTH + DT_PAD

VMEM_LIMIT = 56 * 1024 * 1024

ROW_TILE = 512
FF_CHUNK = 256
PROJ_CHUNK = 512

ATT_TQ = 256
ATT_TK = 256
ATT_BIAS_TILES = 4


def _params(n_axes):
    return pltpu.CompilerParams(dimension_semantics=("arbitrary",) * n_axes,
                                vmem_limit_bytes=VMEM_LIMIT)


def _rmsnorm_f32(x, g):
    ms = jnp.mean(x * x, axis=-1, keepdims=True)
    return x * lax.rsqrt(ms + RMS_EPS) * g


def _silu(x):
    return x * jax.nn.sigmoid(x)


def _ffn_kernel(x_ref, g_ref, wg_ref, wu_ref, wd_ref, *rest, final_norm):
    if final_norm:
        fg_ref, o_ref, xn_ref, h_ref = rest
    else:
        o_ref, xn_ref, h_ref = rest
    xn_ref[...] = _rmsnorm_f32(x_ref[...], g_ref[...]).astype(BF16)
    for f in range(D_FF // FF_CHUNK):
        cols = slice(f * FF_CHUNK, (f + 1) * FF_CHUNK)
        xn = xn_ref[...]
        gate = jnp.dot(xn, wg_ref[:, cols], preferred_element_type=F32)
        up = jnp.dot(xn, wu_ref[:, cols], preferred_element_type=F32)
        h_ref[:, cols] = (_silu(gate) * up).astype(BF16)
    y = jnp.dot(h_ref[...], wd_ref[...], preferred_element_type=F32)
    out = x_ref[...] + 0.5 * y
    if final_norm:
        out = _rmsnorm_f32(out, fg_ref[...])
    o_ref[...] = out


def _ffn(x, gain, w_gate, w_up, w_down, final_gain=None):
    m = x.shape[0]
    row = pl.BlockSpec((ROW_TILE, D_MODEL), lambda i: (i, 0))
    vec = pl.BlockSpec((1, D_MODEL), lambda i: (0, 0))
    in_specs = [row, vec,
                pl.BlockSpec((D_MODEL, D_FF), lambda i: (0, 0)),
                pl.BlockSpec((D_MODEL, D_FF), lambda i: (0, 0)),
                pl.BlockSpec((D_FF, D_MODEL), lambda i: (0, 0))]
    args = [x, gain.reshape(1, D_MODEL), w_gate.astype(BF16), w_up.astype(BF16), w_down.astype(BF16)]
    if final_gain is not None:
        in_specs.append(vec)
        args.append(final_gain.reshape(1, D_MODEL))
    return pl.pallas_call(
        functools.partial(_ffn_kernel, final_norm=final_gain is not None),
        grid=(m // ROW_TILE,),
        in_specs=in_specs,
        out_specs=row,
        out_shape=jax.ShapeDtypeStruct((m, D_MODEL), F32),
        scratch_shapes=[pltpu.VMEM((ROW_TILE, D_MODEL), BF16), pltpu.VMEM((ROW_TILE, D_FF), BF16)],
        compiler_params=_params(1),
        name="ffn",
    )(*args)


_PROJ_PIECES = (("qkv", QKV_WIDTH, BF16), ("z", SSD_WIDTH, F32), ("xbc", SSD_CONV_DIM, F32),
                ("uv", UV_WIDTH, F32), ("dt", DT_PAD, F32))


def _inproj_kernel(x_ref, g_ref, w_ref, qkv_ref, z_ref, xbc_ref, uv_ref, dt_ref, xn_ref):
    outs = (qkv_ref, z_ref, xbc_ref, uv_ref, dt_ref)
    xn_ref[...] = _rmsnorm_f32(x_ref[...], g_ref[...]).astype(BF16)
    starts = np.cumsum([0] + [p[1] for p in _PROJ_PIECES])
    for c in range(PROJ_WIDTH // PROJ_CHUNK):
        lo, hi = c * PROJ_CHUNK, (c + 1) * PROJ_CHUNK
        r = jnp.dot(xn_ref[...], w_ref[:, lo:hi], preferred_element_type=F32)
        for k, o_ref in enumerate(outs):
            a, b = max(lo, int(starts[k])), min(hi, int(starts[k + 1]))
            if a < b:
                o_ref[:, a - int(starts[k]):b - int(starts[k])] = r[:, a - lo:b - lo].astype(o_ref.dtype)


def _inproj(x, gain, w_in):
    m = x.shape[0]
    o = np.cumsum([0, ATT_WIDTH, ATT_WIDTH, ATT_WIDTH, SSD_WIDTH, SSD_CONV_DIM, SSD_HEADS, UV_WIDTH])
    w_qkv, w_z, w_xbc, w_dt, w_uv = (w_in[:, o[0]:o[3]], w_in[:, o[3]:o[4]], w_in[:, o[4]:o[5]],
                                     w_in[:, o[5]:o[6]], w_in[:, o[6]:o[7]])
    w_dt = jnp.pad(w_dt, ((0, 0), (0, DT_PAD - SSD_HEADS)))
    w_all = jnp.concatenate([w_qkv, w_z, w_xbc, w_uv, w_dt], axis=1).astype(BF16)
    row = lambda w: pl.BlockSpec((ROW_TILE, w), lambda i: (i, 0))
    return pl.pallas_call(
        _inproj_kernel,
        grid=(m // ROW_TILE,),
        in_specs=[row(D_MODEL), pl.BlockSpec((1, D_MODEL), lambda i: (0, 0)),
                  pl.BlockSpec((D_MODEL, PROJ_WIDTH), lambda i: (0, 0))],
        out_specs=[row(w) for _, w, _ in _PROJ_PIECES],
        out_shape=[jax.ShapeDtypeStruct((m, w), dt) for _, w, dt in _PROJ_PIECES],
        scratch_shapes=[pltpu.VMEM((ROW_TILE, D_MODEL), BF16)],
        compiler_params=_params(1),
        name="inproj",
    )(x, gain.reshape(1, D_MODEL), w_all)


def _att_bias_tiles():
    i = np.arange(ATT_TQ)[:, None]
    j = np.arange(ATT_TK)[None, :]
    n_off = 2048 // ATT_TQ
    tiles = []
    for off in range(n_off):
        d = off * ATT_TQ + i - j
        cnt = np.zeros(d.shape, np.float64)
        for window, dil in DILATED_PAIRS:
            cnt += (d >= 0) & (d % dil == 0) & (d // dil <= window // dil)
        with np.errstate(divide="ignore"):
            tiles.append(np.log(cnt))
    for off in range(ATT_BIAS_TILES, n_off):
        assert np.array_equal(tiles[off], tiles[ATT_BIAS_TILES - 1])
    return np.stack(tiles[:ATT_BIAS_TILES]).astype(np.float32)


def _att_kernel(q_ref, k_ref, v_ref, bias_ref, o_ref):
    qb = pl.program_id(2)
    lane = lax.broadcasted_iota(jnp.int32, (1, LANES), 1)
    q = q_ref[...] * BF16(HEAD_DIM ** -0.5)
    outs = []
    for h in range(2):
        head_lanes = (lane >= h * HEAD_DIM) & (lane < (h + 1) * HEAD_DIM)
        qh = jnp.where(head_lanes, q, jnp.zeros_like(q))

        def body(kb, carry, qh=qh):
            m, l, acc = carry
            rows = pl.ds(pl.multiple_of(kb * ATT_TK, ATT_TK), ATT_TK)
            k = k_ref[rows, :]
            v = v_ref[rows, :]
            s = lax.dot_general(qh, k, (((1,), (1,)), ((), ())), preferred_element_type=F32)
            s = s + bias_ref[jnp.minimum(qb - kb, ATT_BIAS_TILES - 1)]
            m_new = jnp.maximum(m, jnp.max(s, axis=-1, keepdims=True))
            alpha = jnp.exp(m - m_new)
            p = jnp.exp(s - m_new)
            l = alpha * l + jnp.sum(p, axis=-1, keepdims=True)
            acc = alpha * acc + jnp.dot(p.astype(BF16), v, preferred_element_type=F32)
            return m_new, l, acc

        init = (jnp.full((ATT_TQ, 1), -jnp.inf, F32), jnp.zeros((ATT_TQ, 1), F32),
                jnp.zeros((ATT_TQ, LANES), F32))
        _, l, acc = lax.fori_loop(0, qb + 1, body, init)
        outs.append(acc / l)
    o_ref[...] = jnp.where(lane < HEAD_DIM, outs[0], outs[1]).astype(o_ref.dtype)


def _attention(qkv):
    b, s, _ = qkv.shape
    assert s == 2048 and ATT_TQ == ATT_TK
    n_pairs = ATT_WIDTH // LANES
    bias = jnp.asarray(_att_bias_tiles())
    return pl.pallas_call(
        _att_kernel,
        grid=(b, n_pairs, s // ATT_TQ),
        in_specs=[pl.BlockSpec((None, ATT_TQ, LANES), lambda bi, hp, qb: (bi, qb, hp)),
                  pl.BlockSpec((None, s, LANES), lambda bi, hp, qb: (bi, 0, n_pairs + hp)),
                  pl.BlockSpec((None, s, LANES), lambda bi, hp, qb: (bi, 0, 2 * n_pairs + hp)),
                  pl.BlockSpec((ATT_BIAS_TILES, ATT_TQ, ATT_TK), lambda bi, hp, qb: (0, 0, 0))],
        out_specs=pl.BlockSpec((None, ATT_TQ, LANES), lambda bi, hp, qb: (bi, qb, hp)),
        out_shape=jax.ShapeDtypeStruct((b, s, ATT_WIDTH), BF16),
        compiler_params=_params(3),
        name="dilated_attention",
    )(qkv, qkv, qkv, bias)


SSD_HALO = 8
HEADS_PER_GROUP = SSD_HEADS // SSD_GROUPS
GROUP_LANES = HEADS_PER_GROUP * SSD_HEADDIM


def _ssd_kernel(z_ref, xbc_ref, dt_ref, cw_ref, cb_ref, dtb_ref, alog_ref, dsk_ref, ng_ref,
                o_ref, state_ref, ext_ref):
    n_chunks = z_ref.shape[0] // SSD_CHUNK
    L = SSD_CHUNK
    state_ref[...] = jnp.zeros_like(state_ref)
    row = lax.broadcasted_iota(jnp.int32, (L, L), 0)
    col = lax.broadcasted_iota(jnp.int32, (L, L), 1)
    tril = row >= col
    cumsum_mat = tril.astype(F32)
    lane = lax.broadcasted_iota(jnp.int32, (1, LANES), 1)
    lane_w = lax.broadcasted_iota(jnp.int32, (1, SSD_WIDTH), 1)
    first_group = lane_w < GROUP_LANES
    first_head = lane < SSD_HEADDIM
    a_neg = -jnp.exp(alog_ref[...])
    n_b = SSD_GROUPS * SSD_STATE

    def chunk(c, carry):
        r0 = pl.multiple_of(c * L, L)
        rows = pl.ds(r0, L)
        halo_start = pl.multiple_of(jnp.maximum(r0 - SSD_HALO, 0), SSD_HALO)
        halo = xbc_ref[pl.ds(halo_start, SSD_HALO), :]
        ext_ref[0:SSD_HALO, :] = jnp.where(c > 0, halo, jnp.zeros_like(halo))
        ext_ref[SSD_HALO:, :] = xbc_ref[rows, :]
        conv = cb_ref[...]
        for w in range(SSD_CONV):
            o = SSD_HALO - (SSD_CONV - 1) + w
            conv = conv + cw_ref[w:w + 1, :] * ext_ref[o:o + L, :]
        xact = _silu(conv)
        xs = xact[:, :SSD_WIDTH]
        bm = [xact[:, SSD_WIDTH + g * SSD_STATE:SSD_WIDTH + (g + 1) * SSD_STATE] for g in range(SSD_GROUPS)]
        cm = [xact[:, SSD_WIDTH + n_b + g * SSD_STATE:SSD_WIDTH + n_b + (g + 1) * SSD_STATE]
              for g in range(SSD_GROUPS)]
        bm16 = [t.astype(BF16) for t in bm]
        cm16 = [t.astype(BF16) for t in cm]

        dt = jax.nn.softplus(dt_ref[rows, :] + dtb_ref[...])
        a = dt * a_neg
        acs = jnp.dot(cumsum_mat, a, precision=lax.Precision.HIGHEST, preferred_element_type=F32)
        acs_t = acs.T
        dt_t = dt.T
        cb = [lax.dot_general(cm16[g], bm16[g], (((1,), (1,)), ((), ())), preferred_element_type=F32)
              for g in range(SSD_GROUPS)]

        y_diag, e_pairs, w_pairs, d_pairs = [], [], [], []
        for p in range(SSD_HEADS // 2):
            xs_pair = xs[:, p * LANES:(p + 1) * LANES].astype(BF16)
            yd, ecol, wcol, dcol = [], [], [], []
            for h in (2 * p, 2 * p + 1):
                g = h // HEADS_PER_GROUP
                acs_col = jnp.broadcast_to(acs[:, h:h + 1], (L, L))
                dt_col = jnp.broadcast_to(dt[:, h:h + 1], (L, L))
                seg = acs_col - acs_t[h:h + 1, :]
                decay = jnp.exp(jnp.where(tril, seg, -jnp.inf))
                mix = (cb[g] * decay * dt_t[h:h + 1, :]).astype(BF16)
                yd.append(jnp.dot(mix, xs_pair, preferred_element_type=F32))
                a_last = acs_col[L - 1:L, :]
                ecol.append(jnp.exp(acs_col))
                wcol.append(jnp.exp(a_last - acs_col) * dt_col)
                dcol.append(jnp.exp(a_last))
            y_diag.append(jnp.where(first_head, yd[0], yd[1]))
            e_pairs.append(jnp.where(first_head, ecol[0], ecol[1]))
            w_pairs.append(jnp.where(first_head, wcol[0], wcol[1]))
            d_pairs.append(jnp.where(first_head, dcol[0], dcol[1]))
        y_diag = jnp.concatenate(y_diag, axis=1)
        exp_acs = jnp.concatenate(e_pairs, axis=1)
        to_end = jnp.concatenate(w_pairs, axis=1)
        chunk_decay = jnp.concatenate(d_pairs, axis=1)

        state = state_ref[...]
        st16 = state.astype(BF16)
        y_off = jnp.where(first_group,
                          jnp.dot(cm16[0], st16, preferred_element_type=F32),
                          jnp.dot(cm16[1], st16, preferred_element_type=F32)) * exp_acs
        xdd = (xs * to_end).astype(BF16)
        new = jnp.where(first_group,
                        jnp.dot(bm[0].T.astype(BF16), xdd, preferred_element_type=F32),
                        jnp.dot(bm[1].T.astype(BF16), xdd, preferred_element_type=F32))
        state_ref[...] = state * chunk_decay + new

        y = y_diag + y_off + dsk_ref[...] * xs
        y = y * _silu(z_ref[rows, :])
        ysq = y * y
        s0 = jnp.sum(jnp.where(first_group, ysq, 0.0), axis=-1, keepdims=True)
        s1 = jnp.sum(jnp.where(first_group, 0.0, ysq), axis=-1, keepdims=True)
        ms = jnp.where(first_group, s0, s1) * (1.0 / GROUP_LANES)
        o_ref[rows, :] = (y * lax.rsqrt(ms + RMS_EPS) * ng_ref[...]).astype(o_ref.dtype)
        return carry

    lax.fori_loop(0, n_chunks, chunk, 0)


def _ssd(z, xbc, dt, conv_w, conv_b, dt_bias, a_log, d_skip, norm_g):
    b, s, _ = z.shape
    pad = lambda v: jnp.pad(v, (0, DT_PAD - SSD_HEADS)).reshape(1, DT_PAD)
    seq = lambda w: pl.BlockSpec((None, s, w), lambda bi: (bi, 0, 0))
    full = lambda r, w: pl.BlockSpec((r, w), lambda bi: (0, 0))
    return pl.pallas_call(
        _ssd_kernel,
        grid=(b,),
        in_specs=[seq(SSD_WIDTH), seq(SSD_CONV_DIM), seq(DT_PAD),
                  full(SSD_CONV, SSD_CONV_DIM), full(1, SSD_CONV_DIM), full(1, DT_PAD), full(1, DT_PAD),
                  full(1, SSD_WIDTH), full(1, SSD_WIDTH)],
        out_specs=seq(SSD_WIDTH),
        out_shape=jax.ShapeDtypeStruct((b, s, SSD_WIDTH), BF16),
        scratch_shapes=[pltpu.VMEM((SSD_STATE, SSD_WIDTH), F32),
                        pltpu.VMEM((SSD_HALO + SSD_CHUNK, SSD_CONV_DIM), F32)],
        compiler_params=_params(1),
        name="ssd",
    )(z, xbc, dt, conv_w, conv_b.reshape(1, SSD_CONV_DIM), pad(dt_bias), pad(a_log),
      jnp.repeat(d_skip, SSD_HEADDIM).reshape(1, SSD_WIDTH), norm_g.reshape(1, SSD_WIDTH))


SGU_ROWS = 512


def _sgu_kernel(uv_ref, lng_ref, lnb_ref, w_ref, bs_ref, o_ref):
    uv = uv_ref[...]
    act = 0.5 * uv * (1.0 + lax.erf(uv * (1.0 / math.sqrt(2.0))))
    u = act[:, :SGU_WIDTH]
    v = act[:, SGU_WIDTH:]
    mu = jnp.mean(v, axis=-1, keepdims=True)
    var = jnp.mean(jnp.square(v - mu), axis=-1, keepdims=True)
    vn = (v - mu) * lax.rsqrt(var + LN_EPS) * lng_ref[...] + lnb_ref[...]
    row = lax.broadcasted_iota(jnp.int32, (SGU_CHUNK, SGU_CHUNK), 0)
    col = lax.broadcasted_iota(jnp.int32, (SGU_CHUNK, SGU_CHUNK), 1)
    w = [jnp.where(row >= col, w_ref[g], 0.0).astype(BF16) for g in range(SGU_GROUPS)]
    lane = lax.broadcasted_iota(jnp.int32, (1, LANES), 1)
    first = lane < SGU_GROUP_DIM
    for c in range(SGU_ROWS // SGU_CHUNK):
        rows = slice(c * SGU_CHUNK, (c + 1) * SGU_CHUNK)
        mixed = []
        for p in range(SGU_WIDTH // LANES):
            vp = vn[rows, p * LANES:(p + 1) * LANES]
            lo = jnp.where(first, vp, 0.0).astype(BF16)
            hi = jnp.where(first, 0.0, vp).astype(BF16)
            mixed.append(jnp.dot(w[2 * p], lo, preferred_element_type=F32)
                         + jnp.dot(w[2 * p + 1], hi, preferred_element_type=F32))
        mixed = jnp.concatenate(mixed, axis=1) + bs_ref[...]
        o_ref[rows, :] = (u[rows, :] * mixed).astype(o_ref.dtype)


def _sgu(uv, ln_g, ln_b, w_s, b_s):
    b, s, _ = uv.shape
    bias = jnp.repeat(b_s.T, SGU_GROUP_DIM, axis=1)
    vec = pl.BlockSpec((1, SGU_WIDTH), lambda bi, r: (0, 0))
    return pl.pallas_call(
        _sgu_kernel,
        grid=(b, s // SGU_ROWS),
        in_specs=[pl.BlockSpec((None, SGU_ROWS, UV_WIDTH), lambda bi, r: (bi, r, 0)), vec, vec,
                  pl.BlockSpec((SGU_GROUPS, SGU_CHUNK, SGU_CHUNK), lambda bi, r: (0, 0, 0)),
                  pl.BlockSpec((SGU_CHUNK, SGU_WIDTH), lambda bi, r: (0, 0))],
        out_specs=pl.BlockSpec((None, SGU_ROWS, SGU_WIDTH), lambda bi, r: (bi, r, 0)),
        out_shape=jax.ShapeDtypeStruct((b, s, SGU_WIDTH), BF16),
        compiler_params=_params(2),
        name="sgu",
    )(uv, ln_g.reshape(1, SGU_WIDTH), ln_b.reshape(1, SGU_WIDTH), w_s, bias)


def _outproj_kernel(x_ref, ya_ref, ys_ref, yg_ref, wa_ref, ws_ref, wg_ref, o_ref):
    y = (jnp.dot(ya_ref[...], wa_ref[...], preferred_element_type=F32)
         + jnp.dot(ys_ref[...], ws_ref[...], preferred_element_type=F32)
         + jnp.dot(yg_ref[...], wg_ref[...], preferred_element_type=F32))
    o_ref[...] = x_ref[...] + y


def _outproj(x, y_att, y_ssd, y_sgu, w_out):
    m = x.shape[0]
    w = w_out.astype(BF16)
    row = lambda n: pl.BlockSpec((ROW_TILE, n), lambda i: (i, 0))
    full = lambda n: pl.BlockSpec((n, D_MODEL), lambda i: (0, 0))
    return pl.pallas_call(
        _outproj_kernel,
        grid=(m // ROW_TILE,),
        in_specs=[row(D_MODEL), row(ATT_WIDTH), row(SSD_WIDTH), row(SGU_WIDTH),
                  full(ATT_WIDTH), full(SSD_WIDTH), full(SGU_WIDTH)],
        out_specs=row(D_MODEL),
        out_shape=jax.ShapeDtypeStruct((m, D_MODEL), F32),
        compiler_params=_params(1),
        name="outproj",
    )(x, y_att, y_ssd, y_sgu, w[:ATT_WIDTH], w[ATT_WIDTH:ATT_WIDTH + SSD_WIDTH], w[ATT_WIDTH + SSD_WIDTH:])


def _mixing(x, b, s, gain, w_in, conv_w, conv_b, dt_bias, a_log, d_skip, ssd_norm,
            sgu_ln_g, sgu_ln_b, sgu_w, sgu_b, w_out):
    qkv, z, xbc, uv, dt = _inproj(x, gain, w_in)
    seq = lambda t: t.reshape(b, s, t.shape[-1])
    y_att = _attention(seq(qkv))
    y_ssd = _ssd(seq(z), seq(xbc), seq(dt), conv_w, conv_b, dt_bias, a_log, d_skip, ssd_norm)
    y_sgu = _sgu(seq(uv), sgu_ln_g, sgu_ln_b, sgu_w, sgu_b)
    flat = lambda t: t.reshape(b * s, t.shape[-1])
    return _outproj(x, flat(y_att), flat(y_ssd), flat(y_sgu), w_out)


def kernel(x, ffn1_norm, ffn1_w_gate, ffn1_w_up, ffn1_w_down, mix_norm, w_in, conv_w, conv_b, dt_bias, a_log, d_skip, ssd_norm, sgu_ln_g, sgu_ln_b, sgu_w, sgu_b, w_out, ffn2_norm, ffn2_w_gate, ffn2_w_up, ffn2_w_down, final_norm):
    b, s, d = x.shape
    depth = ffn1_norm.shape[0]
    h = x.reshape(b * s, d)
    for i in range(depth):
        h = _ffn(h, ffn1_norm[i], ffn1_w_gate[i], ffn1_w_up[i], ffn1_w_down[i])
        h = _mixing(h, b, s, mix_norm[i], w_in[i], conv_w[i], conv_b[i], dt_bias[i], a_log[i], d_skip[i],
                    ssd_norm[i], sgu_ln_g[i], sgu_ln_b[i], sgu_w[i], sgu_b[i], w_out[i])
        h = _ffn(h, ffn2_norm[i], ffn2_w_gate[i], ffn2_w_up[i], ffn2_w_down[i],
                 final_gain=final_norm if i == depth - 1 else None)
    return h.reshape(b, s, d)
```

```python
import functools
import math

import numpy as np
import jax
import jax.numpy as jnp
from jax import lax
from jax.experimental import pallas as pl
from jax.experimental.pallas import tpu as pltpu

F32 = jnp.float32
BF16 = jnp.bfloat16

D_MODEL = 1024
D_FF = 2816
HEAD_DIM = 64
ATT_HEADS = 6
ATT_WIDTH = ATT_HEADS * HEAD_DIM
DILATED_PAIRS = ((128, 1), (512, 4), (2048, 16))
SSD_HEADS = 6
SSD_HEADDIM = 64
SSD_WIDTH = SSD_HEADS * SSD_HEADDIM
SSD_GROUPS = 2
SSD_STATE = 128
SSD_CONV = 4
SSD_CHUNK = 128
SSD_CONV_DIM = SSD_WIDTH + 2 * SSD_GROUPS * SSD_STATE
SGU_GROUPS = 4
SGU_GROUP_DIM = 64
SGU_WIDTH = SGU_GROUPS * SGU_GROUP_DIM
SGU_CHUNK = 128
RMS_EPS = 1e-6
LN_EPS = 1e-5

LANES = 128
DT_PAD = LANES
QKV_WIDTH = 3 * ATT_WIDTH
UV_WIDTH = 2 * SGU_WIDTH
PROJ_WIDTH = QKV_WIDTH + SSD_WIDTH + SSD_CONV_DIM + UV_WIDTH + DT_PAD

VMEM_LIMIT = 56 * 1024 * 1024

ROW_TILE = 512
FF_CHUNK = 256
PROJ_CHUNK = 512

ATT_BLOCK = 128


def _params(n_axes):
    return pltpu.CompilerParams(dimension_semantics=("arbitrary",) * n_axes,
                                vmem_limit_bytes=VMEM_LIMIT)


def _rmsnorm_f32(x, g):
    ms = jnp.mean(x * x, axis=-1, keepdims=True)
    return x * lax.rsqrt(ms + RMS_EPS) * g


def _silu(x):
    return x * jax.nn.sigmoid(x)


def _ffn_kernel(x_ref, g_ref, wg_ref, wu_ref, wd_ref, *rest, final_norm):
    if final_norm:
        fg_ref, o_ref, xn_ref, h_ref = rest
    else:
        o_ref, xn_ref, h_ref = rest
    xn_ref[...] = _rmsnorm_f32(x_ref[...], g_ref[...]).astype(BF16)
    for f in range(D_FF // FF_CHUNK):
        cols = slice(f * FF_CHUNK, (f + 1) * FF_CHUNK)
        xn = xn_ref[...]
        gate = jnp.dot(xn, wg_ref[:, cols], preferred_element_type=F32)
        up = jnp.dot(xn, wu_ref[:, cols], preferred_element_type=F32)
        h_ref[:, cols] = (_silu(gate) * up).astype(BF16)
    y = jnp.dot(h_ref[...], wd_ref[...], preferred_element_type=F32)
    out = x_ref[...] + 0.5 * y
    if final_norm:
        out = _rmsnorm_f32(out, fg_ref[...])
    o_ref[...] = out


def _ffn(x, gain, w_gate, w_up, w_down, final_gain=None):
    m = x.shape[0]
    row = pl.BlockSpec((ROW_TILE, D_MODEL), lambda i: (i, 0))
    vec = pl.BlockSpec((1, D_MODEL), lambda i: (0, 0))
    in_specs = [row, vec,
                pl.BlockSpec((D_MODEL, D_FF), lambda i: (0, 0)),
                pl.BlockSpec((D_MODEL, D_FF), lambda i: (0, 0)),
                pl.BlockSpec((D_FF, D_MODEL), lambda i: (0, 0))]
    args = [x, gain.reshape(1, D_MODEL), w_gate.astype(BF16), w_up.astype(BF16), w_down.astype(BF16)]
    if final_gain is not None:
        in_specs.append(vec)
        args.append(final_gain.reshape(1, D_MODEL))
    return pl.pallas_call(
        functools.partial(_ffn_kernel, final_norm=final_gain is not None),
        grid=(m // ROW_TILE,),
        in_specs=in_specs,
        out_specs=row,
        out_shape=jax.ShapeDtypeStruct((m, D_MODEL), F32),
        scratch_shapes=[pltpu.VMEM((ROW_TILE, D_MODEL), BF16), pltpu.VMEM((ROW_TILE, D_FF), BF16)],
        compiler_params=_params(1),
        name="ffn",
    )(*args)


_PROJ_PIECES = (("qkv", QKV_WIDTH, F32),("z", SSD_WIDTH, F32), ("xbc", SSD_CONV_DIM, F32),
                ("uv", UV_WIDTH, F32), ("dt", DT_PAD, F32))


def _inproj_kernel(x_ref, g_ref, w_ref, qkv_ref, z_ref, xbc_ref, uv_ref, dt_ref, xn_ref):
    outs = (qkv_ref, z_ref, xbc_ref, uv_ref, dt_ref)
    xn_ref[...] = _rmsnorm_f32(x_ref[...], g_ref[...]).astype(BF16)
    starts = np.cumsum([0] + [p[1] for p in _PROJ_PIECES])
    for c in range(PROJ_WIDTH // PROJ_CHUNK):
        lo, hi = c * PROJ_CHUNK, (c + 1) * PROJ_CHUNK
        r = jnp.dot(xn_ref[...], w_ref[:, lo:hi], preferred_element_type=F32)
        for k, o_ref in enumerate(outs):
            a, b = max(lo, int(starts[k])), min(hi, int(starts[k + 1]))
            if a < b:
                o_ref[:, a - int(starts[k]):b - int(starts[k])] = r[:, a - lo:b - lo].astype(o_ref.dtype)


def _inproj(x, gain, w_in):
    m = x.shape[0]
    o = np.cumsum([0, ATT_WIDTH, ATT_WIDTH, ATT_WIDTH, SSD_WIDTH, SSD_CONV_DIM, SSD_HEADS, UV_WIDTH])
    w_qkv, w_z, w_xbc, w_dt, w_uv = (w_in[:, o[0]:o[3]], w_in[:, o[3]:o[4]], w_in[:, o[4]:o[5]],
                                     w_in[:, o[5]:o[6]], w_in[:, o[6]:o[7]])
    w_dt = jnp.pad(w_dt, ((0, 0), (0, DT_PAD - SSD_HEADS)))
    w_all = jnp.concatenate([w_qkv, w_z, w_xbc, w_uv, w_dt], axis=1).astype(BF16)
    row = lambda w: pl.BlockSpec((ROW_TILE, w), lambda i: (i, 0))
    return pl.pallas_call(
        _inproj_kernel,
        grid=(m // ROW_TILE,),
        in_specs=[row(D_MODEL), pl.BlockSpec((1, D_MODEL), lambda i: (0, 0)),
                  pl.BlockSpec((D_MODEL, PROJ_WIDTH), lambda i: (0, 0))],
        out_specs=[row(w) for _, w, _ in _PROJ_PIECES],
        out_shape=[jax.ShapeDtypeStruct((m, w), dt) for _, w, dt in _PROJ_PIECES],
        scratch_shapes=[pltpu.VMEM((ROW_TILE, D_MODEL), BF16)],
        compiler_params=_params(1),
        name="inproj",
    )(x, gain.reshape(1, D_MODEL), w_all)


NAT, P4, P16 = 0, 1, 2


def _att_kernel(q_ref, k_ref, v_ref, o_ref, qa_ref, qb_ref, kk_ref, ve_ref,
                acc1_ref, m1_ref, l1_ref, acc3_ref, m3_ref, l3_ref, band_ref, cur_ref):
    seq = q_ref.shape[0]
    T = ATT_BLOCK
    d4, d16 = DILATED_PAIRS[1][1], DILATED_PAIRS[2][1]
    sub4 = seq // d4
    lane = lax.broadcasted_iota(jnp.int32, (1, LANES), 1)
    first = lane < HEAD_DIM
    qi = lax.broadcasted_iota(jnp.int32, (T, T), 0)
    kj = lax.broadcasted_iota(jnp.int32, (T, T), 1)
    cur_bias = jnp.where(kj <= qi, 0.0, -jnp.inf).astype(F32)
    prev_bias = jnp.where(kj >= qi, 0.0, -jnp.inf).astype(F32)
    for half in range(2):
        cur_ref[half * T:(half + 1) * T, :] = cur_bias
        band_ref[half * T:(half + 1) * T, 0:T] = prev_bias
        band_ref[half * T:(half + 1) * T, T:2 * T] = cur_bias
    q_scale = HEAD_DIM ** -0.5 * math.log2(math.e)

    def prep(layout, dst, src):
        q = q_ref[src, :] * q_scale
        qa_ref[layout, dst, :] = jnp.where(first, q, 0.0).astype(BF16)
        qb_ref[layout, dst, :] = jnp.where(first, 0.0, q).astype(BF16)
        kk_ref[layout, dst, :] = k_ref[src, :].astype(BF16)
        ve_ref[layout, dst, 0:LANES] = v_ref[src, :].astype(BF16)
        ve_ref[layout, dst, LANES:2 * LANES] = jnp.ones((T, LANES), BF16)

    def prep_all(c, carry):
        rows = pl.ds(pl.multiple_of(c * T, T), T)
        prep(NAT, rows, rows)
        prep(P16, rows, pl.ds(c, T, stride=d16))
        prep(P4, rows, pl.ds(c // d4 + (c % d4) * (T * d4), T, stride=d4))
        return carry

    lax.fori_loop(0, seq // T, prep_all, 0)

    def block(layout, qrows, krows, bias_ref):
        q2 = jnp.concatenate([qa_ref[layout, qrows, :], qb_ref[layout, qrows, :]], axis=0)
        s = lax.dot_general(q2, kk_ref[layout, krows, :], (((1,), (1,)), ((), ())),
                            preferred_element_type=F32) + bias_ref[...]
        m = jnp.max(s, axis=-1, keepdims=True)
        p = jnp.exp2(s - m).astype(BF16)
        r = jnp.dot(p, ve_ref[layout, krows, :], preferred_element_type=F32)
        acc = jnp.where(first, r[0:T, 0:LANES], r[T:2 * T, 0:LANES])
        lsum = jnp.where(first, r[0:T, LANES:2 * LANES], r[T:2 * T, LANES:2 * LANES])
        return acc, jnp.where(first, m[0:T], m[T:2 * T]), lsum

    def rows_of(start, n=T):
        return pl.ds(pl.multiple_of(start, T), n)

    def store1(rows, acc, mb, lsum):
        acc1_ref[rows, :] = acc
        m1_ref[rows, :] = mb
        l1_ref[rows, :] = lsum

    store1(rows_of(0), *block(NAT, rows_of(0), rows_of(0), cur_ref))

    def branch1(n, carry):
        store1(rows_of(n * T), *block(NAT, rows_of(n * T), rows_of((n - 1) * T, 2 * T), band_ref))
        return carry

    lax.fori_loop(1, seq // T, branch1, 0, unroll=3)

    def branch3(r16, carry):
        rows = rows_of(r16 * T)
        acc, mb, lsum = block(P16, rows, rows, cur_ref)
        dst = pl.ds((r16 % d4) * sub4 + r16 // d4, T, stride=d4)
        acc3_ref[dst, :] = acc
        m3_ref[dst, :] = mb
        l3_ref[dst, :] = lsum
        return carry

    lax.fori_loop(0, d16, branch3, 0, unroll=4)

    def finish(r4, n, keys, bias_ref):
        prow = rows_of(r4 * sub4 + n * T)
        acc2, mb2, l2 = block(P4, prow, keys, bias_ref)
        trow = pl.ds(r4 + n * (T * d4), T, stride=d4)
        acc1, mb1, l1 = acc1_ref[trow, :], m1_ref[trow, :], l1_ref[trow, :]
        acc3, mb3, l3 = acc3_ref[prow, :], m3_ref[prow, :], l3_ref[prow, :]
        m = jnp.maximum(mb1, jnp.maximum(mb2, mb3))
        w1, w2, w3 = jnp.exp2(mb1 - m), jnp.exp2(mb2 - m), jnp.exp2(mb3 - m)
        num = w1 * acc1 + w2 * acc2 + w3 * acc3
        den = w1 * l1 + w2 * l2 + w3 * l3
        o_ref[trow, :] = num / den

    def branch2(r4, carry):
        finish(r4, 0, rows_of(r4 * sub4), cur_ref)
        for n in range(1, sub4 // T):
            finish(r4, n, rows_of(r4 * sub4 + (n - 1) * T, 2 * T), band_ref)
        return carry

    lax.fori_loop(0, d4, branch2, 0)


def _attention(qkv):
    b, s, _ = qkv.shape
    for window, dil in DILATED_PAIRS:
        assert window // dil == ATT_BLOCK and s % (ATT_BLOCK * dil) == 0
    assert DILATED_PAIRS[0][1] == 1 and DILATED_PAIRS[2][1] == DILATED_PAIRS[1][1] ** 2
    n_pairs = ATT_WIDTH // LANES
    spec = lambda part: pl.BlockSpec((None, s, LANES), lambda bi, hp: (bi, 0, part * n_pairs + hp))
    return pl.pallas_call(
        _att_kernel,
        grid=(b, n_pairs),
        in_specs=[spec(0), spec(1), spec(2)],
        out_specs=pl.BlockSpec((None, s, LANES), lambda bi, hp: (bi, 0, hp)),
        out_shape=jax.ShapeDtypeStruct((b, s, ATT_WIDTH), F32),
        scratch_shapes=[pltpu.VMEM((3, s, LANES), BF16)] * 3 + [pltpu.VMEM((3, s, 2 * LANES), BF16)]
        + [pltpu.VMEM((s, LANES), F32)] * 6
        + [pltpu.VMEM((2 * ATT_BLOCK, 2 * ATT_BLOCK), F32), pltpu.VMEM((2 * ATT_BLOCK, ATT_BLOCK), F32)],
        compiler_params=_params(2),
        name="dilated_attention",
    )(qkv, qkv, qkv)


SSD_HALO = 8
HEADS_PER_GROUP = SSD_HEADS // SSD_GROUPS
GROUP_LANES = HEADS_PER_GROUP * SSD_HEADDIM


def _ssd_kernel(z_ref, xbc_ref, dt_ref, cw_ref, cb_ref, dtb_ref, alog_ref, dsk_ref, ng_ref,
                o_ref, state_ref, ext_ref):
    n_chunks = z_ref.shape[0] // SSD_CHUNK
    L = SSD_CHUNK
    state_ref[...] = jnp.zeros_like(state_ref)
    row = lax.broadcasted_iota(jnp.int32, (L, L), 0)
    col = lax.broadcasted_iota(jnp.int32, (L, L), 1)
    tril = row >= col
    cumsum_mat = tril.astype(F32)
    lane = lax.broadcasted_iota(jnp.int32, (1, LANES), 1)
    lane_w = lax.broadcasted_iota(jnp.int32, (1, SSD_WIDTH), 1)
    first_group = lane_w < GROUP_LANES
    first_head = lane < SSD_HEADDIM
    a_neg = -jnp.exp(alog_ref[...])
    n_b = SSD_GROUPS * SSD_STATE

    def chunk(c, carry):
        r0 = pl.multiple_of(c * L, L)
        rows = pl.ds(r0, L)
        halo_start = pl.multiple_of(jnp.maximum(r0 - SSD_HALO, 0), SSD_HALO)
        halo = xbc_ref[pl.ds(halo_start, SSD_HALO), :]
        ext_ref[0:SSD_HALO, :] = jnp.where(c > 0, halo, jnp.zeros_like(halo))
        ext_ref[SSD_HALO:, :] = xbc_ref[rows, :]
        conv = cb_ref[...]
        for w in range(SSD_CONV):
            o = SSD_HALO - (SSD_CONV - 1) + w
            conv = conv + cw_ref[w:w + 1, :] * ext_ref[o:o + L, :]
        xact = _silu(conv)
        xs = xact[:, :SSD_WIDTH]
        bm = [xact[:, SSD_WIDTH + g * SSD_STATE:SSD_WIDTH + (g + 1) * SSD_STATE] for g in range(SSD_GROUPS)]
        cm = [xact[:, SSD_WIDTH + n_b + g * SSD_STATE:SSD_WIDTH + n_b + (g + 1) * SSD_STATE]
              for g in range(SSD_GROUPS)]
        bm16 = [t.astype(BF16) for t in bm]
        cm16 = [t.astype(BF16) for t in cm]

        dt = jax.nn.softplus(dt_ref[rows, :] + dtb_ref[...])
        a = dt * a_neg
        acs = jnp.dot(cumsum_mat, a, precision=lax.Precision.HIGHEST, preferred_element_type=F32)
        acs_t = acs.T
        dt_t = dt.T
        cb = [lax.dot_general(cm16[g], bm16[g], (((1,), (1,)), ((), ())), preferred_element_type=F32)
              for g in range(SSD_GROUPS)]

        y_diag, e_pairs, w_pairs, d_pairs = [], [], [], []
        for p in range(SSD_HEADS // 2):
            xs_pair = xs[:, p * LANES:(p + 1) * LANES].astype(BF16)
            yd, ecol, wcol, dcol = [], [], [], []
            for h in (2 * p, 2 * p + 1):
                g = h // HEADS_PER_GROUP
                acs_col = jnp.broadcast_to(acs[:, h:h + 1], (L, L))
                dt_col = jnp.broadcast_to(dt[:, h:h + 1], (L, L))
                seg = acs_col - acs_t[h:h + 1, :]
                decay = jnp.exp(jnp.where(tril, seg, -jnp.inf))
                mix = (cb[g] * decay * dt_t[h:h + 1, :]).astype(BF16)
                yd.append(jnp.dot(mix, xs_pair, preferred_element_type=F32))
                a_last = acs_col[L - 1:L, :]
                ecol.append(jnp.exp(acs_col))
                wcol.append(jnp.exp(a_last - acs_col) * dt_col)
                dcol.append(jnp.exp(a_last))
            y_diag.append(jnp.where(first_head, yd[0], yd[1]))
            e_pairs.append(jnp.where(first_head, ecol[0], ecol[1]))
            w_pairs.append(jnp.where(first_head, wcol[0], wcol[1]))
            d_pairs.append(jnp.where(first_head, dcol[0], dcol[1]))
        y_diag = jnp.concatenate(y_diag, axis=1)
        exp_acs = jnp.concatenate(e_pairs, axis=1)
        to_end = jnp.concatenate(w_pairs, axis=1)
        chunk_decay = jnp.concatenate(d_pairs, axis=1)

        state = state_ref[...]
        st16 = state.astype(BF16)
        y_off = jnp.where(first_group,
                          jnp.dot(cm16[0], st16, preferred_element_type=F32),
                          jnp.dot(cm16[1], st16, preferred_element_type=F32)) * exp_acs
        xdd = (xs * to_end).astype(BF16)
        new = jnp.where(first_group,
                        jnp.dot(bm[0].T.astype(BF16), xdd, preferred_element_type=F32),
                        jnp.dot(bm[1].T.astype(BF16), xdd, preferred_element_type=F32))
        state_ref[...] = state * chunk_decay + new

        y = y_diag + y_off + dsk_ref[...] * xs
        y = y * _silu(z_ref[rows, :])
        ysq = y * y
        s0 = jnp.sum(jnp.where(first_group, ysq, 0.0), axis=-1, keepdims=True)
        s1 = jnp.sum(jnp.where(first_group, 0.0, ysq), axis=-1, keepdims=True)
        ms = jnp.where(first_group, s0, s1) * (1.0 / GROUP_LANES)
        o_ref[rows, :] = (y * lax.rsqrt(ms + RMS_EPS) * ng_ref[...]).astype(o_ref.dtype)
        return carry

    lax.fori_loop(0, n_chunks, chunk, 0)


def _ssd(z, xbc, dt, conv_w, conv_b, dt_bias, a_log, d_skip, norm_g):
    b, s, _ = z.shape
    pad = lambda v: jnp.pad(v, (0, DT_PAD - SSD_HEADS)).reshape(1, DT_PAD)
    seq = lambda w: pl.BlockSpec((None, s, w), lambda bi: (bi, 0, 0))
    full = lambda r, w: pl.BlockSpec((r, w), lambda bi: (0, 0))
    return pl.pallas_call(
        _ssd_kernel,
        grid=(b,),
        in_specs=[seq(SSD_WIDTH), seq(SSD_CONV_DIM), seq(DT_PAD),
                  full(SSD_CONV, SSD_CONV_DIM), full(1, SSD_CONV_DIM), full(1, DT_PAD), full(1, DT_PAD),
                  full(1, SSD_WIDTH), full(1, SSD_WIDTH)],
        out_specs=seq(SSD_WIDTH),
        out_shape=jax.ShapeDtypeStruct((b, s, SSD_WIDTH), BF16),
        scratch_shapes=[pltpu.VMEM((SSD_STATE, SSD_WIDTH), F32),
                        pltpu.VMEM((SSD_HALO + SSD_CHUNK, SSD_CONV_DIM), F32)],
        compiler_params=_params(1),
        name="ssd",
    )(z, xbc, dt, conv_w, conv_b.reshape(1, SSD_CONV_DIM), pad(dt_bias), pad(a_log),
      jnp.repeat(d_skip, SSD_HEADDIM).reshape(1, SSD_WIDTH), norm_g.reshape(1, SSD_WIDTH))


SGU_ROWS = 512


def _sgu_kernel(uv_ref, lng_ref, lnb_ref, w_ref, bs_ref, o_ref):
    uv = uv_ref[...]
    act = 0.5 * uv * (1.0 + lax.erf(uv * (1.0 / math.sqrt(2.0))))
    u = act[:, :SGU_WIDTH]
    v = act[:, SGU_WIDTH:]
    mu = jnp.mean(v, axis=-1, keepdims=True)
    var = jnp.mean(jnp.square(v - mu), axis=-1, keepdims=True)
    vn = (v - mu) * lax.rsqrt(var + LN_EPS) * lng_ref[...] + lnb_ref[...]
    row = lax.broadcasted_iota(jnp.int32, (SGU_CHUNK, SGU_CHUNK), 0)
    col = lax.broadcasted_iota(jnp.int32, (SGU_CHUNK, SGU_CHUNK), 1)
    w = [jnp.where(row >= col, w_ref[g], 0.0).astype(BF16) for g in range(SGU_GROUPS)]
    lane = lax.broadcasted_iota(jnp.int32, (1, LANES), 1)
    first = lane < SGU_GROUP_DIM
    for c in range(SGU_ROWS // SGU_CHUNK):
        rows = slice(c * SGU_CHUNK, (c + 1) * SGU_CHUNK)
        mixed = []
        for p in range(SGU_WIDTH // LANES):
            vp = vn[rows, p * LANES:(p + 1) * LANES]
            lo = jnp.where(first, vp, 0.0).astype(BF16)
            hi = jnp.where(first, 0.0, vp).astype(BF16)
            mixed.append(jnp.dot(w[2 * p], lo, preferred_element_type=F32)
                         + jnp.dot(w[2 * p + 1], hi, preferred_element_type=F32))
        mixed = jnp.concatenate(mixed, axis=1) + bs_ref[...]
        o_ref[rows, :] = (u[rows, :] * mixed).astype(o_ref.dtype)


def _sgu(uv, ln_g, ln_b, w_s, b_s):
    b, s, _ = uv.shape
    bias = jnp.repeat(b_s.T, SGU_GROUP_DIM, axis=1)
    vec = pl.BlockSpec((1, SGU_WIDTH), lambda bi, r: (0, 0))
    return pl.pallas_call(
        _sgu_kernel,
        grid=(b, s // SGU_ROWS),
        in_specs=[pl.BlockSpec((None, SGU_ROWS, UV_WIDTH), lambda bi, r: (bi, r, 0)), vec, vec,
                  pl.BlockSpec((SGU_GROUPS, SGU_CHUNK, SGU_CHUNK), lambda bi, r: (0, 0, 0)),
                  pl.BlockSpec((SGU_CHUNK, SGU_WIDTH), lambda bi, r: (0, 0))],
        out_specs=pl.BlockSpec((None, SGU_ROWS, SGU_WIDTH), lambda bi, r: (bi, r, 0)),
        out_shape=jax.ShapeDtypeStruct((b, s, SGU_WIDTH), BF16),
        compiler_params=_params(2),
        name="sgu",
    )(uv, ln_g.reshape(1, SGU_WIDTH), ln_b.reshape(1, SGU_WIDTH), w_s, bias)


def _outproj_kernel(x_ref, ya_ref, ys_ref, yg_ref, wa_ref, ws_ref, wg_ref, o_ref):
    y = (jnp.dot(ya_ref[...].astype(BF16), wa_ref[...], preferred_element_type=F32)
         + jnp.dot(ys_ref[...], ws_ref[...], preferred_element_type=F32)
         + jnp.dot(yg_ref[...], wg_ref[...], preferred_element_type=F32))
    o_ref[...] = x_ref[...] + y


def _outproj(x, y_att, y_ssd, y_sgu, w_out):
    m = x.shape[0]
    w = w_out.astype(BF16)
    row = lambda n: pl.BlockSpec((ROW_TILE, n), lambda i: (i, 0))
    full = lambda n: pl.BlockSpec((n, D_MODEL), lambda i: (0, 0))
    return pl.pallas_call(
        _outproj_kernel,
        grid=(m // ROW_TILE,),
        in_specs=[row(D_MODEL), row(ATT_WIDTH), row(SSD_WIDTH), row(SGU_WIDTH),
                  full(ATT_WIDTH), full(SSD_WIDTH), full(SGU_WIDTH)],
        out_specs=row(D_MODEL),
        out_shape=jax.ShapeDtypeStruct((m, D_MODEL), F32),
        compiler_params=_params(1),
        name="outproj",
    )(x, y_att, y_ssd, y_sgu, w[:ATT_WIDTH], w[ATT_WIDTH:ATT_WIDTH + SSD_WIDTH], w[ATT_WIDTH + SSD_WIDTH:])


def _mixing(x, b, s, gain, w_in, conv_w, conv_b, dt_bias, a_log, d_skip, ssd_norm,
            sgu_ln_g, sgu_ln_b, sgu_w, sgu_b, w_out):
    qkv, z, xbc, uv, dt = _inproj(x, gain, w_in)
    seq = lambda t: t.reshape(b, s, t.shape[-1])
    y_att = _attention(seq(qkv))
    y_ssd = _ssd(seq(z), seq(xbc), seq(dt), conv_w, conv_b, dt_bias, a_log, d_skip, ssd_norm)
    y_sgu = _sgu(seq(uv), sgu_ln_g, sgu_ln_b, sgu_w, sgu_b)
    flat = lambda t: t.reshape(b * s, t.shape[-1])
    return _outproj(x, flat(y_att), flat(y_ssd), flat(y_sgu), w_out)


def kernel(x, ffn1_norm, ffn1_w_gate, ffn1_w_up, ffn1_w_down, mix_norm, w_in, conv_w, conv_b, dt_bias, a_log, d_skip, ssd_norm, sgu_ln_g, sgu_ln_b, sgu_w, sgu_b, w_out, ffn2_norm, ffn2_w_gate, ffn2_w_up, ffn2_w_down, final_norm):
    b, s, d = x.shape
    depth = ffn1_norm.shape[0]
    h = x.reshape(b * s, d)
    for i in range(depth):
        h = _ffn(h, ffn1_norm[i], ffn1_w_gate[i], ffn1_w_up[i], ffn1_w_down[i])
        h = _mixing(h, b, s, mix_norm[i], w_in[i], conv_w[i], conv_b[i], dt_bias[i], a_log[i], d_skip[i],
                    ssd_norm[i], sgu_ln_g[i], sgu_ln_b[i], sgu_w[i], sgu_b[i], w_out[i])
        h = _ffn(h, ffn2_norm[i], ffn2_w_gate[i], ffn2_w_up[i], ffn2_w_down[i],
                 final_gain=final_norm if i == depth - 1 else None)
    return h.reshape(b, s, d)
```

```python
import functools
import math

import numpy as np
import jax
import jax.numpy as jnp
from jax import lax
from jax.experimental import pallas as pl
from jax.experimental.pallas import tpu as pltpu

F32 = jnp.float32
BF16 = jnp.bfloat16

D_MODEL = 1024
D_FF = 2816
HEAD_DIM = 64
ATT_HEADS = 6
ATT_WIDTH = ATT_HEADS * HEAD_DIM
DILATED_PAIRS = ((128, 1), (512, 4), (2048, 16))
SSD_HEADS = 6
SSD_HEADDIM = 64
SSD_WIDTH = SSD_HEADS * SSD_HEADDIM
SSD_GROUPS = 2
SSD_STATE = 128
SSD_CONV = 4
SSD_CHUNK = 128
SSD_CONV_DIM = SSD_WIDTH + 2 * SSD_GROUPS * SSD_STATE
SGU_GROUPS = 4
SGU_GROUP_DIM = 64
SGU_WIDTH = SGU_GROUPS * SGU_GROUP_DIM
SGU_CHUNK = 128
RMS_EPS = 1e-6
LN_EPS = 1e-5

LANES = 128
DT_PAD = LANES
QKV_WIDTH = 3 * ATT_WIDTH
UV_WIDTH = 2 * SGU_WIDTH
PROJ_WIDTH = QKV_WIDTH + SSD_WIDTH + SSD_CONV_DIM + UV_WIDTH + DT_PAD

VMEM_LIMIT = 56 * 1024 * 1024

ROW_TILE = 512
FF_CHUNK = 256
PROJ_CHUNK = 512

ATT_BLOCK = 128


def _params(n_axes):
    return pltpu.CompilerParams(dimension_semantics=("arbitrary",) * n_axes,
                                vmem_limit_bytes=VMEM_LIMIT)


def _rmsnorm_f32(x, g):
    ms = jnp.mean(x * x, axis=-1, keepdims=True)
    return x * lax.rsqrt(ms + RMS_EPS) * g


def _silu(x):
    return x * jax.nn.sigmoid(x)


def _ffn_kernel(*refs, mixed, final_norm):
    refs = list(refs)
    x_ref = refs.pop(0)
    if mixed:
        ya_ref, ys_ref, yg_ref, wa_ref, ws_ref, wm_ref = refs[:6]
        del refs[:6]
    g_ref, wg_ref, wu_ref, wd_ref = refs[:4]
    del refs[:4]
    fg_ref = refs.pop(0) if final_norm else None
    o_ref, xn_ref, h_ref, res_ref = refs
    x = x_ref[...]
    if mixed:
        x = x + (jnp.dot(ya_ref[...].astype(BF16), wa_ref[...], preferred_element_type=F32)
                 + jnp.dot(ys_ref[...], ws_ref[...], preferred_element_type=F32)
                 + jnp.dot(yg_ref[...], wm_ref[...], preferred_element_type=F32))
    res_ref[...] = x
    xn_ref[...] = _rmsnorm_f32(x, g_ref[...]).astype(BF16)
    for f in range(D_FF // FF_CHUNK):
        cols = slice(f * FF_CHUNK, (f + 1) * FF_CHUNK)
        xn = xn_ref[...]
        gate = jnp.dot(xn, wg_ref[:, cols], preferred_element_type=F32)
        up = jnp.dot(xn, wu_ref[:, cols], preferred_element_type=F32)
        h_ref[:, cols] = (_silu(gate) * up).astype(BF16)
    y = jnp.dot(h_ref[...], wd_ref[...], preferred_element_type=F32)
    out = res_ref[...] + 0.5 * y
    if final_norm:
        out = _rmsnorm_f32(out, fg_ref[...])
    o_ref[...] = out


def _ffn(x, gain, w_gate, w_up, w_down, mix=None, final_gain=None):
    m = x.shape[0]
    row = lambda n: pl.BlockSpec((ROW_TILE, n), lambda i: (i, 0))
    full = lambda r, c: pl.BlockSpec((r, c), lambda i: (0, 0))
    in_specs, args = [row(D_MODEL)], [x]
    if mix is not None:
        y_att, y_ssd, y_sgu, w_out = mix
        w = w_out.astype(BF16)
        in_specs += [row(ATT_WIDTH), row(SSD_WIDTH), row(SGU_WIDTH),
                     full(ATT_WIDTH, D_MODEL), full(SSD_WIDTH, D_MODEL), full(SGU_WIDTH, D_MODEL)]
        args += [y_att, y_ssd, y_sgu, w[:ATT_WIDTH], w[ATT_WIDTH:ATT_WIDTH + SSD_WIDTH], w[ATT_WIDTH + SSD_WIDTH:]]
    in_specs += [full(1, D_MODEL), full(D_MODEL, D_FF), full(D_MODEL, D_FF), full(D_FF, D_MODEL)]
    args += [gain.reshape(1, D_MODEL), w_gate.astype(BF16), w_up.astype(BF16), w_down.astype(BF16)]
    if final_gain is not None:
        in_specs.append(full(1, D_MODEL))
        args.append(final_gain.reshape(1, D_MODEL))
    return pl.pallas_call(
        functools.partial(_ffn_kernel, mixed=mix is not None, final_norm=final_gain is not None),
        grid=(m // ROW_TILE,),
        in_specs=in_specs,
        out_specs=row(D_MODEL),
        out_shape=jax.ShapeDtypeStruct((m, D_MODEL), F32),
        scratch_shapes=[pltpu.VMEM((ROW_TILE, D_MODEL), BF16), pltpu.VMEM((ROW_TILE, D_FF), BF16),
                        pltpu.VMEM((ROW_TILE, D_MODEL), F32)],
        compiler_params=_params(1),
        name="ffn",
    )(*args)


_PROJ_PIECES = (("qkv", QKV_WIDTH, F32),("z", SSD_WIDTH, F32), ("xbc", SSD_CONV_DIM, F32),
                ("uv", UV_WIDTH, F32), ("dt", DT_PAD, F32))


def _inproj_kernel(x_ref, g_ref, w_ref, qkv_ref, z_ref, xbc_ref, uv_ref, dt_ref, xn_ref):
    outs = (qkv_ref, z_ref, xbc_ref, uv_ref, dt_ref)
    xn_ref[...] = _rmsnorm_f32(x_ref[...], g_ref[...]).astype(BF16)
    starts = np.cumsum([0] + [p[1] for p in _PROJ_PIECES])
    for c in range(PROJ_WIDTH // PROJ_CHUNK):
        lo, hi = c * PROJ_CHUNK, (c + 1) * PROJ_CHUNK
        r = jnp.dot(xn_ref[...], w_ref[:, lo:hi], preferred_element_type=F32)
        for k, o_ref in enumerate(outs):
            a, b = max(lo, int(starts[k])), min(hi, int(starts[k + 1]))
            if a < b:
                o_ref[:, a - int(starts[k]):b - int(starts[k])] = r[:, a - lo:b - lo].astype(o_ref.dtype)


def _inproj(x, gain, w_in):
    m = x.shape[0]
    o = np.cumsum([0, ATT_WIDTH, ATT_WIDTH, ATT_WIDTH, SSD_WIDTH, SSD_CONV_DIM, SSD_HEADS, UV_WIDTH])
    w_qkv, w_z, w_xbc, w_dt, w_uv = (w_in[:, o[0]:o[3]], w_in[:, o[3]:o[4]], w_in[:, o[4]:o[5]],
                                     w_in[:, o[5]:o[6]], w_in[:, o[6]:o[7]])
    w_dt = jnp.pad(w_dt, ((0, 0), (0, DT_PAD - SSD_HEADS)))
    w_all = jnp.concatenate([w_qkv, w_z, w_xbc, w_uv, w_dt], axis=1).astype(BF16)
    row = lambda w: pl.BlockSpec((ROW_TILE, w), lambda i: (i, 0))
    return pl.pallas_call(
        _inproj_kernel,
        grid=(m // ROW_TILE,),
        in_specs=[row(D_MODEL), pl.BlockSpec((1, D_MODEL), lambda i: (0, 0)),
                  pl.BlockSpec((D_MODEL, PROJ_WIDTH), lambda i: (0, 0))],
        out_specs=[row(w) for _, w, _ in _PROJ_PIECES],
        out_shape=[jax.ShapeDtypeStruct((m, w), dt) for _, w, dt in _PROJ_PIECES],
        scratch_shapes=[pltpu.VMEM((ROW_TILE, D_MODEL), BF16)],
        compiler_params=_params(1),
        name="inproj",
    )(x, gain.reshape(1, D_MODEL), w_all)


NAT, P4, P16 = 0, 1, 2


def _att_kernel(q_ref, k_ref, v_ref, o_ref, qa_ref, qb_ref, kk_ref, ve_ref,
                acc1_ref, m1_ref, l1_ref, acc3_ref, m3_ref, l3_ref, band_ref, cur_ref):
    seq = q_ref.shape[0]
    T = ATT_BLOCK
    d4, d16 = DILATED_PAIRS[1][1], DILATED_PAIRS[2][1]
    sub4 = seq // d4
    lane = lax.broadcasted_iota(jnp.int32, (1, LANES), 1)
    first = lane < HEAD_DIM
    qi = lax.broadcasted_iota(jnp.int32, (T, T), 0)
    kj = lax.broadcasted_iota(jnp.int32, (T, T), 1)
    cur_bias = jnp.where(kj <= qi, 0.0, -jnp.inf).astype(F32)
    prev_bias = jnp.where(kj >= qi, 0.0, -jnp.inf).astype(F32)
    for half in range(2):
        cur_ref[half * T:(half + 1) * T, :] = cur_bias
        band_ref[half * T:(half + 1) * T, 0:T] = prev_bias
        band_ref[half * T:(half + 1) * T, T:2 * T] = cur_bias
    q_scale = HEAD_DIM ** -0.5 * math.log2(math.e)

    def prep(layout, dst, src):
        q = q_ref[src, :] * q_scale
        qa_ref[layout, dst, :] = jnp.where(first, q, 0.0).astype(BF16)
        qb_ref[layout, dst, :] = jnp.where(first, 0.0, q).astype(BF16)
        kk_ref[layout, dst, :] = k_ref[src, :].astype(BF16)
        ve_ref[layout, dst, 0:LANES] = v_ref[src, :].astype(BF16)
        ve_ref[layout, dst, LANES:2 * LANES] = jnp.ones((T, LANES), BF16)

    def prep_all(c, carry):
        rows = pl.ds(pl.multiple_of(c * T, T), T)
        prep(NAT, rows, rows)
        prep(P16, rows, pl.ds(c, T, stride=d16))
        prep(P4, rows, pl.ds(c // d4 + (c % d4) * (T * d4), T, stride=d4))
        return carry

    lax.fori_loop(0, seq // T, prep_all, 0)

    def block(layout, qrows, krows, bias_ref):
        q2 = jnp.concatenate([qa_ref[layout, qrows, :], qb_ref[layout, qrows, :]], axis=0)
        s = lax.dot_general(q2, kk_ref[layout, krows, :], (((1,), (1,)), ((), ())),
                            preferred_element_type=F32) + bias_ref[...]
        m = jnp.max(s, axis=-1, keepdims=True)
        p = jnp.exp2(s - m).astype(BF16)
        r = jnp.dot(p, ve_ref[layout, krows, :], preferred_element_type=F32)
        acc = jnp.where(first, r[0:T, 0:LANES], r[T:2 * T, 0:LANES])
        lsum = jnp.where(first, r[0:T, LANES:2 * LANES], r[T:2 * T, LANES:2 * LANES])
        return acc, jnp.where(first, m[0:T], m[T:2 * T]), lsum

    def rows_of(start, n=T):
        return pl.ds(start, n)

    def store1(rows, acc, mb, lsum):
        acc1_ref[rows, :] = acc
        m1_ref[rows, :] = mb
        l1_ref[rows, :] = lsum

    store1(rows_of(0), *block(NAT, rows_of(0), rows_of(0), cur_ref))
    for n in range(1, seq // T):
        store1(rows_of(n * T), *block(NAT, rows_of(n * T), rows_of((n - 1) * T, 2 * T), band_ref))

    for r16 in range(d16):
        rows = rows_of(r16 * T)
        acc, mb, lsum = block(P16, rows, rows, cur_ref)
        dst = pl.ds((r16 % d4) * sub4 + r16 // d4, T, stride=d4)
        acc3_ref[dst, :] = acc
        m3_ref[dst, :] = mb
        l3_ref[dst, :] = lsum

    def finish(r4, n, keys, bias_ref):
        prow = rows_of(r4 * sub4 + n * T)
        acc2, mb2, l2 = block(P4, prow, keys, bias_ref)
        trow = pl.ds(r4 + n * (T * d4), T, stride=d4)
        acc1, mb1, l1 = acc1_ref[trow, :], m1_ref[trow, :], l1_ref[trow, :]
        acc3, mb3, l3 = acc3_ref[prow, :], m3_ref[prow, :], l3_ref[prow, :]
        m = jnp.maximum(mb1, jnp.maximum(mb2, mb3))
        w1, w2, w3 = jnp.exp2(mb1 - m), jnp.exp2(mb2 - m), jnp.exp2(mb3 - m)
        num = w1 * acc1 + w2 * acc2 + w3 * acc3
        den = w1 * l1 + w2 * l2 + w3 * l3
        o_ref[trow, :] = num / den

    for r4 in range(d4):
        finish(r4, 0, rows_of(r4 * sub4), cur_ref)
        for n in range(1, sub4 // T):
            finish(r4, n, rows_of(r4 * sub4 + (n - 1) * T, 2 * T), band_ref)


def _attention(qkv):
    b, s, _ = qkv.shape
    for window, dil in DILATED_PAIRS:
        assert window // dil == ATT_BLOCK and s % (ATT_BLOCK * dil) == 0
    assert DILATED_PAIRS[0][1] == 1 and DILATED_PAIRS[2][1] == DILATED_PAIRS[1][1] ** 2
    n_pairs = ATT_WIDTH // LANES
    spec = lambda part: pl.BlockSpec((None, s, LANES), lambda bi, hp: (bi, 0, part * n_pairs + hp))
    return pl.pallas_call(
        _att_kernel,
        grid=(b, n_pairs),
        in_specs=[spec(0), spec(1), spec(2)],
        out_specs=pl.BlockSpec((None, s, LANES), lambda bi, hp: (bi, 0, hp)),
        out_shape=jax.ShapeDtypeStruct((b, s, ATT_WIDTH), F32),
        scratch_shapes=[pltpu.VMEM((3, s, LANES), BF16)] * 3 + [pltpu.VMEM((3, s, 2 * LANES), BF16)]
        + [pltpu.VMEM((s, LANES), F32)] * 6
        + [pltpu.VMEM((2 * ATT_BLOCK, 2 * ATT_BLOCK), F32), pltpu.VMEM((2 * ATT_BLOCK, ATT_BLOCK), F32)],
        compiler_params=_params(2),
        name="dilated_attention",
    )(qkv, qkv, qkv)


SSD_HALO = 8
HEADS_PER_GROUP = SSD_HEADS // SSD_GROUPS
GROUP_LANES = HEADS_PER_GROUP * SSD_HEADDIM


def _ssd_kernel(z_ref, xbc_ref, dt_ref, cw_ref, cb_ref, dtb_ref, alog_ref, dsk_ref, ng_ref,
                o_ref, state_ref, ext_ref):
    n_chunks = z_ref.shape[0] // SSD_CHUNK
    L = SSD_CHUNK
    state_ref[...] = jnp.zeros_like(state_ref)
    row = lax.broadcasted_iota(jnp.int32, (L, L), 0)
    col = lax.broadcasted_iota(jnp.int32, (L, L), 1)
    tril = row >= col
    cumsum_mat = tril.astype(F32)
    lane = lax.broadcasted_iota(jnp.int32, (1, LANES), 1)
    lane_w = lax.broadcasted_iota(jnp.int32, (1, SSD_WIDTH), 1)
    first_group = lane_w < GROUP_LANES
    first_head = lane < SSD_HEADDIM
    a_neg = -jnp.exp(alog_ref[...])
    n_b = SSD_GROUPS * SSD_STATE

    def chunk(c, carry):
        r0 = pl.multiple_of(c * L, L)
        rows = pl.ds(r0, L)
        halo_start = pl.multiple_of(jnp.maximum(r0 - SSD_HALO, 0), SSD_HALO)
        halo = xbc_ref[pl.ds(halo_start, SSD_HALO), :]
        ext_ref[0:SSD_HALO, :] = jnp.where(c > 0, halo, jnp.zeros_like(halo))
        ext_ref[SSD_HALO:, :] = xbc_ref[rows, :]
        conv = cb_ref[...]
        for w in range(SSD_CONV):
            o = SSD_HALO - (SSD_CONV - 1) + w
            conv = conv + cw_ref[w:w + 1, :] * ext_ref[o:o + L, :]
        xact = _silu(conv)
        xs = xact[:, :SSD_WIDTH]
        bm = [xact[:, SSD_WIDTH + g * SSD_STATE:SSD_WIDTH + (g + 1) * SSD_STATE] for g in range(SSD_GROUPS)]
        cm = [xact[:, SSD_WIDTH + n_b + g * SSD_STATE:SSD_WIDTH + n_b + (g + 1) * SSD_STATE]
              for g in range(SSD_GROUPS)]
        bmt16 = [t.T.astype(BF16) for t in bm]
        cm16 = [t.astype(BF16) for t in cm]

        dt = jax.nn.softplus(dt_ref[rows, :] + dtb_ref[...])
        a = dt * a_neg
        acs = jnp.dot(cumsum_mat, a, precision=lax.Precision.HIGHEST, preferred_element_type=F32)
        acs_t = acs.T
        dt_t = dt.T
        acs_last = acs[L - 1:L, :]
        exp_acs_h = jnp.exp(acs)
        to_end_h = jnp.exp(acs_last - acs) * dt
        chunk_decay_h = jnp.exp(acs_last)
        cb = [jnp.dot(cm16[g], bmt16[g], preferred_element_type=F32) for g in range(SSD_GROUPS)]

        y_diag, e_pairs, w_pairs, d_pairs = [], [], [], []
        for p in range(SSD_HEADS // 2):
            xs_pair = xs[:, p * LANES:(p + 1) * LANES].astype(BF16)
            yd, ecol, wcol, dcol = [], [], [], []
            for h in (2 * p, 2 * p + 1):
                g = h // HEADS_PER_GROUP
                acs_col = jnp.broadcast_to(acs[:, h:h + 1], (L, L))
                seg = acs_col - acs_t[h:h + 1, :]
                decay = jnp.exp(jnp.where(tril, seg, -jnp.inf))
                mix = (cb[g] * decay * dt_t[h:h + 1, :]).astype(BF16)
                yd.append(jnp.dot(mix, xs_pair, preferred_element_type=F32))
                ecol.append(jnp.broadcast_to(exp_acs_h[:, h:h + 1], (L, LANES)))
                wcol.append(jnp.broadcast_to(to_end_h[:, h:h + 1], (L, LANES)))
                dcol.append(jnp.broadcast_to(chunk_decay_h[:, h:h + 1], (1, LANES)))
            y_diag.append(jnp.where(first_head, yd[0], yd[1]))
            e_pairs.append(jnp.where(first_head, ecol[0], ecol[1]))
            w_pairs.append(jnp.where(first_head, wcol[0], wcol[1]))
            d_pairs.append(jnp.where(first_head, dcol[0], dcol[1]))
        y_diag = jnp.concatenate(y_diag, axis=1)
        exp_acs = jnp.concatenate(e_pairs, axis=1)
        to_end = jnp.concatenate(w_pairs, axis=1)
        chunk_decay = jnp.concatenate(d_pairs, axis=1)

        state = state_ref[...]
        st16 = state.astype(BF16)
        y_off = jnp.where(first_group,
                          jnp.dot(cm16[0], st16, preferred_element_type=F32),
                          jnp.dot(cm16[1], st16, preferred_element_type=F32)) * exp_acs
        xdd = (xs * to_end).astype(BF16)
        new = jnp.where(first_group,
                        jnp.dot(bmt16[0], xdd, preferred_element_type=F32),
                        jnp.dot(bmt16[1], xdd, preferred_element_type=F32))
        state_ref[...] = state * chunk_decay + new

        y = y_diag + y_off + dsk_ref[...] * xs
        y = y * _silu(z_ref[rows, :])
        ysq = y * y
        s0 = jnp.sum(jnp.where(first_group, ysq, 0.0), axis=-1, keepdims=True)
        s1 = jnp.sum(jnp.where(first_group, 0.0, ysq), axis=-1, keepdims=True)
        ms = jnp.where(first_group, s0, s1) * (1.0 / GROUP_LANES)
        o_ref[rows, :] = (y * lax.rsqrt(ms + RMS_EPS) * ng_ref[...]).astype(o_ref.dtype)
        return carry

    lax.fori_loop(0, n_chunks, chunk, 0, unroll=2)


def _ssd(z, xbc, dt, conv_w, conv_b, dt_bias, a_log, d_skip, norm_g):
    b, s, _ = z.shape
    pad = lambda v: jnp.pad(v, (0, DT_PAD - SSD_HEADS)).reshape(1, DT_PAD)
    seq = lambda w: pl.BlockSpec((None, s, w), lambda bi: (bi, 0, 0))
    full = lambda r, w: pl.BlockSpec((r, w), lambda bi: (0, 0))
    return pl.pallas_call(
        _ssd_kernel,
        grid=(b,),
        in_specs=[seq(SSD_WIDTH), seq(SSD_CONV_DIM), seq(DT_PAD),
                  full(SSD_CONV, SSD_CONV_DIM), full(1, SSD_CONV_DIM), full(1, DT_PAD), full(1, DT_PAD),
                  full(1, SSD_WIDTH), full(1, SSD_WIDTH)],
        out_specs=seq(SSD_WIDTH),
        out_shape=jax.ShapeDtypeStruct((b, s, SSD_WIDTH), BF16),
        scratch_shapes=[pltpu.VMEM((SSD_STATE, SSD_WIDTH), F32),
                        pltpu.VMEM((SSD_HALO + SSD_CHUNK, SSD_CONV_DIM), F32)],
        compiler_params=_params(1),
        name="ssd",
    )(z, xbc, dt, conv_w, conv_b.reshape(1, SSD_CONV_DIM), pad(dt_bias), pad(a_log),
      jnp.repeat(d_skip, SSD_HEADDIM).reshape(1, SSD_WIDTH), norm_g.reshape(1, SSD_WIDTH))


SGU_ROWS = 512


def _sgu_kernel(uv_ref, lng_ref, lnb_ref, w_ref, bs_ref, o_ref):
    uv = uv_ref[...]
    act = 0.5 * uv * (1.0 + lax.erf(uv * (1.0 / math.sqrt(2.0))))
    u = act[:, :SGU_WIDTH]
    v = act[:, SGU_WIDTH:]
    mu = jnp.mean(v, axis=-1, keepdims=True)
    var = jnp.mean(jnp.square(v - mu), axis=-1, keepdims=True)
    vn = (v - mu) * lax.rsqrt(var + LN_EPS) * lng_ref[...] + lnb_ref[...]
    row = lax.broadcasted_iota(jnp.int32, (SGU_CHUNK, SGU_CHUNK), 0)
    col = lax.broadcasted_iota(jnp.int32, (SGU_CHUNK, SGU_CHUNK), 1)
    w = [jnp.where(row >= col, w_ref[g], 0.0).astype(BF16) for g in range(SGU_GROUPS)]
    lane = lax.broadcasted_iota(jnp.int32, (1, LANES), 1)
    first = lane < SGU_GROUP_DIM
    for c in range(SGU_ROWS // SGU_CHUNK):
        rows = slice(c * SGU_CHUNK, (c + 1) * SGU_CHUNK)
        mixed = []
        for p in range(SGU_WIDTH // LANES):
            vp = vn[rows, p * LANES:(p + 1) * LANES]
            lo = jnp.where(first, vp, 0.0).astype(BF16)
            hi = jnp.where(first, 0.0, vp).astype(BF16)
            mixed.append(jnp.dot(w[2 * p], lo, preferred_element_type=F32)
                         + jnp.dot(w[2 * p + 1], hi, preferred_element_type=F32))
        mixed = jnp.concatenate(mixed, axis=1) + bs_ref[...]
        o_ref[rows, :] = (u[rows, :] * mixed).astype(o_ref.dtype)


def _sgu(uv, ln_g, ln_b, w_s, b_s):
    b, s, _ = uv.shape
    bias = jnp.repeat(b_s.T, SGU_GROUP_DIM, axis=1)
    vec = pl.BlockSpec((1, SGU_WIDTH), lambda bi, r: (0, 0))
    return pl.pallas_call(
        _sgu_kernel,
        grid=(b, s // SGU_ROWS),
        in_specs=[pl.BlockSpec((None, SGU_ROWS, UV_WIDTH), lambda bi, r: (bi, r, 0)), vec, vec,
                  pl.BlockSpec((SGU_GROUPS, SGU_CHUNK, SGU_CHUNK), lambda bi, r: (0, 0, 0)),
                  pl.BlockSpec((SGU_CHUNK, SGU_WIDTH), lambda bi, r: (0, 0))],
        out_specs=pl.BlockSpec((None, SGU_ROWS, SGU_WIDTH), lambda bi, r: (bi, r, 0)),
        out_shape=jax.ShapeDtypeStruct((b, s, SGU_WIDTH), BF16),
        compiler_params=_params(2),
        name="sgu",
    )(uv, ln_g.reshape(1, SGU_WIDTH), ln_b.reshape(1, SGU_WIDTH), w_s, bias)


def _mixers(x, b, s, gain, w_in, conv_w, conv_b, dt_bias, a_log, d_skip, ssd_norm,
            sgu_ln_g, sgu_ln_b, sgu_w, sgu_b):
    qkv, z, xbc, uv, dt = _inproj(x, gain, w_in)
    seq = lambda t: t.reshape(b, s, t.shape[-1])
    y_att = _attention(seq(qkv))
    y_ssd = _ssd(seq(z), seq(xbc), seq(dt), conv_w, conv_b, dt_bias, a_log, d_skip, ssd_norm)
    y_sgu = _sgu(seq(uv), sgu_ln_g, sgu_ln_b, sgu_w, sgu_b)
    flat = lambda t: t.reshape(b * s, t.shape[-1])
    return flat(y_att), flat(y_ssd), flat(y_sgu)


def kernel(x, ffn1_norm, ffn1_w_gate, ffn1_w_up, ffn1_w_down, mix_norm, w_in, conv_w, conv_b, dt_bias, a_log, d_skip, ssd_norm, sgu_ln_g, sgu_ln_b, sgu_w, sgu_b, w_out, ffn2_norm, ffn2_w_gate, ffn2_w_up, ffn2_w_down, final_norm):
    b, s, d = x.shape
    depth = ffn1_norm.shape[0]
    h = x.reshape(b * s, d)
    for i in range(depth):
        h = _ffn(h, ffn1_norm[i], ffn1_w_gate[i], ffn1_w_up[i], ffn1_w_down[i])
        ys = _mixers(h, b, s, mix_norm[i], w_in[i], conv_w[i], conv_b[i], dt_bias[i], a_log[i], d_skip[i],
                     ssd_norm[i], sgu_ln_g[i], sgu_ln_b[i], sgu_w[i], sgu_b[i])
        h = _ffn(h, ffn2_norm[i], ffn2_w_gate[i], ffn2_w_up[i], ffn2_w_down[i], mix=(*ys, w_out[i]),
                 final_gain=final_norm if i == depth - 1 else None)
    return h.reshape(b, s, d)
```

```python
import functools
import math

import numpy as np
import jax
import jax.numpy as jnp
from jax import lax
from jax.experimental import pallas as pl
from jax.experimental.pallas import tpu as pltpu

F32 = jnp.float32
BF16 = jnp.bfloat16

D_MODEL = 1024
D_FF = 2816
HEAD_DIM = 64
ATT_HEADS = 6
ATT_WIDTH = ATT_HEADS * HEAD_DIM
DILATED_PAIRS = ((128, 1), (512, 4), (2048, 16))
SSD_HEADS = 6
SSD_HEADDIM = 64
SSD_WIDTH = SSD_HEADS * SSD_HEADDIM
SSD_GROUPS = 2
SSD_STATE = 128
SSD_CONV = 4
SSD_CHUNK = 128
SSD_CONV_DIM = SSD_WIDTH + 2 * SSD_GROUPS * SSD_STATE
SGU_GROUPS = 4
SGU_GROUP_DIM = 64
SGU_WIDTH = SGU_GROUPS * SGU_GROUP_DIM
SGU_CHUNK = 128
RMS_EPS = 1e-6
LN_EPS = 1e-5

LANES = 128
DT_PAD = LANES
QKV_WIDTH = 3 * ATT_WIDTH
UV_WIDTH = 2 * SGU_WIDTH
PROJ_WIDTH = QKV_WIDTH + SSD_WIDTH + SSD_CONV_DIM + UV_WIDTH + DT_PAD

VMEM_LIMIT = 56 * 1024 * 1024

ROW_TILE = 512
FF_CHUNK = 256
PROJ_CHUNK = 512

ATT_BLOCK = 128


def _params(n_axes):
    return pltpu.CompilerParams(dimension_semantics=("arbitrary",) * n_axes,
                                vmem_limit_bytes=VMEM_LIMIT)


def _rmsnorm_f32(x, g):
    ms = jnp.mean(x * x, axis=-1, keepdims=True)
    return x * lax.rsqrt(ms + RMS_EPS) * g


def _silu(x):
    return x * jax.nn.sigmoid(x)


N_FF_CHUNKS = D_FF // FF_CHUNK


def _ffn_kernel(*refs, mixed, final_norm):
    refs = list(refs)
    x_ref = refs.pop(0)
    if mixed:
        ya_ref, ys_ref, yg_ref, wo_ref = refs[:4]
        del refs[:4]
    g_ref, wg_ref, wu_ref, wd_ref = refs[:4]
    del refs[:4]
    fg_ref = refs.pop(0) if final_norm else None
    o_ref, wg16_ref, wu16_ref, wd16_ref, xn_ref, h_ref, res_ref = refs[:7]
    wo16_ref = refs[7] if mixed else None
    step = pl.program_id(0)

    @pl.when(step < N_FF_CHUNKS)
    def _load_weights():
        wg16_ref[step] = wg_ref[...].astype(BF16)
        wu16_ref[step] = wu_ref[...].astype(BF16)
        wd16_ref[pl.ds(pl.multiple_of(step * FF_CHUNK, FF_CHUNK), FF_CHUNK), :] = wd_ref[...].astype(BF16)
        if mixed:
            @pl.when(step == 0)
            def _():
                wo16_ref[...] = wo_ref[...].astype(BF16)

    @pl.when(step >= N_FF_CHUNKS)
    def _row_tile():
        x = x_ref[...]
        if mixed:
            a, b = ATT_WIDTH, ATT_WIDTH + SSD_WIDTH
            x = x + (jnp.dot(ya_ref[...].astype(BF16), wo16_ref[0:a, :], preferred_element_type=F32)
                     + jnp.dot(ys_ref[...], wo16_ref[a:b, :], preferred_element_type=F32)
                     + jnp.dot(yg_ref[...], wo16_ref[b:, :], preferred_element_type=F32))
        res_ref[...] = x
        xn_ref[...] = _rmsnorm_f32(x, g_ref[...]).astype(BF16)
        for f in range(N_FF_CHUNKS):
            xn = xn_ref[...]
            gate = jnp.dot(xn, wg16_ref[f], preferred_element_type=F32)
            up = jnp.dot(xn, wu16_ref[f], preferred_element_type=F32)
            h_ref[:, f * FF_CHUNK:(f + 1) * FF_CHUNK] = (_silu(gate) * up).astype(BF16)
        y = jnp.dot(h_ref[...], wd16_ref[...], preferred_element_type=F32)
        out = res_ref[...] + 0.5 * y
        if final_norm:
            out = _rmsnorm_f32(out, fg_ref[...])
        o_ref[...] = out


def _ffn(x, gain, w_gate, w_up, w_down, mix=None, final_gain=None):
    m = x.shape[0]
    tile = lambda i: jnp.maximum(i - N_FF_CHUNKS, 0)
    chunk = lambda i: jnp.minimum(i, N_FF_CHUNKS - 1)
    row = lambda n: pl.BlockSpec((ROW_TILE, n), lambda i: (tile(i), 0))
    full = lambda r, c: pl.BlockSpec((r, c), lambda i: (0, 0))
    in_specs, args = [row(D_MODEL)], [x]
    scratch = [pltpu.VMEM((N_FF_CHUNKS, D_MODEL, FF_CHUNK), BF16), pltpu.VMEM((N_FF_CHUNKS, D_MODEL, FF_CHUNK), BF16),
               pltpu.VMEM((D_FF, D_MODEL), BF16),
               pltpu.VMEM((ROW_TILE, D_MODEL), BF16), pltpu.VMEM((ROW_TILE, D_FF), BF16),
               pltpu.VMEM((ROW_TILE, D_MODEL), F32)]
    if mix is not None:
        y_att, y_ssd, y_sgu, w_out = mix
        in_specs += [row(ATT_WIDTH), row(SSD_WIDTH), row(SGU_WIDTH), full(D_MODEL, D_MODEL)]
        args += [y_att, y_ssd, y_sgu, w_out]
        scratch.append(pltpu.VMEM((D_MODEL, D_MODEL), BF16))
    in_specs += [full(1, D_MODEL),
                 pl.BlockSpec((D_MODEL, FF_CHUNK), lambda i: (0, chunk(i))),
                 pl.BlockSpec((D_MODEL, FF_CHUNK), lambda i: (0, chunk(i))),
                 pl.BlockSpec((FF_CHUNK, D_MODEL), lambda i: (chunk(i), 0))]
    args += [gain.reshape(1, D_MODEL), w_gate, w_up, w_down]
    if final_gain is not None:
        in_specs.append(full(1, D_MODEL))
        args.append(final_gain.reshape(1, D_MODEL))
    return pl.pallas_call(
        functools.partial(_ffn_kernel, mixed=mix is not None, final_norm=final_gain is not None),
        grid=(N_FF_CHUNKS + m // ROW_TILE,),
        in_specs=in_specs,
        out_specs=row(D_MODEL),
        out_shape=jax.ShapeDtypeStruct((m, D_MODEL), F32),
        scratch_shapes=scratch,
        compiler_params=_params(1),
        name="ffn",
    )(*args)


_PROJ_PIECES = (("qkv", QKV_WIDTH, F32),("z", SSD_WIDTH, F32), ("xbc", SSD_CONV_DIM, F32),
                ("uv", UV_WIDTH, F32), ("dt", DT_PAD, F32))


N_PROJ_CHUNKS = PROJ_WIDTH // PROJ_CHUNK


def _inproj_kernel(x_ref, g_ref, w_ref, qkv_ref, z_ref, xbc_ref, uv_ref, dt_ref, w16_ref, xn_ref):
    outs = (qkv_ref, z_ref, xbc_ref, uv_ref, dt_ref)
    step = pl.program_id(0)

    @pl.when(step < N_PROJ_CHUNKS)
    def _load_weights():
        w16_ref[step] = w_ref[...].astype(BF16)

    @pl.when(step >= N_PROJ_CHUNKS)
    def _row_tile():
        xn_ref[...] = _rmsnorm_f32(x_ref[...], g_ref[...]).astype(BF16)
        starts = np.cumsum([0] + [p[1] for p in _PROJ_PIECES])
        for c in range(N_PROJ_CHUNKS):
            lo, hi = c * PROJ_CHUNK, (c + 1) * PROJ_CHUNK
            r = jnp.dot(xn_ref[...], w16_ref[c], preferred_element_type=F32)
            for k, o_ref in enumerate(outs):
                a, b = max(lo, int(starts[k])), min(hi, int(starts[k + 1]))
                if a < b:
                    o_ref[:, a - int(starts[k]):b - int(starts[k])] = r[:, a - lo:b - lo].astype(o_ref.dtype)


def _inproj(x, gain, w_in):
    m = x.shape[0]
    o = np.cumsum([0, ATT_WIDTH, ATT_WIDTH, ATT_WIDTH, SSD_WIDTH, SSD_CONV_DIM, SSD_HEADS, UV_WIDTH])
    w_qkv, w_z, w_xbc, w_dt, w_uv = (w_in[:, o[0]:o[3]], w_in[:, o[3]:o[4]], w_in[:, o[4]:o[5]],
                                     w_in[:, o[5]:o[6]], w_in[:, o[6]:o[7]])
    w_dt = jnp.pad(w_dt, ((0, 0), (0, DT_PAD - SSD_HEADS)))
    w_all = jnp.concatenate([w_qkv, w_z, w_xbc, w_uv, w_dt], axis=1)
    row = lambda w: pl.BlockSpec((ROW_TILE, w), lambda i: (jnp.maximum(i - N_PROJ_CHUNKS, 0), 0))
    return pl.pallas_call(
        _inproj_kernel,
        grid=(N_PROJ_CHUNKS + m // ROW_TILE,),
        in_specs=[row(D_MODEL), pl.BlockSpec((1, D_MODEL), lambda i: (0, 0)),
                  pl.BlockSpec((D_MODEL, PROJ_CHUNK), lambda i: (0, jnp.minimum(i, N_PROJ_CHUNKS - 1)))],
        out_specs=[row(w) for _, w, _ in _PROJ_PIECES],
        out_shape=[jax.ShapeDtypeStruct((m, w), dt) for _, w, dt in _PROJ_PIECES],
        scratch_shapes=[pltpu.VMEM((N_PROJ_CHUNKS, D_MODEL, PROJ_CHUNK), BF16),
                        pltpu.VMEM((ROW_TILE, D_MODEL), BF16)],
        compiler_params=_params(1),
        name="inproj",
    )(x, gain.reshape(1, D_MODEL), w_all)


NAT, P4, P16 = 0, 1, 2


def _att_kernel(q_ref, k_ref, v_ref, o_ref, qa_ref, qb_ref, kk_ref, ve_ref,
                acc1_ref, m1_ref, l1_ref, acc3_ref, m3_ref, l3_ref, q4_ref, k4_ref, v4_ref, band_ref, cur_ref):
    seq = q_ref.shape[0]
    T = ATT_BLOCK
    d4, d16 = DILATED_PAIRS[1][1], DILATED_PAIRS[2][1]
    sub4 = seq // d4
    lane = lax.broadcasted_iota(jnp.int32, (1, LANES), 1)
    first = lane < HEAD_DIM
    qi = lax.broadcasted_iota(jnp.int32, (T, T), 0)
    kj = lax.broadcasted_iota(jnp.int32, (T, T), 1)
    cur_bias = jnp.where(kj <= qi, 0.0, -jnp.inf).astype(F32)
    prev_bias = jnp.where(kj >= qi, 0.0, -jnp.inf).astype(F32)
    for half in range(2):
        cur_ref[half * T:(half + 1) * T, :] = cur_bias
        band_ref[half * T:(half + 1) * T, 0:T] = prev_bias
        band_ref[half * T:(half + 1) * T, T:2 * T] = cur_bias
    q_scale = HEAD_DIM ** -0.5 * math.log2(math.e)

    def prep(layout, dst, q, k, v):
        q = q * q_scale
        qa_ref[layout, dst, :] = jnp.where(first, q, 0.0).astype(BF16)
        qb_ref[layout, dst, :] = jnp.where(first, 0.0, q).astype(BF16)
        kk_ref[layout, dst, :] = k.astype(BF16)
        ve_ref[layout, dst, 0:LANES] = v.astype(BF16)
        ve_ref[layout, dst, LANES:2 * LANES] = jnp.ones((T, LANES), BF16)

    def prep_nat_p4(c, carry):
        rows = pl.ds(pl.multiple_of(c * T, T), T)
        prep(NAT, rows, q_ref[rows, :], k_ref[rows, :], v_ref[rows, :])
        src = pl.ds(c // d4 + (c % d4) * (T * d4), T, stride=d4)
        q, k, v = q_ref[src, :], k_ref[src, :], v_ref[src, :]
        q4_ref[rows, :] = q
        k4_ref[rows, :] = k
        v4_ref[rows, :] = v
        prep(P4, rows, q, k, v)
        return carry

    lax.fori_loop(0, seq // T, prep_nat_p4, 0)

    def prep_p16(r16, carry):
        rows = pl.ds(pl.multiple_of(r16 * T, T), T)
        src = pl.ds((r16 % d4) * sub4 + r16 // d4, T, stride=d4)
        prep(P16, rows, q4_ref[src, :], k4_ref[src, :], v4_ref[src, :])
        return carry

    lax.fori_loop(0, d16, prep_p16, 0)

    def block(layout, qrows, krows, bias_ref):
        q2 = jnp.concatenate([qa_ref[layout, qrows, :], qb_ref[layout, qrows, :]], axis=0)
        s = lax.dot_general(q2, kk_ref[layout, krows, :], (((1,), (1,)), ((), ())),
                            preferred_element_type=F32) + bias_ref[...]
        m = jnp.max(s, axis=-1, keepdims=True)
        p = jnp.exp2(s - m).astype(BF16)
        r = jnp.dot(p, ve_ref[layout, krows, :], preferred_element_type=F32)
        acc = jnp.where(first, r[0:T, 0:LANES], r[T:2 * T, 0:LANES])
        lsum = jnp.where(first, r[0:T, LANES:2 * LANES], r[T:2 * T, LANES:2 * LANES])
        return acc, jnp.where(first, m[0:T], m[T:2 * T]), lsum

    def rows_of(start, n=T):
        return pl.ds(start, n)

    def store1(rows, acc, mb, lsum):
        acc1_ref[rows, :] = acc
        m1_ref[rows, :] = mb
        l1_ref[rows, :] = lsum

    store1(rows_of(0), *block(NAT, rows_of(0), rows_of(0), cur_ref))
    for n in range(1, seq // T):
        store1(rows_of(n * T), *block(NAT, rows_of(n * T), rows_of((n - 1) * T, 2 * T), band_ref))

    for r16 in range(d16):
        rows = rows_of(r16 * T)
        acc, mb, lsum = block(P16, rows, rows, cur_ref)
        dst = pl.ds((r16 % d4) * sub4 + r16 // d4, T, stride=d4)
        acc3_ref[dst, :] = acc
        m3_ref[dst, :] = mb
        l3_ref[dst, :] = lsum

    def finish(r4, n, keys, bias_ref):
        prow = rows_of(r4 * sub4 + n * T)
        acc2, mb2, l2 = block(P4, prow, keys, bias_ref)
        trow = pl.ds(r4 + n * (T * d4), T, stride=d4)
        acc1, mb1, l1 = acc1_ref[trow, :], m1_ref[trow, :], l1_ref[trow, :]
        acc3, mb3, l3 = acc3_ref[prow, :], m3_ref[prow, :], l3_ref[prow, :]
        m = jnp.maximum(mb1, jnp.maximum(mb2, mb3))
        w1, w2, w3 = jnp.exp2(mb1 - m), jnp.exp2(mb2 - m), jnp.exp2(mb3 - m)
        num = w1 * acc1 + w2 * acc2 + w3 * acc3
        den = w1 * l1 + w2 * l2 + w3 * l3
        o_ref[trow, :] = num / den

    for r4 in range(d4):
        finish(r4, 0, rows_of(r4 * sub4), cur_ref)
        for n in range(1, sub4 // T):
            finish(r4, n, rows_of(r4 * sub4 + (n - 1) * T, 2 * T), band_ref)


def _attention(qkv):
    b, s, _ = qkv.shape
    for window, dil in DILATED_PAIRS:
        assert window // dil == ATT_BLOCK and s % (ATT_BLOCK * dil) == 0
    assert DILATED_PAIRS[0][1] == 1 and DILATED_PAIRS[2][1] == DILATED_PAIRS[1][1] ** 2
    n_pairs = ATT_WIDTH // LANES
    spec = lambda part: pl.BlockSpec((None, s, LANES), lambda bi, hp: (bi, 0, part * n_pairs + hp))
    return pl.pallas_call(
        _att_kernel,
        grid=(b, n_pairs),
        in_specs=[spec(0), spec(1), spec(2)],
        out_specs=pl.BlockSpec((None, s, LANES), lambda bi, hp: (bi, 0, hp)),
        out_shape=jax.ShapeDtypeStruct((b, s, ATT_WIDTH), F32),
        scratch_shapes=[pltpu.VMEM((3, s, LANES), BF16)] * 3 + [pltpu.VMEM((3, s, 2 * LANES), BF16)]
        + [pltpu.VMEM((s, LANES), F32)] * 9
        + [pltpu.VMEM((2 * ATT_BLOCK, 2 * ATT_BLOCK), F32), pltpu.VMEM((2 * ATT_BLOCK, ATT_BLOCK), F32)],
        compiler_params=_params(2),
        name="dilated_attention",
    )(qkv, qkv, qkv)


SSD_HALO = 8
HEADS_PER_GROUP = SSD_HEADS // SSD_GROUPS
GROUP_LANES = HEADS_PER_GROUP * SSD_HEADDIM


def _ssd_kernel(z_ref, xbc_ref, dt_ref, cw_ref, cb_ref, dtb_ref, alog_ref, dsk_ref, ng_ref,
                o_ref, state_ref, ext_ref):
    n_chunks = z_ref.shape[0] // SSD_CHUNK
    L = SSD_CHUNK
    state_ref[...] = jnp.zeros_like(state_ref)
    row = lax.broadcasted_iota(jnp.int32, (L, L), 0)
    col = lax.broadcasted_iota(jnp.int32, (L, L), 1)
    tril = row >= col
    cumsum_mat = tril.astype(F32)
    lane = lax.broadcasted_iota(jnp.int32, (1, LANES), 1)
    lane_w = lax.broadcasted_iota(jnp.int32, (1, SSD_WIDTH), 1)
    first_group = lane_w < GROUP_LANES
    first_head = lane < SSD_HEADDIM
    a_neg = -jnp.exp(alog_ref[...])
    n_b = SSD_GROUPS * SSD_STATE

    def chunk(c, carry):
        r0 = pl.multiple_of(c * L, L)
        rows = pl.ds(r0, L)
        halo_start = pl.multiple_of(jnp.maximum(r0 - SSD_HALO, 0), SSD_HALO)
        halo = xbc_ref[pl.ds(halo_start, SSD_HALO), :]
        ext_ref[0:SSD_HALO, :] = jnp.where(c > 0, halo, jnp.zeros_like(halo))
        ext_ref[SSD_HALO:, :] = xbc_ref[rows, :]
        conv = cb_ref[...]
        for w in range(SSD_CONV):
            o = SSD_HALO - (SSD_CONV - 1) + w
            conv = conv + cw_ref[w:w + 1, :] * ext_ref[o:o + L, :]
        xact = _silu(conv)
        xs = xact[:, :SSD_WIDTH]
        bm = [xact[:, SSD_WIDTH + g * SSD_STATE:SSD_WIDTH + (g + 1) * SSD_STATE] for g in range(SSD_GROUPS)]
        cm = [xact[:, SSD_WIDTH + n_b + g * SSD_STATE:SSD_WIDTH + n_b + (g + 1) * SSD_STATE]
              for g in range(SSD_GROUPS)]
        bmt16 = [t.T.astype(BF16) for t in bm]
        cm16 = [t.astype(BF16) for t in cm]

        dt = jax.nn.softplus(dt_ref[rows, :] + dtb_ref[...])
        a = dt * a_neg
        acs = jnp.dot(cumsum_mat, a, precision=lax.Precision.HIGHEST, preferred_element_type=F32)
        acs_t = acs.T
        dt_t = dt.T
        acs_last = acs[L - 1:L, :]
        exp_acs_h = jnp.exp(acs)
        to_end_h = jnp.exp(acs_last - acs) * dt
        chunk_decay_h = jnp.exp(acs_last)
        cb = [jnp.dot(cm16[g], bmt16[g], preferred_element_type=F32) for g in range(SSD_GROUPS)]

        y_diag, e_pairs, w_pairs, d_pairs = [], [], [], []
        for p in range(SSD_HEADS // 2):
            xs_pair = xs[:, p * LANES:(p + 1) * LANES].astype(BF16)
            yd, ecol, wcol, dcol = [], [], [], []
            for h in (2 * p, 2 * p + 1):
                g = h // HEADS_PER_GROUP
                acs_col = jnp.broadcast_to(acs[:, h:h + 1], (L, L))
                seg = acs_col - acs_t[h:h + 1, :]
                decay = jnp.exp(jnp.where(tril, seg, -jnp.inf))
                mix = (cb[g] * decay * dt_t[h:h + 1, :]).astype(BF16)
                yd.append(jnp.dot(mix, xs_pair, preferred_element_type=F32))
                ecol.append(jnp.broadcast_to(exp_acs_h[:, h:h + 1], (L, LANES)))
                wcol.append(jnp.broadcast_to(to_end_h[:, h:h + 1], (L, LANES)))
                dcol.append(jnp.broadcast_to(chunk_decay_h[:, h:h + 1], (1, LANES)))
            y_diag.append(jnp.where(first_head, yd[0], yd[1]))
            e_pairs.append(jnp.where(first_head, ecol[0], ecol[1]))
            w_pairs.append(jnp.where(first_head, wcol[0], wcol[1]))
            d_pairs.append(jnp.where(first_head, dcol[0], dcol[1]))
        y_diag = jnp.concatenate(y_diag, axis=1)
        exp_acs = jnp.concatenate(e_pairs, axis=1)
        to_end = jnp.concatenate(w_pairs, axis=1)
        chunk_decay = jnp.concatenate(d_pairs, axis=1)

        state = state_ref[...]
        st16 = state.astype(BF16)
        y_off = jnp.where(first_group,
                          jnp.dot(cm16[0], st16, preferred_element_type=F32),
                          jnp.dot(cm16[1], st16, preferred_element_type=F32)) * exp_acs
        xdd = (xs * to_end).astype(BF16)
        new = jnp.where(first_group,
                        jnp.dot(bmt16[0], xdd, preferred_element_type=F32),
                        jnp.dot(bmt16[1], xdd, preferred_element_type=F32))
        state_ref[...] = state * chunk_decay + new

        y = y_diag + y_off + dsk_ref[...] * xs
        y = y * _silu(z_ref[rows, :])
        ysq = y * y
        s0 = jnp.sum(jnp.where(first_group, ysq, 0.0), axis=-1, keepdims=True)
        s1 = jnp.sum(jnp.where(first_group, 0.0, ysq), axis=-1, keepdims=True)
        ms = jnp.where(first_group, s0, s1) * (1.0 / GROUP_LANES)
        o_ref[rows, :] = (y * lax.rsqrt(ms + RMS_EPS) * ng_ref[...]).astype(o_ref.dtype)
        return carry

    lax.fori_loop(0, n_chunks, chunk, 0, unroll=2)


def _ssd(z, xbc, dt, conv_w, conv_b, dt_bias, a_log, d_skip, norm_g):
    b, s, _ = z.shape
    pad = lambda v: jnp.pad(v, (0, DT_PAD - SSD_HEADS)).reshape(1, DT_PAD)
    seq = lambda w: pl.BlockSpec((None, s, w), lambda bi: (bi, 0, 0))
    full = lambda r, w: pl.BlockSpec((r, w), lambda bi: (0, 0))
    return pl.pallas_call(
        _ssd_kernel,
        grid=(b,),
        in_specs=[seq(SSD_WIDTH), seq(SSD_CONV_DIM), seq(DT_PAD),
                  full(SSD_CONV, SSD_CONV_DIM), full(1, SSD_CONV_DIM), full(1, DT_PAD), full(1, DT_PAD),
                  full(1, SSD_WIDTH), full(1, SSD_WIDTH)],
        out_specs=seq(SSD_WIDTH),
        out_shape=jax.ShapeDtypeStruct((b, s, SSD_WIDTH), BF16),
        scratch_shapes=[pltpu.VMEM((SSD_STATE, SSD_WIDTH), F32),
                        pltpu.VMEM((SSD_HALO + SSD_CHUNK, SSD_CONV_DIM), F32)],
        compiler_params=_params(1),
        name="ssd",
    )(z, xbc, dt, conv_w, conv_b.reshape(1, SSD_CONV_DIM), pad(dt_bias), pad(a_log),
      jnp.repeat(d_skip, SSD_HEADDIM).reshape(1, SSD_WIDTH), norm_g.reshape(1, SSD_WIDTH))


SGU_ROWS = 512


def _sgu_kernel(uv_ref, lng_ref, lnb_ref, w_ref, bs_ref, o_ref):
    uv = uv_ref[...]
    act = 0.5 * uv * (1.0 + lax.erf(uv * (1.0 / math.sqrt(2.0))))
    u = act[:, :SGU_WIDTH]
    v = act[:, SGU_WIDTH:]
    mu = jnp.mean(v, axis=-1, keepdims=True)
    var = jnp.mean(jnp.square(v - mu), axis=-1, keepdims=True)
    vn = (v - mu) * lax.rsqrt(var + LN_EPS) * lng_ref[...] + lnb_ref[...]
    row = lax.broadcasted_iota(jnp.int32, (SGU_CHUNK, SGU_CHUNK), 0)
    col = lax.broadcasted_iota(jnp.int32, (SGU_CHUNK, SGU_CHUNK), 1)
    w = [jnp.where(row >= col, w_ref[g], 0.0).astype(BF16) for g in range(SGU_GROUPS)]
    lane = lax.broadcasted_iota(jnp.int32, (1, LANES), 1)
    first = lane < SGU_GROUP_DIM
    for c in range(SGU_ROWS // SGU_CHUNK):
        rows = slice(c * SGU_CHUNK, (c + 1) * SGU_CHUNK)
        mixed = []
        for p in range(SGU_WIDTH // LANES):
            vp = vn[rows, p * LANES:(p + 1) * LANES]
            lo = jnp.where(first, vp, 0.0).astype(BF16)
            hi = jnp.where(first, 0.0, vp).astype(BF16)
            mixed.append(jnp.dot(w[2 * p], lo, preferred_element_type=F32)
                         + jnp.dot(w[2 * p + 1], hi, preferred_element_type=F32))
        mixed = jnp.concatenate(mixed, axis=1) + bs_ref[...]
        o_ref[rows, :] = (u[rows, :] * mixed).astype(o_ref.dtype)


def _sgu(uv, ln_g, ln_b, w_s, b_s):
    b, s, _ = uv.shape
    bias = jnp.repeat(b_s.T, SGU_GROUP_DIM, axis=1)
    vec = pl.BlockSpec((1, SGU_WIDTH), lambda bi, r: (0, 0))
    return pl.pallas_call(
        _sgu_kernel,
        grid=(b, s // SGU_ROWS),
        in_specs=[pl.BlockSpec((None, SGU_ROWS, UV_WIDTH), lambda bi, r: (bi, r, 0)), vec, vec,
                  pl.BlockSpec((SGU_GROUPS, SGU_CHUNK, SGU_CHUNK), lambda bi, r: (0, 0, 0)),
                  pl.BlockSpec((SGU_CHUNK, SGU_WIDTH), lambda bi, r: (0, 0))],
        out_specs=pl.BlockSpec((None, SGU_ROWS, SGU_WIDTH), lambda bi, r: (bi, r, 0)),
        out_shape=jax.ShapeDtypeStruct((b, s, SGU_WIDTH), BF16),
        compiler_params=_params(2),
        name="sgu",
    )(uv, ln_g.reshape(1, SGU_WIDTH), ln_b.reshape(1, SGU_WIDTH), w_s, bias)


def _mixers(x, b, s, gain, w_in, conv_w, conv_b, dt_bias, a_log, d_skip, ssd_norm,
            sgu_ln_g, sgu_ln_b, sgu_w, sgu_b):
    qkv, z, xbc, uv, dt = _inproj(x, gain, w_in)
    seq = lambda t: t.reshape(b, s, t.shape[-1])
    y_att = _attention(seq(qkv))
    y_ssd = _ssd(seq(z), seq(xbc), seq(dt), conv_w, conv_b, dt_bias, a_log, d_skip, ssd_norm)
    y_sgu = _sgu(seq(uv), sgu_ln_g, sgu_ln_b, sgu_w, sgu_b)
    flat = lambda t: t.reshape(b * s, t.shape[-1])
    return flat(y_att), flat(y_ssd), flat(y_sgu)


def kernel(x, ffn1_norm, ffn1_w_gate, ffn1_w_up, ffn1_w_down, mix_norm, w_in, conv_w, conv_b, dt_bias, a_log, d_skip, ssd_norm, sgu_ln_g, sgu_ln_b, sgu_w, sgu_b, w_out, ffn2_norm, ffn2_w_gate, ffn2_w_up, ffn2_w_down, final_norm):
    b, s, d = x.shape
    depth = ffn1_norm.shape[0]
    h = x.reshape(b * s, d)
    for i in range(depth):
        h = _ffn(h, ffn1_norm[i], ffn1_w_gate[i], ffn1_w_up[i], ffn1_w_down[i])
        ys = _mixers(h, b, s, mix_norm[i], w_in[i], conv_w[i], conv_b[i], dt_bias[i], a_log[i], d_skip[i],
                     ssd_norm[i], sgu_ln_g[i], sgu_ln_b[i], sgu_w[i], sgu_b[i])
        h = _ffn(h, ffn2_norm[i], ffn2_w_gate[i], ffn2_w_up[i], ffn2_w_down[i], mix=(*ys, w_out[i]),
                 final_gain=final_norm if i == depth - 1 else None)
    return h.reshape(b, s, d)
```

```python
import functools
import math

import numpy as np
import jax
import jax.numpy as jnp
from jax import lax
from jax.experimental import pallas as pl
from jax.experimental.pallas import tpu as pltpu

F32 = jnp.float32
BF16 = jnp.bfloat16

D_MODEL = 1024
D_FF = 2816
HEAD_DIM = 64
ATT_HEADS = 6
ATT_WIDTH = ATT_HEADS * HEAD_DIM
DILATED_PAIRS = ((128, 1), (512, 4), (2048, 16))
SSD_HEADS = 6
SSD_HEADDIM = 64
SSD_WIDTH = SSD_HEADS * SSD_HEADDIM
SSD_GROUPS = 2
SSD_STATE = 128
SSD_CONV = 4
SSD_CHUNK = 128
SSD_CONV_DIM = SSD_WIDTH + 2 * SSD_GROUPS * SSD_STATE
SGU_GROUPS = 4
SGU_GROUP_DIM = 64
SGU_WIDTH = SGU_GROUPS * SGU_GROUP_DIM
SGU_CHUNK = 128
RMS_EPS = 1e-6
LN_EPS = 1e-5

LANES = 128
DT_PAD = LANES
QKV_WIDTH = 3 * ATT_WIDTH
UV_WIDTH = 2 * SGU_WIDTH
PROJ_WIDTH = QKV_WIDTH + SSD_WIDTH + SSD_CONV_DIM + UV_WIDTH + DT_PAD

VMEM_LIMIT = 56 * 1024 * 1024

ROW_TILE = 512
FF_CHUNK = 256
PROJ_CHUNK = 512

ATT_BLOCK = 128


def _params(n_axes):
    return pltpu.CompilerParams(dimension_semantics=("arbitrary",) * n_axes,
                                vmem_limit_bytes=VMEM_LIMIT)


def _rmsnorm_f32(x, g):
    ms = jnp.mean(x * x, axis=-1, keepdims=True)
    return x * lax.rsqrt(ms + RMS_EPS) * g


def _silu(x):
    return x * jax.nn.sigmoid(x)


N_FF_CHUNKS = D_FF // FF_CHUNK


def _ffn_kernel(*refs, mixed, final_norm):
    refs = list(refs)
    x_ref = refs.pop(0)
    if mixed:
        ya_ref, ys_ref, yg_ref, wo_ref = refs[:4]
        del refs[:4]
    g_ref, wg_ref, wu_ref, wd_ref = refs[:4]
    del refs[:4]
    fg_ref = refs.pop(0) if final_norm else None
    o_ref, wg16_ref, wu16_ref, wd16_ref, xn_ref, h_ref, res_ref = refs[:7]
    wo16_ref = refs[7] if mixed else None
    step = pl.program_id(0)

    @pl.when(step < N_FF_CHUNKS)
    def _load_weights():
        wg16_ref[step] = wg_ref[...].astype(BF16)
        wu16_ref[step] = wu_ref[...].astype(BF16)
        wd16_ref[pl.ds(pl.multiple_of(step * FF_CHUNK, FF_CHUNK), FF_CHUNK), :] = wd_ref[...].astype(BF16)
        if mixed:
            @pl.when(step == 0)
            def _():
                wo16_ref[...] = wo_ref[...].astype(BF16)

    @pl.when(step >= N_FF_CHUNKS)
    def _row_tile():
        x = x_ref[...]
        if mixed:
            a, b = ATT_WIDTH, ATT_WIDTH + SSD_WIDTH
            x = x + (jnp.dot(ya_ref[...].astype(BF16), wo16_ref[0:a, :], preferred_element_type=F32)
                     + jnp.dot(ys_ref[...], wo16_ref[a:b, :], preferred_element_type=F32)
                     + jnp.dot(yg_ref[...], wo16_ref[b:, :], preferred_element_type=F32))
        res_ref[...] = x
        xn_ref[...] = _rmsnorm_f32(x, g_ref[...]).astype(BF16)
        for f in range(N_FF_CHUNKS):
            xn = xn_ref[...]
            gate = jnp.dot(xn, wg16_ref[f], preferred_element_type=F32)
            up = jnp.dot(xn, wu16_ref[f], preferred_element_type=F32)
            h_ref[:, f * FF_CHUNK:(f + 1) * FF_CHUNK] = (_silu(gate) * up).astype(BF16)
        y = jnp.dot(h_ref[...], wd16_ref[...], preferred_element_type=F32)
        out = res_ref[...] + 0.5 * y
        if final_norm:
            out = _rmsnorm_f32(out, fg_ref[...])
        o_ref[...] = out


def _ffn(x, layer, gain, w_gate, w_up, w_down, mix=None, final_gain=None):
    m = x.shape[0]
    tile = lambda i: jnp.maximum(i - N_FF_CHUNKS, 0)
    chunk = lambda i: jnp.minimum(i, N_FF_CHUNKS - 1)
    row = lambda n: pl.BlockSpec((ROW_TILE, n), lambda i: (tile(i), 0))
    full = lambda r, c: pl.BlockSpec((r, c), lambda i: (0, 0))
    in_specs, args = [row(D_MODEL)], [x]
    scratch = [pltpu.VMEM((N_FF_CHUNKS, D_MODEL, FF_CHUNK), BF16), pltpu.VMEM((N_FF_CHUNKS, D_MODEL, FF_CHUNK), BF16),
               pltpu.VMEM((D_FF, D_MODEL), BF16),
               pltpu.VMEM((ROW_TILE, D_MODEL), BF16), pltpu.VMEM((ROW_TILE, D_FF), BF16),
               pltpu.VMEM((ROW_TILE, D_MODEL), F32)]
    if mix is not None:
        y_att, y_ssd, y_sgu, w_out = mix
        in_specs += [row(ATT_WIDTH), row(SSD_WIDTH), row(SGU_WIDTH),
                     pl.BlockSpec((None, D_MODEL, D_MODEL), lambda i: (layer, 0, 0))]
        args += [y_att, y_ssd, y_sgu, w_out]
        scratch.append(pltpu.VMEM((D_MODEL, D_MODEL), BF16))
    in_specs += [full(1, D_MODEL),
                 pl.BlockSpec((None, D_MODEL, FF_CHUNK), lambda i: (layer, 0, chunk(i))),
                 pl.BlockSpec((None, D_MODEL, FF_CHUNK), lambda i: (layer, 0, chunk(i))),
                 pl.BlockSpec((None, FF_CHUNK, D_MODEL), lambda i: (layer, chunk(i), 0))]
    args += [gain[layer].reshape(1, D_MODEL), w_gate, w_up, w_down]
    if final_gain is not None:
        in_specs.append(full(1, D_MODEL))
        args.append(final_gain.reshape(1, D_MODEL))
    return pl.pallas_call(
        functools.partial(_ffn_kernel, mixed=mix is not None, final_norm=final_gain is not None),
        grid=(N_FF_CHUNKS + m // ROW_TILE,),
        in_specs=in_specs,
        out_specs=row(D_MODEL),
        out_shape=jax.ShapeDtypeStruct((m, D_MODEL), F32),
        scratch_shapes=scratch,
        compiler_params=_params(1),
        name="ffn",
    )(*args)


_PROJ_PIECES = (("qkv", QKV_WIDTH, F32),("z", SSD_WIDTH, F32), ("xbc", SSD_CONV_DIM, F32),
                ("uv", UV_WIDTH, F32), ("dt", DT_PAD, F32))


RAW_WIDTH = QKV_WIDTH + SSD_WIDTH + SSD_CONV_DIM
RAW_CHUNK = 256
N_RAW_CHUNKS = RAW_WIDTH // RAW_CHUNK
RAW_REST = RAW_WIDTH - N_RAW_CHUNKS * RAW_CHUNK
TAIL_WIDTH = UV_WIDTH + DT_PAD
N_W_STEPS = N_RAW_CHUNKS + 1
N_PROJ_CHUNKS = PROJ_WIDTH // PROJ_CHUNK


def _inproj_kernel(x_ref, g_ref, w_ref, wrest_ref, wtail_ref, qkv_ref, z_ref, xbc_ref, uv_ref, dt_ref,
                   w16_ref, xn_ref):
    outs = (qkv_ref, z_ref, xbc_ref, uv_ref, dt_ref)
    step = pl.program_id(0)

    for c in range(N_RAW_CHUNKS):
        @pl.when(step == c)
        def _(c=c):
            w16_ref[:, c * RAW_CHUNK:(c + 1) * RAW_CHUNK] = w_ref[...].astype(BF16)

    @pl.when(step == N_RAW_CHUNKS)
    def _():
        w16_ref[:, N_RAW_CHUNKS * RAW_CHUNK:RAW_WIDTH] = wrest_ref[...].astype(BF16)
        w16_ref[:, RAW_WIDTH:] = wtail_ref[...].astype(BF16)

    @pl.when(step >= N_W_STEPS)
    def _row_tile():
        xn_ref[...] = _rmsnorm_f32(x_ref[...], g_ref[...]).astype(BF16)
        starts = np.cumsum([0] + [p[1] for p in _PROJ_PIECES])
        for c in range(N_PROJ_CHUNKS):
            lo, hi = c * PROJ_CHUNK, (c + 1) * PROJ_CHUNK
            r = jnp.dot(xn_ref[...], w16_ref[:, lo:hi], preferred_element_type=F32)
            for k, o_ref in enumerate(outs):
                a, b = max(lo, int(starts[k])), min(hi, int(starts[k + 1]))
                if a < b:
                    o_ref[:, a - int(starts[k]):b - int(starts[k])] = r[:, a - lo:b - lo].astype(o_ref.dtype)


def _inproj(x, layer, gain, w_in):
    m = x.shape[0]
    assert RAW_REST == LANES and RAW_WIDTH % LANES == 0
    dt0 = RAW_WIDTH
    uv0 = RAW_WIDTH + SSD_HEADS
    w_tail = jnp.concatenate([w_in[layer, :, uv0:uv0 + UV_WIDTH],
                              jnp.pad(w_in[layer, :, dt0:uv0], ((0, 0), (0, DT_PAD - SSD_HEADS)))], axis=1)
    row = lambda w: pl.BlockSpec((ROW_TILE, w), lambda i: (jnp.maximum(i - N_W_STEPS, 0), 0))
    return pl.pallas_call(
        _inproj_kernel,
        grid=(N_W_STEPS + m // ROW_TILE,),
        in_specs=[row(D_MODEL), pl.BlockSpec((1, D_MODEL), lambda i: (0, 0)),
                  pl.BlockSpec((None, D_MODEL, RAW_CHUNK), lambda i: (layer, 0, jnp.minimum(i, N_RAW_CHUNKS - 1))),
                  pl.BlockSpec((None, D_MODEL, RAW_REST), lambda i: (layer, 0, RAW_WIDTH // RAW_REST - 1)),
                  pl.BlockSpec((D_MODEL, TAIL_WIDTH), lambda i: (0, 0))],
        out_specs=[row(w) for _, w, _ in _PROJ_PIECES],
        out_shape=[jax.ShapeDtypeStruct((m, w), dt) for _, w, dt in _PROJ_PIECES],
        scratch_shapes=[pltpu.VMEM((D_MODEL, PROJ_WIDTH), BF16), pltpu.VMEM((ROW_TILE, D_MODEL), BF16)],
        compiler_params=_params(1),
        name="inproj",
    )(x, gain[layer].reshape(1, D_MODEL), w_in, w_in, w_tail)


NAT, P4, P16 = 0, 1, 2


def _att_kernel(q_ref, k_ref, v_ref, o_ref, qa_ref, qb_ref, kk_ref, ve_ref,
                acc1_ref, m1_ref, l1_ref, acc3_ref, m3_ref, l3_ref, q4_ref, k4_ref, v4_ref, band_ref, cur_ref):
    seq = q_ref.shape[0]
    T = ATT_BLOCK
    d4, d16 = DILATED_PAIRS[1][1], DILATED_PAIRS[2][1]
    sub4 = seq // d4
    lane = lax.broadcasted_iota(jnp.int32, (1, LANES), 1)
    first = lane < HEAD_DIM
    qi = lax.broadcasted_iota(jnp.int32, (T, T), 0)
    kj = lax.broadcasted_iota(jnp.int32, (T, T), 1)
    cur_bias = jnp.where(kj <= qi, 0.0, -jnp.inf).astype(F32)
    prev_bias = jnp.where(kj >= qi, 0.0, -jnp.inf).astype(F32)
    for half in range(2):
        cur_ref[half * T:(half + 1) * T, :] = cur_bias
        band_ref[half * T:(half + 1) * T, 0:T] = prev_bias
        band_ref[half * T:(half + 1) * T, T:2 * T] = cur_bias
    q_scale = HEAD_DIM ** -0.5 * math.log2(math.e)

    def prep(layout, dst, q, k, v):
        q = q * q_scale
        qa_ref[layout, dst, :] = jnp.where(first, q, 0.0).astype(BF16)
        qb_ref[layout, dst, :] = jnp.where(first, 0.0, q).astype(BF16)
        kk_ref[layout, dst, :] = k.astype(BF16)
        ve_ref[layout, dst, 0:LANES] = v.astype(BF16)
        ve_ref[layout, dst, LANES:2 * LANES] = jnp.ones((T, LANES), BF16)

    def prep_nat_p4(c, carry):
        rows = pl.ds(pl.multiple_of(c * T, T), T)
        prep(NAT, rows, q_ref[rows, :], k_ref[rows, :], v_ref[rows, :])
        src = pl.ds(c // d4 + (c % d4) * (T * d4), T, stride=d4)
        q, k, v = q_ref[src, :], k_ref[src, :], v_ref[src, :]
        q4_ref[rows, :] = q
        k4_ref[rows, :] = k
        v4_ref[rows, :] = v
        prep(P4, rows, q, k, v)
        return carry

    lax.fori_loop(0, seq // T, prep_nat_p4, 0)

    def prep_p16(r16, carry):
        rows = pl.ds(pl.multiple_of(r16 * T, T), T)
        src = pl.ds((r16 % d4) * sub4 + r16 // d4, T, stride=d4)
        prep(P16, rows, q4_ref[src, :], k4_ref[src, :], v4_ref[src, :])
        return carry

    lax.fori_loop(0, d16, prep_p16, 0)

    def block(layout, qrows, krows, bias_ref):
        q2 = jnp.concatenate([qa_ref[layout, qrows, :], qb_ref[layout, qrows, :]], axis=0)
        s = lax.dot_general(q2, kk_ref[layout, krows, :], (((1,), (1,)), ((), ())),
                            preferred_element_type=F32) + bias_ref[...]
        m = jnp.max(s, axis=-1, keepdims=True)
        p = jnp.exp2(s - m).astype(BF16)
        r = jnp.dot(p, ve_ref[layout, krows, :], preferred_element_type=F32)
        acc = jnp.where(first, r[0:T, 0:LANES], r[T:2 * T, 0:LANES])
        lsum = jnp.where(first, r[0:T, LANES:2 * LANES], r[T:2 * T, LANES:2 * LANES])
        return acc, jnp.where(first, m[0:T], m[T:2 * T]), lsum

    def rows_of(start, n=T):
        return pl.ds(start, n)

    def store1(rows, acc, mb, lsum):
        acc1_ref[rows, :] = acc
        m1_ref[rows, :] = mb
        l1_ref[rows, :] = lsum

    store1(rows_of(0), *block(NAT, rows_of(0), rows_of(0), cur_ref))
    for n in range(1, seq // T):
        store1(rows_of(n * T), *block(NAT, rows_of(n * T), rows_of((n - 1) * T, 2 * T), band_ref))

    for r16 in range(d16):
        rows = rows_of(r16 * T)
        acc, mb, lsum = block(P16, rows, rows, cur_ref)
        dst = pl.ds((r16 % d4) * sub4 + r16 // d4, T, stride=d4)
        acc3_ref[dst, :] = acc
        m3_ref[dst, :] = mb
        l3_ref[dst, :] = lsum

    def finish(r4, n, keys, bias_ref):
        prow = rows_of(r4 * sub4 + n * T)
        acc2, mb2, l2 = block(P4, prow, keys, bias_ref)
        trow = pl.ds(r4 + n * (T * d4), T, stride=d4)
        acc1, mb1, l1 = acc1_ref[trow, :], m1_ref[trow, :], l1_ref[trow, :]
        acc3, mb3, l3 = acc3_ref[prow, :], m3_ref[prow, :], l3_ref[prow, :]
        m = jnp.maximum(mb1, jnp.maximum(mb2, mb3))
        w1, w2, w3 = jnp.exp2(mb1 - m), jnp.exp2(mb2 - m), jnp.exp2(mb3 - m)
        num = w1 * acc1 + w2 * acc2 + w3 * acc3
        den = w1 * l1 + w2 * l2 + w3 * l3
        o_ref[trow, :] = num / den

    for r4 in range(d4):
        finish(r4, 0, rows_of(r4 * sub4), cur_ref)
        for n in range(1, sub4 // T):
            finish(r4, n, rows_of(r4 * sub4 + (n - 1) * T, 2 * T), band_ref)


def _attention(qkv):
    b, s, _ = qkv.shape
    for window, dil in DILATED_PAIRS:
        assert window // dil == ATT_BLOCK and s % (ATT_BLOCK * dil) == 0
    assert DILATED_PAIRS[0][1] == 1 and DILATED_PAIRS[2][1] == DILATED_PAIRS[1][1] ** 2
    n_pairs = ATT_WIDTH // LANES
    spec = lambda part: pl.BlockSpec((None, s, LANES), lambda bi, hp: (bi, 0, part * n_pairs + hp))
    return pl.pallas_call(
        _att_kernel,
        grid=(b, n_pairs),
        in_specs=[spec(0), spec(1), spec(2)],
        out_specs=pl.BlockSpec((None, s, LANES), lambda bi, hp: (bi, 0, hp)),
        out_shape=jax.ShapeDtypeStruct((b, s, ATT_WIDTH), F32),
        scratch_shapes=[pltpu.VMEM((3, s, LANES), BF16)] * 3 + [pltpu.VMEM((3, s, 2 * LANES), BF16)]
        + [pltpu.VMEM((s, LANES), F32)] * 9
        + [pltpu.VMEM((2 * ATT_BLOCK, 2 * ATT_BLOCK), F32), pltpu.VMEM((2 * ATT_BLOCK, ATT_BLOCK), F32)],
        compiler_params=_params(2),
        name="dilated_attention",
    )(qkv, qkv, qkv)


SSD_HALO = 8
HEADS_PER_GROUP = SSD_HEADS // SSD_GROUPS
GROUP_LANES = HEADS_PER_GROUP * SSD_HEADDIM


def _ssd_kernel(z_ref, xbc_ref, dt_ref, cw_ref, cb_ref, dtb_ref, alog_ref, dsk_ref, ng_ref,
                o_ref, state_ref, ext_ref):
    n_chunks = z_ref.shape[0] // SSD_CHUNK
    L = SSD_CHUNK
    state_ref[...] = jnp.zeros_like(state_ref)
    row = lax.broadcasted_iota(jnp.int32, (L, L), 0)
    col = lax.broadcasted_iota(jnp.int32, (L, L), 1)
    tril = row >= col
    cumsum_mat = tril.astype(F32)
    lane = lax.broadcasted_iota(jnp.int32, (1, LANES), 1)
    lane_w = lax.broadcasted_iota(jnp.int32, (1, SSD_WIDTH), 1)
    first_group = lane_w < GROUP_LANES
    first_head = lane < SSD_HEADDIM
    a_neg = -jnp.exp(alog_ref[...])
    n_b = SSD_GROUPS * SSD_STATE

    def chunk(c, carry):
        r0 = pl.multiple_of(c * L, L)
        rows = pl.ds(r0, L)
        halo_start = pl.multiple_of(jnp.maximum(r0 - SSD_HALO, 0), SSD_HALO)
        halo = xbc_ref[pl.ds(halo_start, SSD_HALO), :]
        ext_ref[0:SSD_HALO, :] = jnp.where(c > 0, halo, jnp.zeros_like(halo))
        ext_ref[SSD_HALO:, :] = xbc_ref[rows, :]
        conv = cb_ref[...]
        for w in range(SSD_CONV):
            o = SSD_HALO - (SSD_CONV - 1) + w
            conv = conv + cw_ref[w:w + 1, :] * ext_ref[o:o + L, :]
        xact = _silu(conv)
        xs = xact[:, :SSD_WIDTH]
        bm = [xact[:, SSD_WIDTH + g * SSD_STATE:SSD_WIDTH + (g + 1) * SSD_STATE] for g in range(SSD_GROUPS)]
        cm = [xact[:, SSD_WIDTH + n_b + g * SSD_STATE:SSD_WIDTH + n_b + (g + 1) * SSD_STATE]
              for g in range(SSD_GROUPS)]
        bmt16 = [t.T.astype(BF16) for t in bm]
        cm16 = [t.astype(BF16) for t in cm]

        dt = jax.nn.softplus(dt_ref[rows, :] + dtb_ref[...])
        a = dt * a_neg
        acs = jnp.dot(cumsum_mat, a, precision=lax.Precision.HIGHEST, preferred_element_type=F32)
        acs_t = acs.T
        dt_t = dt.T
        acs_last = acs[L - 1:L, :]
        exp_acs_h = jnp.exp(acs)
        to_end_h = jnp.exp(acs_last - acs) * dt
        chunk_decay_h = jnp.exp(acs_last)
        cb = [jnp.dot(cm16[g], bmt16[g], preferred_element_type=F32) for g in range(SSD_GROUPS)]

        y_diag, e_pairs, w_pairs, d_pairs = [], [], [], []
        for p in range(SSD_HEADS // 2):
            xs_pair = xs[:, p * LANES:(p + 1) * LANES].astype(BF16)
            yd, ecol, wcol, dcol = [], [], [], []
            for h in (2 * p, 2 * p + 1):
                g = h // HEADS_PER_GROUP
                acs_col = jnp.broadcast_to(acs[:, h:h + 1], (L, L))
                seg = acs_col - acs_t[h:h + 1, :]
                decay = jnp.exp(jnp.where(tril, seg, -jnp.inf))
                mix = (cb[g] * decay * dt_t[h:h + 1, :]).astype(BF16)
                yd.append(jnp.dot(mix, xs_pair, preferred_element_type=F32))
                ecol.append(jnp.broadcast_to(exp_acs_h[:, h:h + 1], (L, LANES)))
                wcol.append(jnp.broadcast_to(to_end_h[:, h:h + 1], (L, LANES)))
                dcol.append(jnp.broadcast_to(chunk_decay_h[:, h:h + 1], (1, LANES)))
            y_diag.append(jnp.where(first_head, yd[0], yd[1]))
            e_pairs.append(jnp.where(first_head, ecol[0], ecol[1]))
            w_pairs.append(jnp.where(first_head, wcol[0], wcol[1]))
            d_pairs.append(jnp.where(first_head, dcol[0], dcol[1]))
        y_diag = jnp.concatenate(y_diag, axis=1)
        exp_acs = jnp.concatenate(e_pairs, axis=1)
        to_end = jnp.concatenate(w_pairs, axis=1)
        chunk_decay = jnp.concatenate(d_pairs, axis=1)

        state = state_ref[...]
        st16 = state.astype(BF16)
        y_off = jnp.where(first_group,
                          jnp.dot(cm16[0], st16, preferred_element_type=F32),
                          jnp.dot(cm16[1], st16, preferred_element_type=F32)) * exp_acs
        xdd = (xs * to_end).astype(BF16)
        new = jnp.where(first_group,
                        jnp.dot(bmt16[0], xdd, preferred_element_type=F32),
                        jnp.dot(bmt16[1], xdd, preferred_element_type=F32))
        state_ref[...] = state * chunk_decay + new

        y = y_diag + y_off + dsk_ref[...] * xs
        y = y * _silu(z_ref[rows, :])
        ysq = y * y
        s0 = jnp.sum(jnp.where(first_group, ysq, 0.0), axis=-1, keepdims=True)
        s1 = jnp.sum(jnp.where(first_group, 0.0, ysq), axis=-1, keepdims=True)
        ms = jnp.where(first_group, s0, s1) * (1.0 / GROUP_LANES)
        o_ref[rows, :] = (y * lax.rsqrt(ms + RMS_EPS) * ng_ref[...]).astype(o_ref.dtype)
        return carry

    lax.fori_loop(0, n_chunks, chunk, 0, unroll=2)


def _ssd(z, xbc, dt, conv_w, conv_b, dt_bias, a_log, d_skip, norm_g):
    b, s, _ = z.shape
    pad = lambda v: jnp.pad(v, (0, DT_PAD - SSD_HEADS)).reshape(1, DT_PAD)
    seq = lambda w: pl.BlockSpec((None, s, w), lambda bi: (bi, 0, 0))
    full = lambda r, w: pl.BlockSpec((r, w), lambda bi: (0, 0))
    return pl.pallas_call(
        _ssd_kernel,
        grid=(b,),
        in_specs=[seq(SSD_WIDTH), seq(SSD_CONV_DIM), seq(DT_PAD),
                  full(SSD_CONV, SSD_CONV_DIM), full(1, SSD_CONV_DIM), full(1, DT_PAD), full(1, DT_PAD),
                  full(1, SSD_WIDTH), full(1, SSD_WIDTH)],
        out_specs=seq(SSD_WIDTH),
        out_shape=jax.ShapeDtypeStruct((b, s, SSD_WIDTH), BF16),
        scratch_shapes=[pltpu.VMEM((SSD_STATE, SSD_WIDTH), F32),
                        pltpu.VMEM((SSD_HALO + SSD_CHUNK, SSD_CONV_DIM), F32)],
        compiler_params=_params(1),
        name="ssd",
    )(z, xbc, dt, conv_w, conv_b.reshape(1, SSD_CONV_DIM), pad(dt_bias), pad(a_log),
      jnp.repeat(d_skip, SSD_HEADDIM).reshape(1, SSD_WIDTH), norm_g.reshape(1, SSD_WIDTH))


SGU_ROWS = 512


def _sgu_kernel(uv_ref, lng_ref, lnb_ref, w_ref, bs_ref, o_ref):
    uv = uv_ref[...]
    act = 0.5 * uv * (1.0 + lax.erf(uv * (1.0 / math.sqrt(2.0))))
    u = act[:, :SGU_WIDTH]
    v = act[:, SGU_WIDTH:]
    mu = jnp.mean(v, axis=-1, keepdims=True)
    var = jnp.mean(jnp.square(v - mu), axis=-1, keepdims=True)
    vn = (v - mu) * lax.rsqrt(var + LN_EPS) * lng_ref[...] + lnb_ref[...]
    row = lax.broadcasted_iota(jnp.int32, (SGU_CHUNK, SGU_CHUNK), 0)
    col = lax.broadcasted_iota(jnp.int32, (SGU_CHUNK, SGU_CHUNK), 1)
    w = [jnp.where(row >= col, w_ref[g], 0.0).astype(BF16) for g in range(SGU_GROUPS)]
    lane = lax.broadcasted_iota(jnp.int32, (1, LANES), 1)
    first = lane < SGU_GROUP_DIM
    for c in range(SGU_ROWS // SGU_CHUNK):
        rows = slice(c * SGU_CHUNK, (c + 1) * SGU_CHUNK)
        mixed = []
        for p in range(SGU_WIDTH // LANES):
            vp = vn[rows, p * LANES:(p + 1) * LANES]
            lo = jnp.where(first, vp, 0.0).astype(BF16)
            hi = jnp.where(first, 0.0, vp).astype(BF16)
            mixed.append(jnp.dot(w[2 * p], lo, preferred_element_type=F32)
                         + jnp.dot(w[2 * p + 1], hi, preferred_element_type=F32))
        mixed = jnp.concatenate(mixed, axis=1) + bs_ref[...]
        o_ref[rows, :] = (u[rows, :] * mixed).astype(o_ref.dtype)


def _sgu(uv, ln_g, ln_b, w_s, b_s):
    b, s, _ = uv.shape
    bias = jnp.repeat(b_s.T, SGU_GROUP_DIM, axis=1)
    vec = pl.BlockSpec((1, SGU_WIDTH), lambda bi, r: (0, 0))
    return pl.pallas_call(
        _sgu_kernel,
        grid=(b, s // SGU_ROWS),
        in_specs=[pl.BlockSpec((None, SGU_ROWS, UV_WIDTH), lambda bi, r: (bi, r, 0)), vec, vec,
                  pl.BlockSpec((SGU_GROUPS, SGU_CHUNK, SGU_CHUNK), lambda bi, r: (0, 0, 0)),
                  pl.BlockSpec((SGU_CHUNK, SGU_WIDTH), lambda bi, r: (0, 0))],
        out_specs=pl.BlockSpec((None, SGU_ROWS, SGU_WIDTH), lambda bi, r: (bi, r, 0)),
        out_shape=jax.ShapeDtypeStruct((b, s, SGU_WIDTH), BF16),
        compiler_params=_params(2),
        name="sgu",
    )(uv, ln_g.reshape(1, SGU_WIDTH), ln_b.reshape(1, SGU_WIDTH), w_s, bias)


def _mixers(x, b, s, layer, gain, w_in, conv_w, conv_b, dt_bias, a_log, d_skip, ssd_norm,
            sgu_ln_g, sgu_ln_b, sgu_w, sgu_b):
    qkv, z, xbc, uv, dt = _inproj(x, layer, gain, w_in)
    seq = lambda t: t.reshape(b, s, t.shape[-1])
    y_att = _attention(seq(qkv))
    y_ssd = _ssd(seq(z), seq(xbc), seq(dt), conv_w, conv_b, dt_bias, a_log, d_skip, ssd_norm)
    y_sgu = _sgu(seq(uv), sgu_ln_g, sgu_ln_b, sgu_w, sgu_b)
    flat = lambda t: t.reshape(b * s, t.shape[-1])
    return flat(y_att), flat(y_ssd), flat(y_sgu)


def kernel(x, ffn1_norm, ffn1_w_gate, ffn1_w_up, ffn1_w_down, mix_norm, w_in, conv_w, conv_b, dt_bias, a_log, d_skip, ssd_norm, sgu_ln_g, sgu_ln_b, sgu_w, sgu_b, w_out, ffn2_norm, ffn2_w_gate, ffn2_w_up, ffn2_w_down, final_norm):
    b, s, d = x.shape
    depth = ffn1_norm.shape[0]
    h = x.reshape(b * s, d)
    for i in range(depth):
        h = _ffn(h, i, ffn1_norm, ffn1_w_gate, ffn1_w_up, ffn1_w_down)
        ys = _mixers(h, b, s, i, mix_norm, w_in, conv_w[i], conv_b[i], dt_bias[i], a_log[i], d_skip[i],
                     ssd_norm[i], sgu_ln_g[i], sgu_ln_b[i], sgu_w[i], sgu_b[i])
        h = _ffn(h, i, ffn2_norm, ffn2_w_gate, ffn2_w_up, ffn2_w_down, mix=(*ys, w_out),
                 final_gain=final_norm if i == depth - 1 else None)
    return h.reshape(b, s, d)
```

```python
import functools
import math

import numpy as np
import jax
import jax.numpy as jnp
from jax import lax
from jax.experimental import pallas as pl
from jax.experimental.pallas import tpu as pltpu

F32 = jnp.float32
BF16 = jnp.bfloat16

D_MODEL = 1024
D_FF = 2816
HEAD_DIM = 64
ATT_HEADS = 6
ATT_WIDTH = ATT_HEADS * HEAD_DIM
DILATED_PAIRS = ((128, 1), (512, 4), (2048, 16))
SSD_HEADS = 6
SSD_HEADDIM = 64
SSD_WIDTH = SSD_HEADS * SSD_HEADDIM
SSD_GROUPS = 2
SSD_STATE = 128
SSD_CONV = 4
SSD_CHUNK = 128
SSD_CONV_DIM = SSD_WIDTH + 2 * SSD_GROUPS * SSD_STATE
SGU_GROUPS = 4
SGU_GROUP_DIM = 64
SGU_WIDTH = SGU_GROUPS * SGU_GROUP_DIM
SGU_CHUNK = 128
RMS_EPS = 1e-6
LN_EPS = 1e-5

LANES = 128
DT_PAD = LANES
QKV_WIDTH = 3 * ATT_WIDTH
UV_WIDTH = 2 * SGU_WIDTH
PROJ_WIDTH = QKV_WIDTH + SSD_WIDTH + SSD_CONV_DIM + UV_WIDTH + DT_PAD

VMEM_LIMIT = 56 * 1024 * 1024

ROW_TILE = 512
FF_CHUNK = 256
PROJ_CHUNK = 512

ATT_BLOCK = 128


def _params(n_axes):
    return pltpu.CompilerParams(dimension_semantics=("arbitrary",) * n_axes,
                                vmem_limit_bytes=VMEM_LIMIT)


def _rmsnorm_f32(x, g):
    ms = jnp.mean(x * x, axis=-1, keepdims=True)
    return x * lax.rsqrt(ms + RMS_EPS) * g


def _silu(x):
    return x * jax.nn.sigmoid(x)


N_FF_CHUNKS = D_FF // FF_CHUNK


def _ffn_kernel(*refs, mixed, final_norm):
    refs = list(refs)
    x_ref = refs.pop(0)
    if mixed:
        ya_ref, ys_ref, yg_ref, wo_ref = refs[:4]
        del refs[:4]
    g_ref, wg_ref, wu_ref, wd_ref = refs[:4]
    del refs[:4]
    fg_ref = refs.pop(0) if final_norm else None
    o_ref, wg16_ref, wu16_ref, wd16_ref, h_ref = refs[:5]
    stages = (refs[5:7], refs[7:9])
    wo16_ref = refs[9] if mixed else None
    step = pl.program_id(0)

    def stage(xn_ref, res_ref):
        x = x_ref[...]
        if mixed:
            a, b = ATT_WIDTH, ATT_WIDTH + SSD_WIDTH
            x = x + (jnp.dot(ya_ref[...].astype(BF16), wo16_ref[0:a, :], preferred_element_type=F32)
                     + jnp.dot(ys_ref[...], wo16_ref[a:b, :], preferred_element_type=F32)
                     + jnp.dot(yg_ref[...], wo16_ref[b:, :], preferred_element_type=F32))
        res_ref[...] = x
        xn_ref[...] = _rmsnorm_f32(x, g_ref[...]).astype(BF16)

    @pl.when(step < N_FF_CHUNKS)
    def _load_weights():
        wg16_ref[step] = wg_ref[...].astype(BF16)
        wu16_ref[step] = wu_ref[...].astype(BF16)
        wd16_ref[pl.ds(pl.multiple_of(step * FF_CHUNK, FF_CHUNK), FF_CHUNK), :] = wd_ref[...].astype(BF16)
        if mixed:
            @pl.when(step == 0)
            def _():
                wo16_ref[...] = wo_ref[...].astype(BF16)

        @pl.when(step == N_FF_CHUNKS - 1)
        def _():
            stage(*stages[0])

    def row_tile(cur, nxt):
        xn_ref, res_ref = cur
        stage(*nxt)
        for f in range(N_FF_CHUNKS):
            xn = xn_ref[...]
            gate = jnp.dot(xn, wg16_ref[f], preferred_element_type=F32)
            up = jnp.dot(xn, wu16_ref[f], preferred_element_type=F32)
            h_ref[:, f * FF_CHUNK:(f + 1) * FF_CHUNK] = (_silu(gate) * up).astype(BF16)
        y = jnp.dot(h_ref[...], wd16_ref[...], preferred_element_type=F32)
        out = res_ref[...] + 0.5 * y
        if final_norm:
            out = _rmsnorm_f32(out, fg_ref[...])
        o_ref[...] = out

    tile = step - N_FF_CHUNKS
    for parity in range(2):
        @pl.when((tile >= 0) & (lax.rem(tile, 2) == parity))
        def _(parity=parity):
            row_tile(stages[parity], stages[1 - parity])


def _ffn(x, layer, gain, w_gate, w_up, w_down, mix=None, final_gain=None):
    m = x.shape[0]
    n_tiles = m // ROW_TILE
    tile = lambda i: jnp.maximum(i - N_FF_CHUNKS, 0)
    ahead = lambda i: jnp.clip(i - N_FF_CHUNKS + 1, 0, n_tiles - 1)
    chunk = lambda i: jnp.minimum(i, N_FF_CHUNKS - 1)
    row = lambda n: pl.BlockSpec((ROW_TILE, n), lambda i: (ahead(i), 0))
    full = lambda r, c: pl.BlockSpec((r, c), lambda i: (0, 0))
    in_specs, args = [row(D_MODEL)], [x]
    scratch = [pltpu.VMEM((N_FF_CHUNKS, D_MODEL, FF_CHUNK), BF16), pltpu.VMEM((N_FF_CHUNKS, D_MODEL, FF_CHUNK), BF16),
               pltpu.VMEM((D_FF, D_MODEL), BF16), pltpu.VMEM((ROW_TILE, D_FF), BF16)]
    scratch += [pltpu.VMEM((ROW_TILE, D_MODEL), BF16), pltpu.VMEM((ROW_TILE, D_MODEL), F32)] * 2
    if mix is not None:
        y_att, y_ssd, y_sgu, w_out = mix
        in_specs += [row(ATT_WIDTH), row(SSD_WIDTH), row(SGU_WIDTH),
                     pl.BlockSpec((None, D_MODEL, D_MODEL), lambda i: (layer, 0, 0))]
        args += [y_att, y_ssd, y_sgu, w_out]
        scratch.append(pltpu.VMEM((D_MODEL, D_MODEL), BF16))
    in_specs += [full(1, D_MODEL),
                 pl.BlockSpec((None, D_MODEL, FF_CHUNK), lambda i: (layer, 0, chunk(i))),
                 pl.BlockSpec((None, D_MODEL, FF_CHUNK), lambda i: (layer, 0, chunk(i))),
                 pl.BlockSpec((None, FF_CHUNK, D_MODEL), lambda i: (layer, chunk(i), 0))]
    args += [gain[layer].reshape(1, D_MODEL), w_gate, w_up, w_down]
    if final_gain is not None:
        in_specs.append(full(1, D_MODEL))
        args.append(final_gain.reshape(1, D_MODEL))
    return pl.pallas_call(
        functools.partial(_ffn_kernel, mixed=mix is not None, final_norm=final_gain is not None),
        grid=(N_FF_CHUNKS + n_tiles,),
        in_specs=in_specs,
        out_specs=pl.BlockSpec((ROW_TILE, D_MODEL), lambda i: (tile(i), 0)),
        out_shape=jax.ShapeDtypeStruct((m, D_MODEL), F32),
        scratch_shapes=scratch,
        compiler_params=_params(1),
        name="ffn",
    )(*args)


_PROJ_PIECES = (("qkv", QKV_WIDTH, F32),("z", SSD_WIDTH, F32), ("xbc", SSD_CONV_DIM, F32),
                ("uv", UV_WIDTH, F32), ("dt", DT_PAD, F32))


RAW_CHUNK = 256
N_RAW_CHUNKS = (QKV_WIDTH + SSD_WIDTH + SSD_CONV_DIM) // RAW_CHUNK
RAW_WIDTH = N_RAW_CHUNKS * RAW_CHUNK
TAIL_WIDTH = PROJ_WIDTH - RAW_WIDTH
N_W_STEPS = N_RAW_CHUNKS
N_PROJ_CHUNKS = PROJ_WIDTH // PROJ_CHUNK


def _inproj_kernel(x_ref, g_ref, w_ref, wtail_ref, qkv_ref, z_ref, xbc_ref, uv_ref, dt_ref,
                   w16_ref, xn_a, xn_b):
    outs = (qkv_ref, z_ref, xbc_ref, uv_ref, dt_ref)
    step = pl.program_id(0)

    def stage(xn_ref):
        xn_ref[...] = _rmsnorm_f32(x_ref[...], g_ref[...]).astype(BF16)

    for c in range(N_RAW_CHUNKS):
        @pl.when(step == c)
        def _(c=c):
            w16_ref[:, c * RAW_CHUNK:(c + 1) * RAW_CHUNK] = w_ref[...].astype(BF16)

    @pl.when(step == N_W_STEPS - 1)
    def _():
        w16_ref[:, RAW_WIDTH:] = wtail_ref[...].astype(BF16)
        stage(xn_a)

    def row_tile(xn_ref, xn_next):
        stage(xn_next)
        starts = np.cumsum([0] + [p[1] for p in _PROJ_PIECES])
        for c in range(N_PROJ_CHUNKS):
            lo, hi = c * PROJ_CHUNK, (c + 1) * PROJ_CHUNK
            r = jnp.dot(xn_ref[...], w16_ref[:, lo:hi], preferred_element_type=F32)
            for k, o_ref in enumerate(outs):
                a, b = max(lo, int(starts[k])), min(hi, int(starts[k + 1]))
                if a < b:
                    o_ref[:, a - int(starts[k]):b - int(starts[k])] = r[:, a - lo:b - lo].astype(o_ref.dtype)

    tile = step - N_W_STEPS
    for parity, (cur, nxt) in enumerate(((xn_a, xn_b), (xn_b, xn_a))):
        @pl.when((tile >= 0) & (lax.rem(tile, 2) == parity))
        def _(cur=cur, nxt=nxt):
            row_tile(cur, nxt)


def _inproj(x, layer, gain, w_in):
    m = x.shape[0]
    n_tiles = m // ROW_TILE
    dt0 = QKV_WIDTH + SSD_WIDTH + SSD_CONV_DIM
    uv0 = dt0 + SSD_HEADS
    w_tail = jnp.concatenate([w_in[layer, :, RAW_WIDTH:dt0], w_in[layer, :, uv0:uv0 + UV_WIDTH],
                              jnp.pad(w_in[layer, :, dt0:uv0], ((0, 0), (0, DT_PAD - SSD_HEADS)))], axis=1)
    ahead = lambda i: jnp.clip(i - N_W_STEPS + 1, 0, n_tiles - 1)
    out_row = lambda w: pl.BlockSpec((ROW_TILE, w), lambda i: (jnp.maximum(i - N_W_STEPS, 0), 0))
    return pl.pallas_call(
        _inproj_kernel,
        grid=(N_W_STEPS + n_tiles,),
        in_specs=[pl.BlockSpec((ROW_TILE, D_MODEL), lambda i: (ahead(i), 0)),
                  pl.BlockSpec((1, D_MODEL), lambda i: (0, 0)),
                  pl.BlockSpec((None, D_MODEL, RAW_CHUNK), lambda i: (layer, 0, jnp.minimum(i, N_RAW_CHUNKS - 1))),
                  pl.BlockSpec((D_MODEL, TAIL_WIDTH), lambda i: (0, 0))],
        out_specs=[out_row(w) for _, w, _ in _PROJ_PIECES],
        out_shape=[jax.ShapeDtypeStruct((m, w), dt) for _, w, dt in _PROJ_PIECES],
        scratch_shapes=[pltpu.VMEM((D_MODEL, PROJ_WIDTH), BF16)] + [pltpu.VMEM((ROW_TILE, D_MODEL), BF16)] * 2,
        compiler_params=_params(1),
        name="inproj",
    )(x, gain[layer].reshape(1, D_MODEL), w_in, w_tail)


NAT, P4, P16 = 0, 1, 2


def _att_kernel(q_ref, k_ref, v_ref, o_ref, qa_ref, qb_ref, kk_ref, ve_ref,
                acc1_ref, m1_ref, l1_ref, acc3_ref, m3_ref, l3_ref, q4_ref, k4_ref, v4_ref, band_ref, cur_ref):
    seq = q_ref.shape[0]
    T = ATT_BLOCK
    d4, d16 = DILATED_PAIRS[1][1], DILATED_PAIRS[2][1]
    sub4 = seq // d4
    lane = lax.broadcasted_iota(jnp.int32, (1, LANES), 1)
    first = lane < HEAD_DIM
    qi = lax.broadcasted_iota(jnp.int32, (T, T), 0)
    kj = lax.broadcasted_iota(jnp.int32, (T, T), 1)
    cur_bias = jnp.where(kj <= qi, 0.0, -jnp.inf).astype(F32)
    prev_bias = jnp.where(kj >= qi, 0.0, -jnp.inf).astype(F32)
    for half in range(2):
        cur_ref[half * T:(half + 1) * T, :] = cur_bias
        band_ref[half * T:(half + 1) * T, 0:T] = prev_bias
        band_ref[half * T:(half + 1) * T, T:2 * T] = cur_bias
    q_scale = HEAD_DIM ** -0.5 * math.log2(math.e)

    def prep(layout, dst, q, k, v):
        q = q * q_scale
        qa_ref[layout, dst, :] = jnp.where(first, q, 0.0).astype(BF16)
        qb_ref[layout, dst, :] = jnp.where(first, 0.0, q).astype(BF16)
        kk_ref[layout, dst, :] = k.astype(BF16)
        ve_ref[layout, dst, 0:LANES] = v.astype(BF16)
        ve_ref[layout, dst, LANES:2 * LANES] = jnp.ones((T, LANES), BF16)

    def prep_nat_p4(c, carry):
        rows = pl.ds(pl.multiple_of(c * T, T), T)
        prep(NAT, rows, q_ref[rows, :], k_ref[rows, :], v_ref[rows, :])
        src = pl.ds(c // d4 + (c % d4) * (T * d4), T, stride=d4)
        q, k, v = q_ref[src, :], k_ref[src, :], v_ref[src, :]
        q4_ref[rows, :] = q
        k4_ref[rows, :] = k
        v4_ref[rows, :] = v
        prep(P4, rows, q, k, v)
        return carry

    lax.fori_loop(0, seq // T, prep_nat_p4, 0)

    def prep_p16(r16, carry):
        rows = pl.ds(pl.multiple_of(r16 * T, T), T)
        src = pl.ds((r16 % d4) * sub4 + r16 // d4, T, stride=d4)
        prep(P16, rows, q4_ref[src, :], k4_ref[src, :], v4_ref[src, :])
        return carry

    lax.fori_loop(0, d16, prep_p16, 0)

    def block(layout, qrows, krows, bias_ref):
        q2 = jnp.concatenate([qa_ref[layout, qrows, :], qb_ref[layout, qrows, :]], axis=0)
        s = lax.dot_general(q2, kk_ref[layout, krows, :], (((1,), (1,)), ((), ())),
                            preferred_element_type=F32) + bias_ref[...]
        m = jnp.max(s, axis=-1, keepdims=True)
        p = jnp.exp2(s - m).astype(BF16)
        r = jnp.dot(p, ve_ref[layout, krows, :], preferred_element_type=F32)
        acc = jnp.where(first, r[0:T, 0:LANES], r[T:2 * T, 0:LANES])
        lsum = jnp.where(first, r[0:T, LANES:2 * LANES], r[T:2 * T, LANES:2 * LANES])
        return acc, jnp.where(first, m[0:T], m[T:2 * T]), lsum

    def rows_of(start, n=T):
        return pl.ds(start, n)

    def store1(rows, acc, mb, lsum):
        acc1_ref[rows, :] = acc
        m1_ref[rows, :] = mb
        l1_ref[rows, :] = lsum

    store1(rows_of(0), *block(NAT, rows_of(0), rows_of(0), cur_ref))
    for n in range(1, seq // T):
        store1(rows_of(n * T), *block(NAT, rows_of(n * T), rows_of((n - 1) * T, 2 * T), band_ref))

    for r16 in range(d16):
        rows = rows_of(r16 * T)
        acc, mb, lsum = block(P16, rows, rows, cur_ref)
        dst = pl.ds((r16 % d4) * sub4 + r16 // d4, T, stride=d4)
        acc3_ref[dst, :] = acc
        m3_ref[dst, :] = mb
        l3_ref[dst, :] = lsum

    def finish(r4, n, keys, bias_ref):
        prow = rows_of(r4 * sub4 + n * T)
        acc2, mb2, l2 = block(P4, prow, keys, bias_ref)
        trow = pl.ds(r4 + n * (T * d4), T, stride=d4)
        acc1, mb1, l1 = acc1_ref[trow, :], m1_ref[trow, :], l1_ref[trow, :]
        acc3, mb3, l3 = acc3_ref[prow, :], m3_ref[prow, :], l3_ref[prow, :]
        m = jnp.maximum(mb1, jnp.maximum(mb2, mb3))
        w1, w2, w3 = jnp.exp2(mb1 - m), jnp.exp2(mb2 - m), jnp.exp2(mb3 - m)
        num = w1 * acc1 + w2 * acc2 + w3 * acc3
        den = w1 * l1 + w2 * l2 + w3 * l3
        o_ref[trow, :] = num / den

    for r4 in range(d4):
        finish(r4, 0, rows_of(r4 * sub4), cur_ref)
        for n in range(1, sub4 // T):
            finish(r4, n, rows_of(r4 * sub4 + (n - 1) * T, 2 * T), band_ref)


def _attention(qkv):
    b, s, _ = qkv.shape
    for window, dil in DILATED_PAIRS:
        assert window // dil == ATT_BLOCK and s % (ATT_BLOCK * dil) == 0
    assert DILATED_PAIRS[0][1] == 1 and DILATED_PAIRS[2][1] == DILATED_PAIRS[1][1] ** 2
    n_pairs = ATT_WIDTH // LANES
    spec = lambda part: pl.BlockSpec((None, s, LANES), lambda bi, hp: (bi, 0, part * n_pairs + hp))
    return pl.pallas_call(
        _att_kernel,
        grid=(b, n_pairs),
        in_specs=[spec(0), spec(1), spec(2)],
        out_specs=pl.BlockSpec((None, s, LANES), lambda bi, hp: (bi, 0, hp)),
        out_shape=jax.ShapeDtypeStruct((b, s, ATT_WIDTH), F32),
        scratch_shapes=[pltpu.VMEM((3, s, LANES), BF16)] * 3 + [pltpu.VMEM((3, s, 2 * LANES), BF16)]
        + [pltpu.VMEM((s, LANES), F32)] * 9
        + [pltpu.VMEM((2 * ATT_BLOCK, 2 * ATT_BLOCK), F32), pltpu.VMEM((2 * ATT_BLOCK, ATT_BLOCK), F32)],
        compiler_params=_params(2),
        name="dilated_attention",
    )(qkv, qkv, qkv)


SSD_HALO = 8
HEADS_PER_GROUP = SSD_HEADS // SSD_GROUPS
GROUP_LANES = HEADS_PER_GROUP * SSD_HEADDIM


def _ssd_kernel(z_ref, xbc_ref, dt_ref, cw_ref, cb_ref, dtb_ref, alog_ref, dsk_ref, ng_ref,
                o_ref, state_ref, ext_ref):
    n_chunks = z_ref.shape[0] // SSD_CHUNK
    L = SSD_CHUNK
    state_ref[...] = jnp.zeros_like(state_ref)
    row = lax.broadcasted_iota(jnp.int32, (L, L), 0)
    col = lax.broadcasted_iota(jnp.int32, (L, L), 1)
    tril = row >= col
    cumsum_mat = tril.astype(F32)
    lane = lax.broadcasted_iota(jnp.int32, (1, LANES), 1)
    lane_w = lax.broadcasted_iota(jnp.int32, (1, SSD_WIDTH), 1)
    first_group = lane_w < GROUP_LANES
    first_head = lane < SSD_HEADDIM
    a_neg = -jnp.exp(alog_ref[...])
    n_b = SSD_GROUPS * SSD_STATE

    def chunk(c, carry):
        r0 = pl.multiple_of(c * L, L)
        rows = pl.ds(r0, L)
        halo_start = pl.multiple_of(jnp.maximum(r0 - SSD_HALO, 0), SSD_HALO)
        halo = xbc_ref[pl.ds(halo_start, SSD_HALO), :]
        ext_ref[0:SSD_HALO, :] = jnp.where(c > 0, halo, jnp.zeros_like(halo))
        ext_ref[SSD_HALO:, :] = xbc_ref[rows, :]
        conv = cb_ref[...]
        for w in range(SSD_CONV):
            o = SSD_HALO - (SSD_CONV - 1) + w
            conv = conv + cw_ref[w:w + 1, :] * ext_ref[o:o + L, :]
        xact = _silu(conv)
        xs = xact[:, :SSD_WIDTH]
        bm = [xact[:, SSD_WIDTH + g * SSD_STATE:SSD_WIDTH + (g + 1) * SSD_STATE] for g in range(SSD_GROUPS)]
        cm = [xact[:, SSD_WIDTH + n_b + g * SSD_STATE:SSD_WIDTH + n_b + (g + 1) * SSD_STATE]
              for g in range(SSD_GROUPS)]
        bmt16 = [t.T.astype(BF16) for t in bm]
        cm16 = [t.astype(BF16) for t in cm]

        dt = jax.nn.softplus(dt_ref[rows, :] + dtb_ref[...])
        a = dt * a_neg
        acs = jnp.dot(cumsum_mat, a, precision=lax.Precision.HIGHEST, preferred_element_type=F32)
        acs_t = acs.T
        dt_t = dt.T
        acs_last = acs[L - 1:L, :]
        exp_acs_h = jnp.exp(acs)
        to_end_h = jnp.exp(acs_last - acs) * dt
        chunk_decay_h = jnp.exp(acs_last)
        cb = [jnp.dot(cm16[g], bmt16[g], preferred_element_type=F32) for g in range(SSD_GROUPS)]

        y_diag, e_pairs, w_pairs, d_pairs = [], [], [], []
        for p in range(SSD_HEADS // 2):
            xs_pair = xs[:, p * LANES:(p + 1) * LANES].astype(BF16)
            yd, ecol, wcol, dcol = [], [], [], []
            for h in (2 * p, 2 * p + 1):
                g = h // HEADS_PER_GROUP
                acs_col = jnp.broadcast_to(acs[:, h:h + 1], (L, L))
                seg = acs_col - acs_t[h:h + 1, :]
                decay = jnp.exp(jnp.where(tril, seg, -jnp.inf))
                mix = (cb[g] * decay * dt_t[h:h + 1, :]).astype(BF16)
                yd.append(jnp.dot(mix, xs_pair, preferred_element_type=F32))
                ecol.append(jnp.broadcast_to(exp_acs_h[:, h:h + 1], (L, LANES)))
                wcol.append(jnp.broadcast_to(to_end_h[:, h:h + 1], (L, LANES)))
                dcol.append(jnp.broadcast_to(chunk_decay_h[:, h:h + 1], (1, LANES)))
            y_diag.append(jnp.where(first_head, yd[0], yd[1]))
            e_pairs.append(jnp.where(first_head, ecol[0], ecol[1]))
            w_pairs.append(jnp.where(first_head, wcol[0], wcol[1]))
            d_pairs.append(jnp.where(first_head, dcol[0], dcol[1]))
        y_diag = jnp.concatenate(y_diag, axis=1)
        exp_acs = jnp.concatenate(e_pairs, axis=1)
        to_end = jnp.concatenate(w_pairs, axis=1)
        chunk_decay = jnp.concatenate(d_pairs, axis=1)

        state = state_ref[...]
        st16 = state.astype(BF16)
        y_off = jnp.where(first_group,
                          jnp.dot(cm16[0], st16, preferred_element_type=F32),
                          jnp.dot(cm16[1], st16, preferred_element_type=F32)) * exp_acs
        xdd = (xs * to_end).astype(BF16)
        new = jnp.where(first_group,
                        jnp.dot(bmt16[0], xdd, preferred_element_type=F32),
                        jnp.dot(bmt16[1], xdd, preferred_element_type=F32))
        state_ref[...] = state * chunk_decay + new

        y = y_diag + y_off + dsk_ref[...] * xs
        y = y * _silu(z_ref[rows, :])
        ysq = y * y
        s0 = jnp.sum(jnp.where(first_group, ysq, 0.0), axis=-1, keepdims=True)
        s1 = jnp.sum(jnp.where(first_group, 0.0, ysq), axis=-1, keepdims=True)
        ms = jnp.where(first_group, s0, s1) * (1.0 / GROUP_LANES)
        o_ref[rows, :] = (y * lax.rsqrt(ms + RMS_EPS) * ng_ref[...]).astype(o_ref.dtype)
        return carry

    lax.fori_loop(0, n_chunks, chunk, 0, unroll=2)


def _ssd(z, xbc, dt, conv_w, conv_b, dt_bias, a_log, d_skip, norm_g):
    b, s, _ = z.shape
    pad = lambda v: jnp.pad(v, (0, DT_PAD - SSD_HEADS)).reshape(1, DT_PAD)
    seq = lambda w: pl.BlockSpec((None, s, w), lambda bi: (bi, 0, 0))
    full = lambda r, w: pl.BlockSpec((r, w), lambda bi: (0, 0))
    return pl.pallas_call(
        _ssd_kernel,
        grid=(b,),
        in_specs=[seq(SSD_WIDTH), seq(SSD_CONV_DIM), seq(DT_PAD),
                  full(SSD_CONV, SSD_CONV_DIM), full(1, SSD_CONV_DIM), full(1, DT_PAD), full(1, DT_PAD),
                  full(1, SSD_WIDTH), full(1, SSD_WIDTH)],
        out_specs=seq(SSD_WIDTH),
        out_shape=jax.ShapeDtypeStruct((b, s, SSD_WIDTH), BF16),
        scratch_shapes=[pltpu.VMEM((SSD_STATE, SSD_WIDTH), F32),
                        pltpu.VMEM((SSD_HALO + SSD_CHUNK, SSD_CONV_DIM), F32)],
        compiler_params=_params(1),
        name="ssd",
    )(z, xbc, dt, conv_w, conv_b.reshape(1, SSD_CONV_DIM), pad(dt_bias), pad(a_log),
      jnp.repeat(d_skip, SSD_HEADDIM).reshape(1, SSD_WIDTH), norm_g.reshape(1, SSD_WIDTH))


SGU_ROWS = 512


def _sgu_kernel(uv_ref, lng_ref, lnb_ref, w_ref, bs_ref, o_ref):
    uv = uv_ref[...]
    act = 0.5 * uv * (1.0 + lax.erf(uv * (1.0 / math.sqrt(2.0))))
    u = act[:, :SGU_WIDTH]
    v = act[:, SGU_WIDTH:]
    mu = jnp.mean(v, axis=-1, keepdims=True)
    var = jnp.mean(jnp.square(v - mu), axis=-1, keepdims=True)
    vn = (v - mu) * lax.rsqrt(var + LN_EPS) * lng_ref[...] + lnb_ref[...]
    row = lax.broadcasted_iota(jnp.int32, (SGU_CHUNK, SGU_CHUNK), 0)
    col = lax.broadcasted_iota(jnp.int32, (SGU_CHUNK, SGU_CHUNK), 1)
    w = [jnp.where(row >= col, w_ref[g], 0.0).astype(BF16) for g in range(SGU_GROUPS)]
    lane = lax.broadcasted_iota(jnp.int32, (1, LANES), 1)
    first = lane < SGU_GROUP_DIM
    for c in range(SGU_ROWS // SGU_CHUNK):
        rows = slice(c * SGU_CHUNK, (c + 1) * SGU_CHUNK)
        mixed = []
        for p in range(SGU_WIDTH // LANES):
            vp = vn[rows, p * LANES:(p + 1) * LANES]
            lo = jnp.where(first, vp, 0.0).astype(BF16)
            hi = jnp.where(first, 0.0, vp).astype(BF16)
            mixed.append(jnp.dot(w[2 * p], lo, preferred_element_type=F32)
                         + jnp.dot(w[2 * p + 1], hi, preferred_element_type=F32))
        mixed = jnp.concatenate(mixed, axis=1) + bs_ref[...]
        o_ref[rows, :] = (u[rows, :] * mixed).astype(o_ref.dtype)


def _sgu(uv, ln_g, ln_b, w_s, b_s):
    b, s, _ = uv.shape
    bias = jnp.repeat(b_s.T, SGU_GROUP_DIM, axis=1)
    vec = pl.BlockSpec((1, SGU_WIDTH), lambda bi, r: (0, 0))
    return pl.pallas_call(
        _sgu_kernel,
        grid=(b, s // SGU_ROWS),
        in_specs=[pl.BlockSpec((None, SGU_ROWS, UV_WIDTH), lambda bi, r: (bi, r, 0)), vec, vec,
                  pl.BlockSpec((SGU_GROUPS, SGU_CHUNK, SGU_CHUNK), lambda bi, r: (0, 0, 0)),
                  pl.BlockSpec((SGU_CHUNK, SGU_WIDTH), lambda bi, r: (0, 0))],
        out_specs=pl.BlockSpec((None, SGU_ROWS, SGU_WIDTH), lambda bi, r: (bi, r, 0)),
        out_shape=jax.ShapeDtypeStruct((b, s, SGU_WIDTH), BF16),
        compiler_params=_params(2),
        name="sgu",
    )(uv, ln_g.reshape(1, SGU_WIDTH), ln_b.reshape(1, SGU_WIDTH), w_s, bias)


def _mixers(x, b, s, layer, gain, w_in, conv_w, conv_b, dt_bias, a_log, d_skip, ssd_norm,
            sgu_ln_g, sgu_ln_b, sgu_w, sgu_b):
    qkv, z, xbc, uv, dt = _inproj(x, layer, gain, w_in)
    seq = lambda t: t.reshape(b, s, t.shape[-1])
    y_att = _attention(seq(qkv))
    y_ssd = _ssd(seq(z), seq(xbc), seq(dt), conv_w, conv_b, dt_bias, a_log, d_skip, ssd_norm)
    y_sgu = _sgu(seq(uv), sgu_ln_g, sgu_ln_b, sgu_w, sgu_b)
    flat = lambda t: t.reshape(b * s, t.shape[-1])
    return flat(y_att), flat(y_ssd), flat(y_sgu)


def kernel(x, ffn1_norm, ffn1_w_gate, ffn1_w_up, ffn1_w_down, mix_norm, w_in, conv_w, conv_b, dt_bias, a_log, d_skip, ssd_norm, sgu_ln_g, sgu_ln_b, sgu_w, sgu_b, w_out, ffn2_norm, ffn2_w_gate, ffn2_w_up, ffn2_w_down, final_norm):
    b, s, d = x.shape
    depth = ffn1_norm.shape[0]
    h = x.reshape(b * s, d)
    for i in range(depth):
        h = _ffn(h, i, ffn1_norm, ffn1_w_gate, ffn1_w_up, ffn1_w_down)
        ys = _mixers(h, b, s, i, mix_norm, w_in, conv_w[i], conv_b[i], dt_bias[i], a_log[i], d_skip[i],
                     ssd_norm[i], sgu_ln_g[i], sgu_ln_b[i], sgu_w[i], sgu_b[i])
        h = _ffn(h, i, ffn2_norm, ffn2_w_gate, ffn2_w_up, ffn2_w_down, mix=(*ys, w_out),
                 final_gain=final_norm if i == depth - 1 else None)
    return h.reshape(b, s, d)
```

```python
import functools
import math

import numpy as np
import jax
import jax.numpy as jnp
from jax import lax
from jax.experimental import pallas as pl
from jax.experimental.pallas import tpu as pltpu

F32 = jnp.float32
BF16 = jnp.bfloat16

D_MODEL = 1024
D_FF = 2816
HEAD_DIM = 64
ATT_HEADS = 6
ATT_WIDTH = ATT_HEADS * HEAD_DIM
DILATED_PAIRS = ((128, 1), (512, 4), (2048, 16))
SSD_HEADS = 6
SSD_HEADDIM = 64
SSD_WIDTH = SSD_HEADS * SSD_HEADDIM
SSD_GROUPS = 2
SSD_STATE = 128
SSD_CONV = 4
SSD_CHUNK = 128
SSD_CONV_DIM = SSD_WIDTH + 2 * SSD_GROUPS * SSD_STATE
SGU_GROUPS = 4
SGU_GROUP_DIM = 64
SGU_WIDTH = SGU_GROUPS * SGU_GROUP_DIM
SGU_CHUNK = 128
RMS_EPS = 1e-6
LN_EPS = 1e-5

LANES = 128
DT_PAD = LANES
QKV_WIDTH = 3 * ATT_WIDTH
UV_WIDTH = 2 * SGU_WIDTH
PROJ_WIDTH = QKV_WIDTH + SSD_WIDTH + SSD_CONV_DIM + UV_WIDTH + DT_PAD

VMEM_LIMIT = 56 * 1024 * 1024

ROW_TILE = 512
FF_CHUNK = 256
PROJ_CHUNK = 512

ATT_BLOCK = 128


def _params(n_axes):
    return pltpu.CompilerParams(dimension_semantics=("arbitrary",) * n_axes,
                                vmem_limit_bytes=VMEM_LIMIT)


def _rmsnorm_f32(x, g):
    ms = jnp.mean(x * x, axis=-1, keepdims=True)
    return x * lax.rsqrt(ms + RMS_EPS) * g


def _silu(x):
    return x * jax.nn.sigmoid(x)


N_FF_CHUNKS = D_FF // FF_CHUNK
N_WO_CHUNKS = D_MODEL // FF_CHUNK
N_SSD_REFS = 9


def _ffn_kernel(*refs, mixed, final_norm, tiles_per_seq):
    refs = list(refs)
    x_ref = refs.pop(0)
    if mixed:
        ya_ref, yg_ref, wo_ref = refs[:3]
        ssd_in = refs[3:3 + N_SSD_REFS]
        del refs[:3 + N_SSD_REFS]
    g_ref, wg_ref, wu_ref, wd_ref = refs[:4]
    del refs[:4]
    fg_ref = refs.pop(0) if final_norm else None
    o_ref, wg16_ref, wu16_ref, wd16_ref, xn_ref, h_ref, res_ref = refs[:7]
    if mixed:
        wo16_ref, ys_ref, state_ref, halo_ref, ext_ref = refs[7:]
    step = pl.program_id(0)

    @pl.when(step < N_FF_CHUNKS)
    def _load_weights():
        wg16_ref[step] = wg_ref[...].astype(BF16)
        wu16_ref[step] = wu_ref[...].astype(BF16)
        wd16_ref[pl.ds(pl.multiple_of(step * FF_CHUNK, FF_CHUNK), FF_CHUNK), :] = wd_ref[...].astype(BF16)
        if mixed:
            @pl.when(step < N_WO_CHUNKS)
            def _():
                wo16_ref[pl.ds(pl.multiple_of(step * FF_CHUNK, FF_CHUNK), FF_CHUNK), :] = wo_ref[...].astype(BF16)

            @pl.when(step == 0)
            def _():
                ys_ref[...] = jnp.zeros_like(ys_ref)

    @pl.when(step >= N_FF_CHUNKS)
    def _row_tile():
        x = x_ref[...]
        if mixed:
            a, b = ATT_WIDTH, ATT_WIDTH + SSD_WIDTH
            x = x + (jnp.dot(ya_ref[...].astype(BF16), wo16_ref[0:a, :], preferred_element_type=F32)
                     + jnp.dot(ys_ref[...], wo16_ref[a:b, :], preferred_element_type=F32)
                     + jnp.dot(yg_ref[...], wo16_ref[b:, :], preferred_element_type=F32))
        res_ref[...] = x
        xn_ref[...] = _rmsnorm_f32(x, g_ref[...]).astype(BF16)
        ssd_chunks = iter(())
        if mixed:
            fresh = lax.rem(step - N_FF_CHUNKS, tiles_per_seq) == 0
            ssd_chunks = _ssd_tile(fresh, *ssd_in, ys_ref, state_ref, halo_ref, ext_ref)
        for f in range(N_FF_CHUNKS):
            xn = xn_ref[...]
            gate = jnp.dot(xn, wg16_ref[f], preferred_element_type=F32)
            up = jnp.dot(xn, wu16_ref[f], preferred_element_type=F32)
            h_ref[:, f * FF_CHUNK:(f + 1) * FF_CHUNK] = (_silu(gate) * up).astype(BF16)
        pieces = []
        for n in range(N_WO_CHUNKS):
            pieces.append(jnp.dot(h_ref[...], wd16_ref[:, n * FF_CHUNK:(n + 1) * FF_CHUNK],
                                  preferred_element_type=F32))
            next(ssd_chunks, None)
        for _ in ssd_chunks:
            pass
        y = jnp.concatenate(pieces, axis=1)
        out = res_ref[...] + 0.5 * y
        if final_norm:
            out = _rmsnorm_f32(out, fg_ref[...])
        o_ref[...] = out


def _ffn(x, layer, gain, w_gate, w_up, w_down, mix=None, final_gain=None):
    m = x.shape[0]
    n_tiles = m // ROW_TILE
    lag = 0 if mix is None else 1
    tile = lambda i: jnp.clip(i - N_FF_CHUNKS - lag, 0, n_tiles - 1)
    ahead = lambda i: jnp.clip(i - N_FF_CHUNKS, 0, n_tiles - 1)
    chunk = lambda i: jnp.minimum(i, N_FF_CHUNKS - 1)
    row = lambda n: pl.BlockSpec((ROW_TILE, n), lambda i: (tile(i), 0))
    full = lambda r, c: pl.BlockSpec((r, c), lambda i: (0, 0))
    in_specs, args = [row(D_MODEL)], [x]
    scratch = [pltpu.VMEM((N_FF_CHUNKS, D_MODEL, FF_CHUNK), BF16), pltpu.VMEM((N_FF_CHUNKS, D_MODEL, FF_CHUNK), BF16),
               pltpu.VMEM((D_FF, D_MODEL), BF16),
               pltpu.VMEM((ROW_TILE, D_MODEL), BF16), pltpu.VMEM((ROW_TILE, D_FF), BF16),
               pltpu.VMEM((ROW_TILE, D_MODEL), F32)]
    tiles_per_seq = None
    if mix is not None:
        y_att, y_sgu, w_out, tiles_per_seq, ssd_args = mix
        ssd_arrays, ssd_specs, ssd_scratch = _ssd_operands(
            *ssd_args, row_spec=lambda n: pl.BlockSpec((ROW_TILE, n), lambda i: (ahead(i), 0)), const_spec=full)
        in_specs += [row(ATT_WIDTH), row(SGU_WIDTH),
                     pl.BlockSpec((None, FF_CHUNK, D_MODEL), lambda i: (layer, jnp.minimum(i, N_WO_CHUNKS - 1), 0))]
        in_specs += ssd_specs
        args += [y_att, y_sgu, w_out] + ssd_arrays
        scratch += [pltpu.VMEM((D_MODEL, D_MODEL), BF16)] + ssd_scratch
    in_specs += [full(1, D_MODEL),
                 pl.BlockSpec((None, D_MODEL, FF_CHUNK), lambda i: (layer, 0, chunk(i))),
                 pl.BlockSpec((None, D_MODEL, FF_CHUNK), lambda i: (layer, 0, chunk(i))),
                 pl.BlockSpec((None, FF_CHUNK, D_MODEL), lambda i: (layer, chunk(i), 0))]
    args += [gain[layer].reshape(1, D_MODEL), w_gate, w_up, w_down]
    if final_gain is not None:
        in_specs.append(full(1, D_MODEL))
        args.append(final_gain.reshape(1, D_MODEL))
    return pl.pallas_call(
        functools.partial(_ffn_kernel, mixed=mix is not None, final_norm=final_gain is not None,
                          tiles_per_seq=tiles_per_seq),
        grid=(N_FF_CHUNKS + n_tiles + lag,),
        in_specs=in_specs,
        out_specs=row(D_MODEL),
        out_shape=jax.ShapeDtypeStruct((m, D_MODEL), F32),
        scratch_shapes=scratch,
        compiler_params=_params(1),
        name="ffn",
    )(*args)


_PROJ_PIECES = (("qkv", QKV_WIDTH, F32),("z", SSD_WIDTH, F32), ("xbc", SSD_CONV_DIM, F32),
                ("uv", UV_WIDTH, F32), ("dt", DT_PAD, F32))


RAW_WIDTH = QKV_WIDTH + SSD_WIDTH + SSD_CONV_DIM
RAW_CHUNK = 256
N_RAW_CHUNKS = RAW_WIDTH // RAW_CHUNK
RAW_REST = RAW_WIDTH - N_RAW_CHUNKS * RAW_CHUNK
TAIL_WIDTH = UV_WIDTH + DT_PAD
N_W_STEPS = N_RAW_CHUNKS + 1
N_PROJ_CHUNKS = PROJ_WIDTH // PROJ_CHUNK


def _inproj_kernel(x_ref, g_ref, w_ref, wrest_ref, wtail_ref, qkv_ref, z_ref, xbc_ref, uv_ref, dt_ref,
                   w16_ref, xn_ref):
    outs = (qkv_ref, z_ref, xbc_ref, uv_ref, dt_ref)
    step = pl.program_id(0)

    for c in range(N_RAW_CHUNKS):
        @pl.when(step == c)
        def _(c=c):
            w16_ref[:, c * RAW_CHUNK:(c + 1) * RAW_CHUNK] = w_ref[...].astype(BF16)

    @pl.when(step == N_RAW_CHUNKS)
    def _():
        w16_ref[:, N_RAW_CHUNKS * RAW_CHUNK:RAW_WIDTH] = wrest_ref[...].astype(BF16)
        w16_ref[:, RAW_WIDTH:] = wtail_ref[...].astype(BF16)

    @pl.when(step >= N_W_STEPS)
    def _row_tile():
        xn_ref[...] = _rmsnorm_f32(x_ref[...], g_ref[...]).astype(BF16)
        starts = np.cumsum([0] + [p[1] for p in _PROJ_PIECES])
        for c in range(N_PROJ_CHUNKS):
            lo, hi = c * PROJ_CHUNK, (c + 1) * PROJ_CHUNK
            r = jnp.dot(xn_ref[...], w16_ref[:, lo:hi], preferred_element_type=F32)
            for k, o_ref in enumerate(outs):
                a, b = max(lo, int(starts[k])), min(hi, int(starts[k + 1]))
                if a < b:
                    o_ref[:, a - int(starts[k]):b - int(starts[k])] = r[:, a - lo:b - lo].astype(o_ref.dtype)


def _inproj(x, layer, gain, w_in):
    m = x.shape[0]
    assert RAW_REST == LANES and RAW_WIDTH % LANES == 0
    dt0 = RAW_WIDTH
    uv0 = RAW_WIDTH + SSD_HEADS
    w_tail = jnp.concatenate([w_in[layer, :, uv0:uv0 + UV_WIDTH],
                              jnp.pad(w_in[layer, :, dt0:uv0], ((0, 0), (0, DT_PAD - SSD_HEADS)))], axis=1)
    row = lambda w: pl.BlockSpec((ROW_TILE, w), lambda i: (jnp.maximum(i - N_W_STEPS, 0), 0))
    return pl.pallas_call(
        _inproj_kernel,
        grid=(N_W_STEPS + m // ROW_TILE,),
        in_specs=[row(D_MODEL), pl.BlockSpec((1, D_MODEL), lambda i: (0, 0)),
                  pl.BlockSpec((None, D_MODEL, RAW_CHUNK), lambda i: (layer, 0, jnp.minimum(i, N_RAW_CHUNKS - 1))),
                  pl.BlockSpec((None, D_MODEL, RAW_REST), lambda i: (layer, 0, RAW_WIDTH // RAW_REST - 1)),
                  pl.BlockSpec((D_MODEL, TAIL_WIDTH), lambda i: (0, 0))],
        out_specs=[row(w) for _, w, _ in _PROJ_PIECES],
        out_shape=[jax.ShapeDtypeStruct((m, w), dt) for _, w, dt in _PROJ_PIECES],
        scratch_shapes=[pltpu.VMEM((D_MODEL, PROJ_WIDTH), BF16), pltpu.VMEM((ROW_TILE, D_MODEL), BF16)],
        compiler_params=_params(1),
        name="inproj",
    )(x, gain[layer].reshape(1, D_MODEL), w_in, w_in, w_tail)


NAT, P4, P16 = 0, 1, 2


def _att_kernel(q_ref, k_ref, v_ref, o_ref, qa_ref, qb_ref, kk_ref, ve_ref,
                acc1_ref, m1_ref, l1_ref, acc3_ref, m3_ref, l3_ref, q4_ref, k4_ref, v4_ref, band_ref, cur_ref):
    seq = q_ref.shape[0]
    T = ATT_BLOCK
    d4, d16 = DILATED_PAIRS[1][1], DILATED_PAIRS[2][1]
    sub4 = seq // d4
    lane = lax.broadcasted_iota(jnp.int32, (1, LANES), 1)
    first = lane < HEAD_DIM
    qi = lax.broadcasted_iota(jnp.int32, (T, T), 0)
    kj = lax.broadcasted_iota(jnp.int32, (T, T), 1)
    cur_bias = jnp.where(kj <= qi, 0.0, -jnp.inf).astype(F32)
    prev_bias = jnp.where(kj >= qi, 0.0, -jnp.inf).astype(F32)
    for half in range(2):
        cur_ref[half * T:(half + 1) * T, :] = cur_bias
        band_ref[half * T:(half + 1) * T, 0:T] = prev_bias
        band_ref[half * T:(half + 1) * T, T:2 * T] = cur_bias
    q_scale = HEAD_DIM ** -0.5 * math.log2(math.e)

    def prep(layout, dst, q, k, v):
        q = q * q_scale
        qa_ref[layout, dst, :] = jnp.where(first, q, 0.0).astype(BF16)
        qb_ref[layout, dst, :] = jnp.where(first, 0.0, q).astype(BF16)
        kk_ref[layout, dst, :] = k.astype(BF16)
        ve_ref[layout, dst, 0:LANES] = v.astype(BF16)
        ve_ref[layout, dst, LANES:2 * LANES] = jnp.ones((T, LANES), BF16)

    def prep_nat_p4(c, carry):
        rows = pl.ds(pl.multiple_of(c * T, T), T)
        prep(NAT, rows, q_ref[rows, :], k_ref[rows, :], v_ref[rows, :])
        src = pl.ds(c // d4 + (c % d4) * (T * d4), T, stride=d4)
        q, k, v = q_ref[src, :], k_ref[src, :], v_ref[src, :]
        q4_ref[rows, :] = q
        k4_ref[rows, :] = k
        v4_ref[rows, :] = v
        prep(P4, rows, q, k, v)
        return carry

    lax.fori_loop(0, seq // T, prep_nat_p4, 0)

    def prep_p16(r16, carry):
        rows = pl.ds(pl.multiple_of(r16 * T, T), T)
        src = pl.ds((r16 % d4) * sub4 + r16 // d4, T, stride=d4)
        prep(P16, rows, q4_ref[src, :], k4_ref[src, :], v4_ref[src, :])
        return carry

    lax.fori_loop(0, d16, prep_p16, 0)

    def block(layout, qrows, krows, bias_ref):
        q2 = jnp.concatenate([qa_ref[layout, qrows, :], qb_ref[layout, qrows, :]], axis=0)
        s = lax.dot_general(q2, kk_ref[layout, krows, :], (((1,), (1,)), ((), ())),
                            preferred_element_type=F32) + bias_ref[...]
        m = jnp.max(s, axis=-1, keepdims=True)
        p = jnp.exp2(s - m).astype(BF16)
        r = jnp.dot(p, ve_ref[layout, krows, :], preferred_element_type=F32)
        acc = jnp.where(first, r[0:T, 0:LANES], r[T:2 * T, 0:LANES])
        lsum = jnp.where(first, r[0:T, LANES:2 * LANES], r[T:2 * T, LANES:2 * LANES])
        return acc, jnp.where(first, m[0:T], m[T:2 * T]), lsum

    def rows_of(start, n=T):
        return pl.ds(start, n)

    def store1(rows, acc, mb, lsum):
        acc1_ref[rows, :] = acc
        m1_ref[rows, :] = mb
        l1_ref[rows, :] = lsum

    store1(rows_of(0), *block(NAT, rows_of(0), rows_of(0), cur_ref))
    for n in range(1, seq // T):
        store1(rows_of(n * T), *block(NAT, rows_of(n * T), rows_of((n - 1) * T, 2 * T), band_ref))

    for r16 in range(d16):
        rows = rows_of(r16 * T)
        acc, mb, lsum = block(P16, rows, rows, cur_ref)
        dst = pl.ds((r16 % d4) * sub4 + r16 // d4, T, stride=d4)
        acc3_ref[dst, :] = acc
        m3_ref[dst, :] = mb
        l3_ref[dst, :] = lsum

    def finish(r4, n, keys, bias_ref):
        prow = rows_of(r4 * sub4 + n * T)
        acc2, mb2, l2 = block(P4, prow, keys, bias_ref)
        trow = pl.ds(r4 + n * (T * d4), T, stride=d4)
        acc1, mb1, l1 = acc1_ref[trow, :], m1_ref[trow, :], l1_ref[trow, :]
        acc3, mb3, l3 = acc3_ref[prow, :], m3_ref[prow, :], l3_ref[prow, :]
        m = jnp.maximum(mb1, jnp.maximum(mb2, mb3))
        w1, w2, w3 = jnp.exp2(mb1 - m), jnp.exp2(mb2 - m), jnp.exp2(mb3 - m)
        num = w1 * acc1 + w2 * acc2 + w3 * acc3
        den = w1 * l1 + w2 * l2 + w3 * l3
        o_ref[trow, :] = num / den

    for r4 in range(d4):
        finish(r4, 0, rows_of(r4 * sub4), cur_ref)
        for n in range(1, sub4 // T):
            finish(r4, n, rows_of(r4 * sub4 + (n - 1) * T, 2 * T), band_ref)


def _attention(qkv):
    b, s, _ = qkv.shape
    for window, dil in DILATED_PAIRS:
        assert window // dil == ATT_BLOCK and s % (ATT_BLOCK * dil) == 0
    assert DILATED_PAIRS[0][1] == 1 and DILATED_PAIRS[2][1] == DILATED_PAIRS[1][1] ** 2
    n_pairs = ATT_WIDTH // LANES
    spec = lambda part: pl.BlockSpec((None, s, LANES), lambda bi, hp: (bi, 0, part * n_pairs + hp))
    return pl.pallas_call(
        _att_kernel,
        grid=(b, n_pairs),
        in_specs=[spec(0), spec(1), spec(2)],
        out_specs=pl.BlockSpec((None, s, LANES), lambda bi, hp: (bi, 0, hp)),
        out_shape=jax.ShapeDtypeStruct((b, s, ATT_WIDTH), F32),
        scratch_shapes=[pltpu.VMEM((3, s, LANES), BF16)] * 3 + [pltpu.VMEM((3, s, 2 * LANES), BF16)]
        + [pltpu.VMEM((s, LANES), F32)] * 9
        + [pltpu.VMEM((2 * ATT_BLOCK, 2 * ATT_BLOCK), F32), pltpu.VMEM((2 * ATT_BLOCK, ATT_BLOCK), F32)],
        compiler_params=_params(2),
        name="dilated_attention",
    )(qkv, qkv, qkv)


SSD_HALO = 8
HEADS_PER_GROUP = SSD_HEADS // SSD_GROUPS
GROUP_LANES = HEADS_PER_GROUP * SSD_HEADDIM


def _ssd_tile(fresh, z_ref, xbc_ref, dt_ref, cw_ref, cb_ref, dtb_ref, alog_ref, dsk_ref, ng_ref,
              o_ref, state_ref, halo_ref, ext_ref):
    n_chunks = z_ref.shape[0] // SSD_CHUNK
    L = SSD_CHUNK
    row = lax.broadcasted_iota(jnp.int32, (L, L), 0)
    col = lax.broadcasted_iota(jnp.int32, (L, L), 1)
    tril = row >= col
    cumsum_mat = tril.astype(F32)
    lane = lax.broadcasted_iota(jnp.int32, (1, LANES), 1)
    lane_w = lax.broadcasted_iota(jnp.int32, (1, SSD_WIDTH), 1)
    first_group = lane_w < GROUP_LANES
    first_head = lane < SSD_HEADDIM
    a_neg = -jnp.exp(alog_ref[...])
    n_b = SSD_GROUPS * SSD_STATE

    state = jnp.where(fresh, 0.0, state_ref[...])
    for c in range(n_chunks):
        rows = slice(c * L, (c + 1) * L)
        if c == 0:
            halo = jnp.where(fresh, 0.0, halo_ref[...])
        else:
            halo = xbc_ref[c * L - SSD_HALO:c * L, :]
        ext_ref[c, 0:SSD_HALO, :] = halo
        ext_ref[c, SSD_HALO:, :] = xbc_ref[rows, :]
        conv = cb_ref[...]
        for w in range(SSD_CONV):
            o = SSD_HALO - (SSD_CONV - 1) + w
            conv = conv + cw_ref[w:w + 1, :] * ext_ref[c, o:o + L, :]
        xact = _silu(conv)
        xs = xact[:, :SSD_WIDTH]
        bm = [xact[:, SSD_WIDTH + g * SSD_STATE:SSD_WIDTH + (g + 1) * SSD_STATE] for g in range(SSD_GROUPS)]
        cm = [xact[:, SSD_WIDTH + n_b + g * SSD_STATE:SSD_WIDTH + n_b + (g + 1) * SSD_STATE]
              for g in range(SSD_GROUPS)]
        bmt16 = [t.T.astype(BF16) for t in bm]
        cm16 = [t.astype(BF16) for t in cm]

        dt = jax.nn.softplus(dt_ref[rows, :] + dtb_ref[...])
        a = dt * a_neg
        acs = jnp.dot(cumsum_mat, a, precision=lax.Precision.HIGHEST, preferred_element_type=F32)
        acs_t = acs.T
        dt_t = dt.T
        acs_last = acs[L - 1:L, :]
        exp_acs_h = jnp.exp(acs)
        to_end_h = jnp.exp(acs_last - acs) * dt
        chunk_decay_h = jnp.exp(acs_last)
        cb = [jnp.dot(cm16[g], bmt16[g], preferred_element_type=F32) for g in range(SSD_GROUPS)]

        y_diag, e_pairs, w_pairs, d_pairs = [], [], [], []
        for p in range(SSD_HEADS // 2):
            xs_pair = xs[:, p * LANES:(p + 1) * LANES].astype(BF16)
            yd, ecol, wcol, dcol = [], [], [], []
            for h in (2 * p, 2 * p + 1):
                g = h // HEADS_PER_GROUP
                acs_col = jnp.broadcast_to(acs[:, h:h + 1], (L, L))
                seg = acs_col - acs_t[h:h + 1, :]
                decay = jnp.exp(jnp.where(tril, seg, -jnp.inf))
                mix = (cb[g] * decay * dt_t[h:h + 1, :]).astype(BF16)
                yd.append(jnp.dot(mix, xs_pair, preferred_element_type=F32))
                ecol.append(jnp.broadcast_to(exp_acs_h[:, h:h + 1], (L, LANES)))
                wcol.append(jnp.broadcast_to(to_end_h[:, h:h + 1], (L, LANES)))
                dcol.append(jnp.broadcast_to(chunk_decay_h[:, h:h + 1], (1, LANES)))
            y_diag.append(jnp.where(first_head, yd[0], yd[1]))
            e_pairs.append(jnp.where(first_head, ecol[0], ecol[1]))
            w_pairs.append(jnp.where(first_head, wcol[0], wcol[1]))
            d_pairs.append(jnp.where(first_head, dcol[0], dcol[1]))
        y_diag = jnp.concatenate(y_diag, axis=1)
        exp_acs = jnp.concatenate(e_pairs, axis=1)
        to_end = jnp.concatenate(w_pairs, axis=1)
        chunk_decay = jnp.concatenate(d_pairs, axis=1)

        st16 = state.astype(BF16)
        y_off = jnp.where(first_group,
                          jnp.dot(cm16[0], st16, preferred_element_type=F32),
                          jnp.dot(cm16[1], st16, preferred_element_type=F32)) * exp_acs
        xdd = (xs * to_end).astype(BF16)
        new = jnp.where(first_group,
                        jnp.dot(bmt16[0], xdd, preferred_element_type=F32),
                        jnp.dot(bmt16[1], xdd, preferred_element_type=F32))
        state = state * chunk_decay + new

        y = y_diag + y_off + dsk_ref[...] * xs
        y = y * _silu(z_ref[rows, :])
        ysq = y * y
        s0 = jnp.sum(jnp.where(first_group, ysq, 0.0), axis=-1, keepdims=True)
        s1 = jnp.sum(jnp.where(first_group, 0.0, ysq), axis=-1, keepdims=True)
        ms = jnp.where(first_group, s0, s1) * (1.0 / GROUP_LANES)
        o_ref[rows, :] = (y * lax.rsqrt(ms + RMS_EPS) * ng_ref[...]).astype(o_ref.dtype)
        if c == n_chunks - 1:
            state_ref[...] = state
            halo_ref[...] = xbc_ref[n_chunks * L - SSD_HALO:n_chunks * L, :]
        yield


def _ssd_operands(z, xbc, dt, conv_w, conv_b, dt_bias, a_log, d_skip, norm_g, row_spec, const_spec):
    pad = lambda v: jnp.pad(v, (0, DT_PAD - SSD_HEADS)).reshape(1, DT_PAD)
    arrays = [z, xbc, dt, conv_w, conv_b.reshape(1, SSD_CONV_DIM), pad(dt_bias), pad(a_log),
              jnp.repeat(d_skip, SSD_HEADDIM).reshape(1, SSD_WIDTH), norm_g.reshape(1, SSD_WIDTH)]
    specs = [row_spec(SSD_WIDTH), row_spec(SSD_CONV_DIM), row_spec(DT_PAD),
             const_spec(SSD_CONV, SSD_CONV_DIM), const_spec(1, SSD_CONV_DIM), const_spec(1, DT_PAD),
             const_spec(1, DT_PAD), const_spec(1, SSD_WIDTH), const_spec(1, SSD_WIDTH)]
    scratch = [pltpu.VMEM((ROW_TILE, SSD_WIDTH), BF16),
               pltpu.VMEM((SSD_STATE, SSD_WIDTH), F32),
               pltpu.VMEM((SSD_HALO, SSD_CONV_DIM), F32),
               pltpu.VMEM((ROW_TILE // SSD_CHUNK, SSD_HALO + SSD_CHUNK, SSD_CONV_DIM), F32)]
    return arrays, specs, scratch


SGU_ROWS = 512


def _sgu_kernel(uv_ref, lng_ref, lnb_ref, w_ref, bs_ref, o_ref):
    uv = uv_ref[...]
    act = 0.5 * uv * (1.0 + lax.erf(uv * (1.0 / math.sqrt(2.0))))
    u = act[:, :SGU_WIDTH]
    v = act[:, SGU_WIDTH:]
    mu = jnp.mean(v, axis=-1, keepdims=True)
    var = jnp.mean(jnp.square(v - mu), axis=-1, keepdims=True)
    vn = (v - mu) * lax.rsqrt(var + LN_EPS) * lng_ref[...] + lnb_ref[...]
    row = lax.broadcasted_iota(jnp.int32, (SGU_CHUNK, SGU_CHUNK), 0)
    col = lax.broadcasted_iota(jnp.int32, (SGU_CHUNK, SGU_CHUNK), 1)
    w = [jnp.where(row >= col, w_ref[g], 0.0).astype(BF16) for g in range(SGU_GROUPS)]
    lane = lax.broadcasted_iota(jnp.int32, (1, LANES), 1)
    first = lane < SGU_GROUP_DIM
    for c in range(SGU_ROWS // SGU_CHUNK):
        rows = slice(c * SGU_CHUNK, (c + 1) * SGU_CHUNK)
        mixed = []
        for p in range(SGU_WIDTH // LANES):
            vp = vn[rows, p * LANES:(p + 1) * LANES]
            lo = jnp.where(first, vp, 0.0).astype(BF16)
            hi = jnp.where(first, 0.0, vp).astype(BF16)
            mixed.append(jnp.dot(w[2 * p], lo, preferred_element_type=F32)
                         + jnp.dot(w[2 * p + 1], hi, preferred_element_type=F32))
        mixed = jnp.concatenate(mixed, axis=1) + bs_ref[...]
        o_ref[rows, :] = (u[rows, :] * mixed).astype(o_ref.dtype)


def _sgu(uv, ln_g, ln_b, w_s, b_s):
    b, s, _ = uv.shape
    bias = jnp.repeat(b_s.T, SGU_GROUP_DIM, axis=1)
    vec = pl.BlockSpec((1, SGU_WIDTH), lambda bi, r: (0, 0))
    return pl.pallas_call(
        _sgu_kernel,
        grid=(b, s // SGU_ROWS),
        in_specs=[pl.BlockSpec((None, SGU_ROWS, UV_WIDTH), lambda bi, r: (bi, r, 0)), vec, vec,
                  pl.BlockSpec((SGU_GROUPS, SGU_CHUNK, SGU_CHUNK), lambda bi, r: (0, 0, 0)),
                  pl.BlockSpec((SGU_CHUNK, SGU_WIDTH), lambda bi, r: (0, 0))],
        out_specs=pl.BlockSpec((None, SGU_ROWS, SGU_WIDTH), lambda bi, r: (bi, r, 0)),
        out_shape=jax.ShapeDtypeStruct((b, s, SGU_WIDTH), BF16),
        compiler_params=_params(2),
        name="sgu",
    )(uv, ln_g.reshape(1, SGU_WIDTH), ln_b.reshape(1, SGU_WIDTH), w_s, bias)


def _mixers(x, b, s, layer, gain, w_in, sgu_ln_g, sgu_ln_b, sgu_w, sgu_b):
    qkv, z, xbc, uv, dt = _inproj(x, layer, gain, w_in)
    seq = lambda t: t.reshape(b, s, t.shape[-1])
    y_att = _attention(seq(qkv))
    y_sgu = _sgu(seq(uv), sgu_ln_g, sgu_ln_b, sgu_w, sgu_b)
    flat = lambda t: t.reshape(b * s, t.shape[-1])
    return flat(y_att), flat(y_sgu), (z, xbc, dt)


def kernel(x, ffn1_norm, ffn1_w_gate, ffn1_w_up, ffn1_w_down, mix_norm, w_in, conv_w, conv_b, dt_bias, a_log, d_skip, ssd_norm, sgu_ln_g, sgu_ln_b, sgu_w, sgu_b, w_out, ffn2_norm, ffn2_w_gate, ffn2_w_up, ffn2_w_down, final_norm):
    b, s, d = x.shape
    depth = ffn1_norm.shape[0]
    h = x.reshape(b * s, d)
    for i in range(depth):
        h = _ffn(h, i, ffn1_norm, ffn1_w_gate, ffn1_w_up, ffn1_w_down)
        y_att, y_sgu, ssd_proj = _mixers(h, b, s, i, mix_norm, w_in, sgu_ln_g[i], sgu_ln_b[i], sgu_w[i], sgu_b[i])
        ssd_args = (*ssd_proj, conv_w[i], conv_b[i], dt_bias[i], a_log[i], d_skip[i], ssd_norm[i])
        h = _ffn(h, i, ffn2_norm, ffn2_w_gate, ffn2_w_up, ffn2_w_down,
                 mix=(y_att, y_sgu, w_out, s // ROW_TILE, ssd_args),
                 final_gain=final_norm if i == depth - 1 else None)
    return h.reshape(b, s, d)
```

```python
import functools
import math

import numpy as np
import jax
import jax.numpy as jnp
from jax import lax
from jax.experimental import pallas as pl
from jax.experimental.pallas import tpu as pltpu

F32 = jnp.float32
BF16 = jnp.bfloat16

D_MODEL = 1024
D_FF = 2816
HEAD_DIM = 64
ATT_HEADS = 6
ATT_WIDTH = ATT_HEADS * HEAD_DIM
DILATED_PAIRS = ((128, 1), (512, 4), (2048, 16))
SSD_HEADS = 6
SSD_HEADDIM = 64
SSD_WIDTH = SSD_HEADS * SSD_HEADDIM
SSD_GROUPS = 2
SSD_STATE = 128
SSD_CONV = 4
SSD_CHUNK = 128
SSD_CONV_DIM = SSD_WIDTH + 2 * SSD_GROUPS * SSD_STATE
SGU_GROUPS = 4
SGU_GROUP_DIM = 64
SGU_WIDTH = SGU_GROUPS * SGU_GROUP_DIM
SGU_CHUNK = 128
RMS_EPS = 1e-6
LN_EPS = 1e-5

LANES = 128
DT_PAD = LANES
QKV_WIDTH = 3 * ATT_WIDTH
UV_WIDTH = 2 * SGU_WIDTH
PROJ_WIDTH = QKV_WIDTH + SSD_WIDTH + SSD_CONV_DIM + UV_WIDTH + DT_PAD

VMEM_LIMIT = 56 * 1024 * 1024

ROW_TILE = 512
FF_CHUNK = 256
PROJ_CHUNK = 512

ATT_BLOCK = 128


def _params(n_axes):
    return pltpu.CompilerParams(dimension_semantics=("arbitrary",) * n_axes,
                                vmem_limit_bytes=VMEM_LIMIT)


def _rmsnorm_f32(x, g):
    ms = jnp.mean(x * x, axis=-1, keepdims=True)
    return x * lax.rsqrt(ms + RMS_EPS) * g


def _silu(x):
    return x * jax.nn.sigmoid(x)


N_FF_CHUNKS = D_FF // FF_CHUNK
N_WO_CHUNKS = D_MODEL // FF_CHUNK
N_SSD_REFS = 9
N_SGU_REFS = 5


def _ffn_kernel(*refs, mixed, final_norm, tiles_per_seq):
    refs = list(refs)
    x_ref = refs.pop(0)
    if mixed:
        ya_ref, wo_ref = refs[:2]
        ssd_in = refs[2:2 + N_SSD_REFS]
        sgu_in = refs[2 + N_SSD_REFS:2 + N_SSD_REFS + N_SGU_REFS]
        del refs[:2 + N_SSD_REFS + N_SGU_REFS]
    g_ref, wg_ref, wu_ref, wd_ref = refs[:4]
    del refs[:4]
    fg_ref = refs.pop(0) if final_norm else None
    o_ref, wg16_ref, wu16_ref, wd16_ref, xn_ref, h_ref, res_ref = refs[:7]
    if mixed:
        wo16_ref, ys_ref, state_ref, halo_ref, ext_ref, yg_ref = refs[7:]
    step = pl.program_id(0)

    @pl.when(step < N_FF_CHUNKS)
    def _load_weights():
        wg16_ref[step] = wg_ref[...].astype(BF16)
        wu16_ref[step] = wu_ref[...].astype(BF16)
        wd16_ref[pl.ds(pl.multiple_of(step * FF_CHUNK, FF_CHUNK), FF_CHUNK), :] = wd_ref[...].astype(BF16)
        if mixed:
            @pl.when(step < N_WO_CHUNKS)
            def _():
                wo16_ref[pl.ds(pl.multiple_of(step * FF_CHUNK, FF_CHUNK), FF_CHUNK), :] = wo_ref[...].astype(BF16)

            @pl.when(step == 0)
            def _():
                ys_ref[...] = jnp.zeros_like(ys_ref)
                yg_ref[...] = jnp.zeros_like(yg_ref)

    @pl.when(step >= N_FF_CHUNKS)
    def _row_tile():
        x = x_ref[...]
        if mixed:
            a, b = ATT_WIDTH, ATT_WIDTH + SSD_WIDTH
            x = x + (jnp.dot(ya_ref[...].astype(BF16), wo16_ref[0:a, :], preferred_element_type=F32)
                     + jnp.dot(ys_ref[...], wo16_ref[a:b, :], preferred_element_type=F32)
                     + jnp.dot(yg_ref[...], wo16_ref[b:, :], preferred_element_type=F32))
        res_ref[...] = x
        xn_ref[...] = _rmsnorm_f32(x, g_ref[...]).astype(BF16)
        ssd_chunks = iter(())
        if mixed:
            _sgu_tile(*sgu_in, yg_ref)
            fresh = lax.rem(step - N_FF_CHUNKS, tiles_per_seq) == 0
            ssd_chunks = _ssd_tile(fresh, *ssd_in, ys_ref, state_ref, halo_ref, ext_ref)
        for f in range(N_FF_CHUNKS):
            xn = xn_ref[...]
            gate = jnp.dot(xn, wg16_ref[f], preferred_element_type=F32)
            up = jnp.dot(xn, wu16_ref[f], preferred_element_type=F32)
            h_ref[:, f * FF_CHUNK:(f + 1) * FF_CHUNK] = (_silu(gate) * up).astype(BF16)
        pieces = []
        for n in range(N_WO_CHUNKS):
            pieces.append(jnp.dot(h_ref[...], wd16_ref[:, n * FF_CHUNK:(n + 1) * FF_CHUNK],
                                  preferred_element_type=F32))
            next(ssd_chunks, None)
        for _ in ssd_chunks:
            pass
        y = jnp.concatenate(pieces, axis=1)
        out = res_ref[...] + 0.5 * y
        if final_norm:
            out = _rmsnorm_f32(out, fg_ref[...])
        o_ref[...] = out


def _ffn(x, layer, gain, w_gate, w_up, w_down, mix=None, final_gain=None):
    m = x.shape[0]
    n_tiles = m // ROW_TILE
    lag = 0 if mix is None else 1
    tile = lambda i: jnp.clip(i - N_FF_CHUNKS - lag, 0, n_tiles - 1)
    ahead = lambda i: jnp.clip(i - N_FF_CHUNKS, 0, n_tiles - 1)
    chunk = lambda i: jnp.minimum(i, N_FF_CHUNKS - 1)
    row = lambda n: pl.BlockSpec((ROW_TILE, n), lambda i: (tile(i), 0))
    full = lambda r, c: pl.BlockSpec((r, c), lambda i: (0, 0))
    in_specs, args = [row(D_MODEL)], [x]
    scratch = [pltpu.VMEM((N_FF_CHUNKS, D_MODEL, FF_CHUNK), BF16), pltpu.VMEM((N_FF_CHUNKS, D_MODEL, FF_CHUNK), BF16),
               pltpu.VMEM((D_FF, D_MODEL), BF16),
               pltpu.VMEM((ROW_TILE, D_MODEL), BF16), pltpu.VMEM((ROW_TILE, D_FF), BF16),
               pltpu.VMEM((ROW_TILE, D_MODEL), F32)]
    tiles_per_seq = None
    if mix is not None:
        y_att, w_out, tiles_per_seq, ssd_args, sgu_args = mix
        next_row = lambda n: pl.BlockSpec((ROW_TILE, n), lambda i: (ahead(i), 0))
        ssd_arrays, ssd_specs, ssd_scratch = _ssd_operands(*ssd_args, row_spec=next_row, const_spec=full)
        sgu_arrays, sgu_specs, sgu_scratch = _sgu_operands(*sgu_args, row_spec=next_row, const_spec=full)
        in_specs += [row(ATT_WIDTH),
                     pl.BlockSpec((None, FF_CHUNK, D_MODEL), lambda i: (layer, jnp.minimum(i, N_WO_CHUNKS - 1), 0))]
        in_specs += ssd_specs + sgu_specs
        args += [y_att, w_out] + ssd_arrays + sgu_arrays
        scratch += [pltpu.VMEM((D_MODEL, D_MODEL), BF16)] + ssd_scratch + sgu_scratch
    in_specs += [full(1, D_MODEL),
                 pl.BlockSpec((None, D_MODEL, FF_CHUNK), lambda i: (layer, 0, chunk(i))),
                 pl.BlockSpec((None, D_MODEL, FF_CHUNK), lambda i: (layer, 0, chunk(i))),
                 pl.BlockSpec((None, FF_CHUNK, D_MODEL), lambda i: (layer, chunk(i), 0))]
    args += [gain[layer].reshape(1, D_MODEL), w_gate, w_up, w_down]
    if final_gain is not None:
        in_specs.append(full(1, D_MODEL))
        args.append(final_gain.reshape(1, D_MODEL))
    return pl.pallas_call(
        functools.partial(_ffn_kernel, mixed=mix is not None, final_norm=final_gain is not None,
                          tiles_per_seq=tiles_per_seq),
        grid=(N_FF_CHUNKS + n_tiles + lag,),
        in_specs=in_specs,
        out_specs=row(D_MODEL),
        out_shape=jax.ShapeDtypeStruct((m, D_MODEL), F32),
        scratch_shapes=scratch,
        compiler_params=_params(1),
        name="ffn",
    )(*args)


_PROJ_PIECES = (("qkv", QKV_WIDTH, F32),("z", SSD_WIDTH, F32), ("xbc", SSD_CONV_DIM, F32),
                ("uv", UV_WIDTH, F32), ("dt", DT_PAD, F32))


RAW_WIDTH = QKV_WIDTH + SSD_WIDTH + SSD_CONV_DIM
RAW_CHUNK = 256
N_RAW_CHUNKS = RAW_WIDTH // RAW_CHUNK
RAW_REST = RAW_WIDTH - N_RAW_CHUNKS * RAW_CHUNK
TAIL_WIDTH = UV_WIDTH + DT_PAD
N_W_STEPS = N_RAW_CHUNKS + 1
N_PROJ_CHUNKS = PROJ_WIDTH // PROJ_CHUNK


def _inproj_kernel(x_ref, g_ref, w_ref, wrest_ref, wtail_ref, qkv_ref, z_ref, xbc_ref, uv_ref, dt_ref,
                   w16_ref, xn_ref):
    outs = (qkv_ref, z_ref, xbc_ref, uv_ref, dt_ref)
    step = pl.program_id(0)

    for c in range(N_RAW_CHUNKS):
        @pl.when(step == c)
        def _(c=c):
            w16_ref[:, c * RAW_CHUNK:(c + 1) * RAW_CHUNK] = w_ref[...].astype(BF16)

    @pl.when(step == N_RAW_CHUNKS)
    def _():
        w16_ref[:, N_RAW_CHUNKS * RAW_CHUNK:RAW_WIDTH] = wrest_ref[...].astype(BF16)
        w16_ref[:, RAW_WIDTH:] = wtail_ref[...].astype(BF16)

    @pl.when(step >= N_W_STEPS)
    def _row_tile():
        xn_ref[...] = _rmsnorm_f32(x_ref[...], g_ref[...]).astype(BF16)
        starts = np.cumsum([0] + [p[1] for p in _PROJ_PIECES])
        for c in range(N_PROJ_CHUNKS):
            lo, hi = c * PROJ_CHUNK, (c + 1) * PROJ_CHUNK
            r = jnp.dot(xn_ref[...], w16_ref[:, lo:hi], preferred_element_type=F32)
            for k, o_ref in enumerate(outs):
                a, b = max(lo, int(starts[k])), min(hi, int(starts[k + 1]))
                if a < b:
                    o_ref[:, a - int(starts[k]):b - int(starts[k])] = r[:, a - lo:b - lo].astype(o_ref.dtype)


def _inproj(x, layer, gain, w_in):
    m = x.shape[0]
    assert RAW_REST == LANES and RAW_WIDTH % LANES == 0
    dt0 = RAW_WIDTH
    uv0 = RAW_WIDTH + SSD_HEADS
    w_tail = jnp.concatenate([w_in[layer, :, uv0:uv0 + UV_WIDTH],
                              jnp.pad(w_in[layer, :, dt0:uv0], ((0, 0), (0, DT_PAD - SSD_HEADS)))], axis=1)
    row = lambda w: pl.BlockSpec((ROW_TILE, w), lambda i: (jnp.maximum(i - N_W_STEPS, 0), 0))
    return pl.pallas_call(
        _inproj_kernel,
        grid=(N_W_STEPS + m // ROW_TILE,),
        in_specs=[row(D_MODEL), pl.BlockSpec((1, D_MODEL), lambda i: (0, 0)),
                  pl.BlockSpec((None, D_MODEL, RAW_CHUNK), lambda i: (layer, 0, jnp.minimum(i, N_RAW_CHUNKS - 1))),
                  pl.BlockSpec((None, D_MODEL, RAW_REST), lambda i: (layer, 0, RAW_WIDTH // RAW_REST - 1)),
                  pl.BlockSpec((D_MODEL, TAIL_WIDTH), lambda i: (0, 0))],
        out_specs=[row(w) for _, w, _ in _PROJ_PIECES],
        out_shape=[jax.ShapeDtypeStruct((m, w), dt) for _, w, dt in _PROJ_PIECES],
        scratch_shapes=[pltpu.VMEM((D_MODEL, PROJ_WIDTH), BF16), pltpu.VMEM((ROW_TILE, D_MODEL), BF16)],
        compiler_params=_params(1),
        name="inproj",
    )(x, gain[layer].reshape(1, D_MODEL), w_in, w_in, w_tail)


NAT, P4, P16 = 0, 1, 2


def _att_kernel(q_ref, k_ref, v_ref, o_ref, qa_ref, qb_ref, kk_ref, ve_ref,
                acc1_ref, m1_ref, l1_ref, acc3_ref, m3_ref, l3_ref, q4_ref, k4_ref, v4_ref, band_ref, cur_ref):
    seq = q_ref.shape[0]
    T = ATT_BLOCK
    d4, d16 = DILATED_PAIRS[1][1], DILATED_PAIRS[2][1]
    sub4 = seq // d4
    lane = lax.broadcasted_iota(jnp.int32, (1, LANES), 1)
    first = lane < HEAD_DIM
    qi = lax.broadcasted_iota(jnp.int32, (T, T), 0)
    kj = lax.broadcasted_iota(jnp.int32, (T, T), 1)
    cur_bias = jnp.where(kj <= qi, 0.0, -jnp.inf).astype(F32)
    prev_bias = jnp.where(kj >= qi, 0.0, -jnp.inf).astype(F32)
    for half in range(2):
        cur_ref[half * T:(half + 1) * T, :] = cur_bias
        band_ref[half * T:(half + 1) * T, 0:T] = prev_bias
        band_ref[half * T:(half + 1) * T, T:2 * T] = cur_bias
    q_scale = HEAD_DIM ** -0.5 * math.log2(math.e)

    def prep(layout, dst, q, k, v):
        q = q * q_scale
        qa_ref[layout, dst, :] = jnp.where(first, q, 0.0).astype(BF16)
        qb_ref[layout, dst, :] = jnp.where(first, 0.0, q).astype(BF16)
        kk_ref[layout, dst, :] = k.astype(BF16)
        ve_ref[layout, dst, 0:LANES] = v.astype(BF16)
        ve_ref[layout, dst, LANES:2 * LANES] = jnp.ones((T, LANES), BF16)

    def prep_nat_p4(c, carry):
        rows = pl.ds(pl.multiple_of(c * T, T), T)
        prep(NAT, rows, q_ref[rows, :], k_ref[rows, :], v_ref[rows, :])
        src = pl.ds(c // d4 + (c % d4) * (T * d4), T, stride=d4)
        q, k, v = q_ref[src, :], k_ref[src, :], v_ref[src, :]
        q4_ref[rows, :] = q
        k4_ref[rows, :] = k
        v4_ref[rows, :] = v
        prep(P4, rows, q, k, v)
        return carry

    lax.fori_loop(0, seq // T, prep_nat_p4, 0)

    def prep_p16(r16, carry):
        rows = pl.ds(pl.multiple_of(r16 * T, T), T)
        src = pl.ds((r16 % d4) * sub4 + r16 // d4, T, stride=d4)
        prep(P16, rows, q4_ref[src, :], k4_ref[src, :], v4_ref[src, :])
        return carry

    lax.fori_loop(0, d16, prep_p16, 0)

    def block(layout, qrows, krows, bias_ref):
        q2 = jnp.concatenate([qa_ref[layout, qrows, :], qb_ref[layout, qrows, :]], axis=0)
        s = lax.dot_general(q2, kk_ref[layout, krows, :], (((1,), (1,)), ((), ())),
                            preferred_element_type=F32) + bias_ref[...]
        m = jnp.max(s, axis=-1, keepdims=True)
        p = jnp.exp2(s - m).astype(BF16)
        r = jnp.dot(p, ve_ref[layout, krows, :], preferred_element_type=F32)
        acc = jnp.where(first, r[0:T, 0:LANES], r[T:2 * T, 0:LANES])
        lsum = jnp.where(first, r[0:T, LANES:2 * LANES], r[T:2 * T, LANES:2 * LANES])
        return acc, jnp.where(first, m[0:T], m[T:2 * T]), lsum

    def rows_of(start, n=T):
        return pl.ds(start, n)

    def store1(rows, acc, mb, lsum):
        acc1_ref[rows, :] = acc
        m1_ref[rows, :] = mb
        l1_ref[rows, :] = lsum

    store1(rows_of(0), *block(NAT, rows_of(0), rows_of(0), cur_ref))
    for n in range(1, seq // T):
        store1(rows_of(n * T), *block(NAT, rows_of(n * T), rows_of((n - 1) * T, 2 * T), band_ref))

    for r16 in range(d16):
        rows = rows_of(r16 * T)
        acc, mb, lsum = block(P16, rows, rows, cur_ref)
        dst = pl.ds((r16 % d4) * sub4 + r16 // d4, T, stride=d4)
        acc3_ref[dst, :] = acc
        m3_ref[dst, :] = mb
        l3_ref[dst, :] = lsum

    def finish(r4, n, keys, bias_ref):
        prow = rows_of(r4 * sub4 + n * T)
        acc2, mb2, l2 = block(P4, prow, keys, bias_ref)
        trow = pl.ds(r4 + n * (T * d4), T, stride=d4)
        acc1, mb1, l1 = acc1_ref[trow, :], m1_ref[trow, :], l1_ref[trow, :]
        acc3, mb3, l3 = acc3_ref[prow, :], m3_ref[prow, :], l3_ref[prow, :]
        m = jnp.maximum(mb1, jnp.maximum(mb2, mb3))
        w1, w2, w3 = jnp.exp2(mb1 - m), jnp.exp2(mb2 - m), jnp.exp2(mb3 - m)
        num = w1 * acc1 + w2 * acc2 + w3 * acc3
        den = w1 * l1 + w2 * l2 + w3 * l3
        o_ref[trow, :] = num / den

    for r4 in range(d4):
        finish(r4, 0, rows_of(r4 * sub4), cur_ref)
        for n in range(1, sub4 // T):
            finish(r4, n, rows_of(r4 * sub4 + (n - 1) * T, 2 * T), band_ref)


def _attention(qkv):
    b, s, _ = qkv.shape
    for window, dil in DILATED_PAIRS:
        assert window // dil == ATT_BLOCK and s % (ATT_BLOCK * dil) == 0
    assert DILATED_PAIRS[0][1] == 1 and DILATED_PAIRS[2][1] == DILATED_PAIRS[1][1] ** 2
    n_pairs = ATT_WIDTH // LANES
    spec = lambda part: pl.BlockSpec((None, s, LANES), lambda bi, hp: (bi, 0, part * n_pairs + hp))
    return pl.pallas_call(
        _att_kernel,
        grid=(b, n_pairs),
        in_specs=[spec(0), spec(1), spec(2)],
        out_specs=pl.BlockSpec((None, s, LANES), lambda bi, hp: (bi, 0, hp)),
        out_shape=jax.ShapeDtypeStruct((b, s, ATT_WIDTH), F32),
        scratch_shapes=[pltpu.VMEM((3, s, LANES), BF16)] * 3 + [pltpu.VMEM((3, s, 2 * LANES), BF16)]
        + [pltpu.VMEM((s, LANES), F32)] * 9
        + [pltpu.VMEM((2 * ATT_BLOCK, 2 * ATT_BLOCK), F32), pltpu.VMEM((2 * ATT_BLOCK, ATT_BLOCK), F32)],
        compiler_params=_params(2),
        name="dilated_attention",
    )(qkv, qkv, qkv)


SSD_HALO = 8
HEADS_PER_GROUP = SSD_HEADS // SSD_GROUPS
GROUP_LANES = HEADS_PER_GROUP * SSD_HEADDIM


def _ssd_tile(fresh, z_ref, xbc_ref, dt_ref, cw_ref, cb_ref, dtb_ref, alog_ref, dsk_ref, ng_ref,
              o_ref, state_ref, halo_ref, ext_ref):
    n_chunks = z_ref.shape[0] // SSD_CHUNK
    L = SSD_CHUNK
    row = lax.broadcasted_iota(jnp.int32, (L, L), 0)
    col = lax.broadcasted_iota(jnp.int32, (L, L), 1)
    tril = row >= col
    cumsum_mat = tril.astype(F32)
    lane = lax.broadcasted_iota(jnp.int32, (1, LANES), 1)
    lane_w = lax.broadcasted_iota(jnp.int32, (1, SSD_WIDTH), 1)
    first_group = lane_w < GROUP_LANES
    first_head = lane < SSD_HEADDIM
    a_neg = -jnp.exp(alog_ref[...])
    n_b = SSD_GROUPS * SSD_STATE

    state = jnp.where(fresh, 0.0, state_ref[...])
    for c in range(n_chunks):
        rows = slice(c * L, (c + 1) * L)
        if c == 0:
            halo = jnp.where(fresh, 0.0, halo_ref[...])
        else:
            halo = xbc_ref[c * L - SSD_HALO:c * L, :]
        ext_ref[c, 0:SSD_HALO, :] = halo
        ext_ref[c, SSD_HALO:, :] = xbc_ref[rows, :]
        conv = cb_ref[...]
        for w in range(SSD_CONV):
            o = SSD_HALO - (SSD_CONV - 1) + w
            conv = conv + cw_ref[w:w + 1, :] * ext_ref[c, o:o + L, :]
        xact = _silu(conv)
        xs = xact[:, :SSD_WIDTH]
        bm = [xact[:, SSD_WIDTH + g * SSD_STATE:SSD_WIDTH + (g + 1) * SSD_STATE] for g in range(SSD_GROUPS)]
        cm = [xact[:, SSD_WIDTH + n_b + g * SSD_STATE:SSD_WIDTH + n_b + (g + 1) * SSD_STATE]
              for g in range(SSD_GROUPS)]
        bmt16 = [t.T.astype(BF16) for t in bm]
        cm16 = [t.astype(BF16) for t in cm]

        dt = jax.nn.softplus(dt_ref[rows, :] + dtb_ref[...])
        a = dt * a_neg
        acs = jnp.dot(cumsum_mat, a, precision=lax.Precision.HIGHEST, preferred_element_type=F32)
        acs_t = acs.T
        dt_t = dt.T
        acs_last = acs[L - 1:L, :]
        exp_acs_h = jnp.exp(acs)
        to_end_h = jnp.exp(acs_last - acs) * dt
        chunk_decay_h = jnp.exp(acs_last)
        cb = [jnp.dot(cm16[g], bmt16[g], preferred_element_type=F32) for g in range(SSD_GROUPS)]

        y_diag, e_pairs, w_pairs, d_pairs = [], [], [], []
        for p in range(SSD_HEADS // 2):
            xs_pair = xs[:, p * LANES:(p + 1) * LANES].astype(BF16)
            yd, ecol, wcol, dcol = [], [], [], []
            for h in (2 * p, 2 * p + 1):
                g = h // HEADS_PER_GROUP
                acs_col = jnp.broadcast_to(acs[:, h:h + 1], (L, L))
                seg = acs_col - acs_t[h:h + 1, :]
                decay = jnp.exp(jnp.where(tril, seg, -jnp.inf))
                mix = (cb[g] * decay * dt_t[h:h + 1, :]).astype(BF16)
                yd.append(jnp.dot(mix, xs_pair, preferred_element_type=F32))
                ecol.append(jnp.broadcast_to(exp_acs_h[:, h:h + 1], (L, LANES)))
                wcol.append(jnp.broadcast_to(to_end_h[:, h:h + 1], (L, LANES)))
                dcol.append(jnp.broadcast_to(chunk_decay_h[:, h:h + 1], (1, LANES)))
            y_diag.append(jnp.where(first_head, yd[0], yd[1]))
            e_pairs.append(jnp.where(first_head, ecol[0], ecol[1]))
            w_pairs.append(jnp.where(first_head, wcol[0], wcol[1]))
            d_pairs.append(jnp.where(first_head, dcol[0], dcol[1]))
        y_diag = jnp.concatenate(y_diag, axis=1)
        exp_acs = jnp.concatenate(e_pairs, axis=1)
        to_end = jnp.concatenate(w_pairs, axis=1)
        chunk_decay = jnp.concatenate(d_pairs, axis=1)

        st16 = state.astype(BF16)
        y_off = jnp.where(first_group,
                          jnp.dot(cm16[0], st16, preferred_element_type=F32),
                          jnp.dot(cm16[1], st16, preferred_element_type=F32)) * exp_acs
        xdd = (xs * to_end).astype(BF16)
        new = jnp.where(first_group,
                        jnp.dot(bmt16[0], xdd, preferred_element_type=F32),
                        jnp.dot(bmt16[1], xdd, preferred_element_type=F32))
        state = state * chunk_decay + new

        y = y_diag + y_off + dsk_ref[...] * xs
        y = y * _silu(z_ref[rows, :])
        ysq = y * y
        s0 = jnp.sum(jnp.where(first_group, ysq, 0.0), axis=-1, keepdims=True)
        s1 = jnp.sum(jnp.where(first_group, 0.0, ysq), axis=-1, keepdims=True)
        ms = jnp.where(first_group, s0, s1) * (1.0 / GROUP_LANES)
        o_ref[rows, :] = (y * lax.rsqrt(ms + RMS_EPS) * ng_ref[...]).astype(o_ref.dtype)
        if c == n_chunks - 1:
            state_ref[...] = state
            halo_ref[...] = xbc_ref[n_chunks * L - SSD_HALO:n_chunks * L, :]
        yield


def _ssd_operands(z, xbc, dt, conv_w, conv_b, dt_bias, a_log, d_skip, norm_g, row_spec, const_spec):
    pad = lambda v: jnp.pad(v, (0, DT_PAD - SSD_HEADS)).reshape(1, DT_PAD)
    arrays = [z, xbc, dt, conv_w, conv_b.reshape(1, SSD_CONV_DIM), pad(dt_bias), pad(a_log),
              jnp.repeat(d_skip, SSD_HEADDIM).reshape(1, SSD_WIDTH), norm_g.reshape(1, SSD_WIDTH)]
    specs = [row_spec(SSD_WIDTH), row_spec(SSD_CONV_DIM), row_spec(DT_PAD),
             const_spec(SSD_CONV, SSD_CONV_DIM), const_spec(1, SSD_CONV_DIM), const_spec(1, DT_PAD),
             const_spec(1, DT_PAD), const_spec(1, SSD_WIDTH), const_spec(1, SSD_WIDTH)]
    scratch = [pltpu.VMEM((ROW_TILE, SSD_WIDTH), BF16),
               pltpu.VMEM((SSD_STATE, SSD_WIDTH), F32),
               pltpu.VMEM((SSD_HALO, SSD_CONV_DIM), F32),
               pltpu.VMEM((ROW_TILE // SSD_CHUNK, SSD_HALO + SSD_CHUNK, SSD_CONV_DIM), F32)]
    return arrays, specs, scratch


def _sgu_tile(uv_ref, lng_ref, lnb_ref, w_ref, bs_ref, o_ref):
    uv = uv_ref[...]
    act = 0.5 * uv * (1.0 + lax.erf(uv * (1.0 / math.sqrt(2.0))))
    u = act[:, :SGU_WIDTH]
    v = act[:, SGU_WIDTH:]
    mu = jnp.mean(v, axis=-1, keepdims=True)
    var = jnp.mean(jnp.square(v - mu), axis=-1, keepdims=True)
    vn = (v - mu) * lax.rsqrt(var + LN_EPS) * lng_ref[...] + lnb_ref[...]
    row = lax.broadcasted_iota(jnp.int32, (SGU_CHUNK, SGU_CHUNK), 0)
    col = lax.broadcasted_iota(jnp.int32, (SGU_CHUNK, SGU_CHUNK), 1)
    w = [jnp.where(row >= col, w_ref[g], 0.0).astype(BF16) for g in range(SGU_GROUPS)]
    lane = lax.broadcasted_iota(jnp.int32, (1, LANES), 1)
    first = lane < SGU_GROUP_DIM
    for c in range(uv_ref.shape[0] // SGU_CHUNK):
        rows = slice(c * SGU_CHUNK, (c + 1) * SGU_CHUNK)
        mixed = []
        for p in range(SGU_WIDTH // LANES):
            vp = vn[rows, p * LANES:(p + 1) * LANES]
            lo = jnp.where(first, vp, 0.0).astype(BF16)
            hi = jnp.where(first, 0.0, vp).astype(BF16)
            mixed.append(jnp.dot(w[2 * p], lo, preferred_element_type=F32)
                         + jnp.dot(w[2 * p + 1], hi, preferred_element_type=F32))
        mixed = jnp.concatenate(mixed, axis=1) + bs_ref[...]
        o_ref[rows, :] = (u[rows, :] * mixed).astype(o_ref.dtype)


def _sgu_operands(uv, ln_g, ln_b, w_s, b_s, row_spec, const_spec):
    bias = jnp.repeat(b_s.T, SGU_GROUP_DIM, axis=1)
    arrays = [uv, ln_g.reshape(1, SGU_WIDTH), ln_b.reshape(1, SGU_WIDTH), w_s, bias]
    specs = [row_spec(UV_WIDTH), const_spec(1, SGU_WIDTH), const_spec(1, SGU_WIDTH),
             pl.BlockSpec((SGU_GROUPS, SGU_CHUNK, SGU_CHUNK), lambda i: (0, 0, 0)),
             const_spec(SGU_CHUNK, SGU_WIDTH)]
    scratch = [pltpu.VMEM((ROW_TILE, SGU_WIDTH), BF16)]
    return arrays, specs, scratch


def _mixers(x, b, s, layer, gain, w_in):
    qkv, z, xbc, uv, dt = _inproj(x, layer, gain, w_in)
    y_att = _attention(qkv.reshape(b, s, QKV_WIDTH))
    return y_att.reshape(b * s, ATT_WIDTH), (z, xbc, dt), uv


def kernel(x, ffn1_norm, ffn1_w_gate, ffn1_w_up, ffn1_w_down, mix_norm, w_in, conv_w, conv_b, dt_bias, a_log, d_skip, ssd_norm, sgu_ln_g, sgu_ln_b, sgu_w, sgu_b, w_out, ffn2_norm, ffn2_w_gate, ffn2_w_up, ffn2_w_down, final_norm):
    b, s, d = x.shape
    depth = ffn1_norm.shape[0]
    h = x.reshape(b * s, d)
    for i in range(depth):
        h = _ffn(h, i, ffn1_norm, ffn1_w_gate, ffn1_w_up, ffn1_w_down)
        y_att, ssd_proj, uv = _mixers(h, b, s, i, mix_norm, w_in)
        ssd_args = (*ssd_proj, conv_w[i], conv_b[i], dt_bias[i], a_log[i], d_skip[i], ssd_norm[i])
        sgu_args = (uv, sgu_ln_g[i], sgu_ln_b[i], sgu_w[i], sgu_b[i])
        h = _ffn(h, i, ffn2_norm, ffn2_w_gate, ffn2_w_up, ffn2_w_down,
                 mix=(y_att, w_out, s // ROW_TILE, ssd_args, sgu_args),
                 final_gain=final_norm if i == depth - 1 else None)
    return h.reshape(b, s, d)
```

```python
import functools
import math

import numpy as np
import jax
import jax.numpy as jnp
from jax import lax
from jax.experimental import pallas as pl
from jax.experimental.pallas import tpu as pltpu

F32 = jnp.float32
BF16 = jnp.bfloat16

D_MODEL = 1024
D_FF = 2816
HEAD_DIM = 64
ATT_HEADS = 6
ATT_WIDTH = ATT_HEADS * HEAD_DIM
DILATED_PAIRS = ((128, 1), (512, 4), (2048, 16))
SSD_HEADS = 6
SSD_HEADDIM = 64
SSD_WIDTH = SSD_HEADS * SSD_HEADDIM
SSD_GROUPS = 2
SSD_STATE = 128
SSD_CONV = 4
SSD_CHUNK = 128
SSD_CONV_DIM = SSD_WIDTH + 2 * SSD_GROUPS * SSD_STATE
SGU_GROUPS = 4
SGU_GROUP_DIM = 64
SGU_WIDTH = SGU_GROUPS * SGU_GROUP_DIM
SGU_CHUNK = 128
RMS_EPS = 1e-6
LN_EPS = 1e-5

LANES = 128
DT_PAD = LANES
QKV_WIDTH = 3 * ATT_WIDTH
UV_WIDTH = 2 * SGU_WIDTH
PROJ_WIDTH = QKV_WIDTH + SSD_WIDTH + SSD_CONV_DIM + UV_WIDTH + DT_PAD

VMEM_LIMIT = 56 * 1024 * 1024

ROW_TILE = 512
FF_CHUNK = 256
PROJ_CHUNK = 512

ATT_BLOCK = 128
ATT_PIPE = 4


def _params(n_axes):
    return pltpu.CompilerParams(dimension_semantics=("arbitrary",) * n_axes,
                                vmem_limit_bytes=VMEM_LIMIT)


def _rmsnorm_f32(x, g):
    ms = jnp.mean(x * x, axis=-1, keepdims=True)
    return x * lax.rsqrt(ms + RMS_EPS) * g


def _silu(x):
    return x * jax.nn.sigmoid(x)


N_FF_CHUNKS = D_FF // FF_CHUNK
N_WO_CHUNKS = D_MODEL // FF_CHUNK
N_SSD_REFS = 9
N_SGU_REFS = 5


def _ffn_kernel(*refs, mixed, final_norm, tiles_per_seq):
    refs = list(refs)
    x_ref = refs.pop(0)
    if mixed:
        ya_ref, wo_ref = refs[:2]
        ssd_in = refs[2:2 + N_SSD_REFS]
        sgu_in = refs[2 + N_SSD_REFS:2 + N_SSD_REFS + N_SGU_REFS]
        del refs[:2 + N_SSD_REFS + N_SGU_REFS]
    g_ref, wg_ref, wu_ref, wd_ref = refs[:4]
    del refs[:4]
    fg_ref = refs.pop(0) if final_norm else None
    o_ref, wg16_ref, wu16_ref, wd16_ref, xn_ref, h_ref, res_ref = refs[:7]
    if mixed:
        wo16_ref, ys_ref, state_ref, halo_ref, ext_ref, yg_ref = refs[7:]
    step = pl.program_id(0)

    @pl.when(step < N_FF_CHUNKS)
    def _load_weights():
        wg16_ref[step] = wg_ref[...].astype(BF16)
        wu16_ref[step] = wu_ref[...].astype(BF16)
        wd16_ref[pl.ds(pl.multiple_of(step * FF_CHUNK, FF_CHUNK), FF_CHUNK), :] = wd_ref[...].astype(BF16)
        if mixed:
            @pl.when(step < N_WO_CHUNKS)
            def _():
                wo16_ref[pl.ds(pl.multiple_of(step * FF_CHUNK, FF_CHUNK), FF_CHUNK), :] = wo_ref[...].astype(BF16)

            @pl.when(step == 0)
            def _():
                ys_ref[...] = jnp.zeros_like(ys_ref)
                yg_ref[...] = jnp.zeros_like(yg_ref)

    @pl.when(step >= N_FF_CHUNKS)
    def _row_tile():
        x = x_ref[...]
        if mixed:
            a, b = ATT_WIDTH, ATT_WIDTH + SSD_WIDTH
            x = x + (jnp.dot(ya_ref[...].astype(BF16), wo16_ref[0:a, :], preferred_element_type=F32)
                     + jnp.dot(ys_ref[...], wo16_ref[a:b, :], preferred_element_type=F32)
                     + jnp.dot(yg_ref[...], wo16_ref[b:, :], preferred_element_type=F32))
        res_ref[...] = x
        xn_ref[...] = _rmsnorm_f32(x, g_ref[...]).astype(BF16)
        ssd_chunks = iter(())
        if mixed:
            _sgu_tile(*sgu_in, yg_ref)
            fresh = lax.rem(step - N_FF_CHUNKS, tiles_per_seq) == 0
            ssd_chunks = _ssd_tile(fresh, *ssd_in, ys_ref, state_ref, halo_ref, ext_ref)
        for f in range(N_FF_CHUNKS):
            xn = xn_ref[...]
            gate = jnp.dot(xn, wg16_ref[f], preferred_element_type=F32)
            up = jnp.dot(xn, wu16_ref[f], preferred_element_type=F32)
            h_ref[:, f * FF_CHUNK:(f + 1) * FF_CHUNK] = (_silu(gate) * up).astype(BF16)
        pieces = []
        for n in range(N_WO_CHUNKS):
            pieces.append(jnp.dot(h_ref[...], wd16_ref[:, n * FF_CHUNK:(n + 1) * FF_CHUNK],
                                  preferred_element_type=F32))
            next(ssd_chunks, None)
        for _ in ssd_chunks:
            pass
        y = jnp.concatenate(pieces, axis=1)
        out = res_ref[...] + 0.5 * y
        if final_norm:
            out = _rmsnorm_f32(out, fg_ref[...])
        o_ref[...] = out


def _ffn(x, layer, gain, w_gate, w_up, w_down, mix=None, final_gain=None):
    m = x.shape[0]
    n_tiles = m // ROW_TILE
    lag = 0 if mix is None else 1
    tile = lambda i: jnp.clip(i - N_FF_CHUNKS - lag, 0, n_tiles - 1)
    ahead = lambda i: jnp.clip(i - N_FF_CHUNKS, 0, n_tiles - 1)
    chunk = lambda i: jnp.minimum(i, N_FF_CHUNKS - 1)
    row = lambda n: pl.BlockSpec((ROW_TILE, n), lambda i: (tile(i), 0))
    full = lambda r, c: pl.BlockSpec((r, c), lambda i: (0, 0))
    in_specs, args = [row(D_MODEL)], [x]
    scratch = [pltpu.VMEM((N_FF_CHUNKS, D_MODEL, FF_CHUNK), BF16), pltpu.VMEM((N_FF_CHUNKS, D_MODEL, FF_CHUNK), BF16),
               pltpu.VMEM((D_FF, D_MODEL), BF16),
               pltpu.VMEM((ROW_TILE, D_MODEL), BF16), pltpu.VMEM((ROW_TILE, D_FF), BF16),
               pltpu.VMEM((ROW_TILE, D_MODEL), F32)]
    tiles_per_seq = None
    if mix is not None:
        y_att, w_out, tiles_per_seq, ssd_args, sgu_args = mix
        next_row = lambda n: pl.BlockSpec((ROW_TILE, n), lambda i: (ahead(i), 0))
        ssd_arrays, ssd_specs, ssd_scratch = _ssd_operands(*ssd_args, row_spec=next_row, const_spec=full)
        sgu_arrays, sgu_specs, sgu_scratch = _sgu_operands(*sgu_args, row_spec=next_row, const_spec=full)
        in_specs += [row(ATT_WIDTH),
                     pl.BlockSpec((None, FF_CHUNK, D_MODEL), lambda i: (layer, jnp.minimum(i, N_WO_CHUNKS - 1), 0))]
        in_specs += ssd_specs + sgu_specs
        args += [y_att, w_out] + ssd_arrays + sgu_arrays
        scratch += [pltpu.VMEM((D_MODEL, D_MODEL), BF16)] + ssd_scratch + sgu_scratch
    in_specs += [full(1, D_MODEL),
                 pl.BlockSpec((None, D_MODEL, FF_CHUNK), lambda i: (layer, 0, chunk(i))),
                 pl.BlockSpec((None, D_MODEL, FF_CHUNK), lambda i: (layer, 0, chunk(i))),
                 pl.BlockSpec((None, FF_CHUNK, D_MODEL), lambda i: (layer, chunk(i), 0))]
    args += [gain[layer].reshape(1, D_MODEL), w_gate, w_up, w_down]
    if final_gain is not None:
        in_specs.append(full(1, D_MODEL))
        args.append(final_gain.reshape(1, D_MODEL))
    return pl.pallas_call(
        functools.partial(_ffn_kernel, mixed=mix is not None, final_norm=final_gain is not None,
                          tiles_per_seq=tiles_per_seq),
        grid=(N_FF_CHUNKS + n_tiles + lag,),
        in_specs=in_specs,
        out_specs=row(D_MODEL),
        out_shape=jax.ShapeDtypeStruct((m, D_MODEL), F32),
        scratch_shapes=scratch,
        compiler_params=_params(1),
        name="ffn",
    )(*args)


_PROJ_PIECES = (("qkv", QKV_WIDTH, F32),("z", SSD_WIDTH, F32), ("xbc", SSD_CONV_DIM, F32),
                ("uv", UV_WIDTH, F32), ("dt", DT_PAD, F32))


RAW_WIDTH = QKV_WIDTH + SSD_WIDTH + SSD_CONV_DIM
RAW_CHUNK = 256
N_RAW_CHUNKS = RAW_WIDTH // RAW_CHUNK
RAW_REST = RAW_WIDTH - N_RAW_CHUNKS * RAW_CHUNK
TAIL_WIDTH = UV_WIDTH + DT_PAD
N_W_STEPS = N_RAW_CHUNKS + 1
N_PROJ_CHUNKS = PROJ_WIDTH // PROJ_CHUNK


def _inproj_kernel(x_ref, g_ref, w_ref, wrest_ref, wtail_ref, qkv_ref, z_ref, xbc_ref, uv_ref, dt_ref,
                   w16_ref, xn_ref):
    outs = (qkv_ref, z_ref, xbc_ref, uv_ref, dt_ref)
    step = pl.program_id(0)

    for c in range(N_RAW_CHUNKS):
        @pl.when(step == c)
        def _(c=c):
            w16_ref[:, c * RAW_CHUNK:(c + 1) * RAW_CHUNK] = w_ref[...].astype(BF16)

    @pl.when(step == N_RAW_CHUNKS)
    def _():
        w16_ref[:, N_RAW_CHUNKS * RAW_CHUNK:RAW_WIDTH] = wrest_ref[...].astype(BF16)
        w16_ref[:, RAW_WIDTH:] = wtail_ref[...].astype(BF16)

    @pl.when(step >= N_W_STEPS)
    def _row_tile():
        xn_ref[...] = _rmsnorm_f32(x_ref[...], g_ref[...]).astype(BF16)
        starts = np.cumsum([0] + [p[1] for p in _PROJ_PIECES])
        for c in range(N_PROJ_CHUNKS):
            lo, hi = c * PROJ_CHUNK, (c + 1) * PROJ_CHUNK
            r = jnp.dot(xn_ref[...], w16_ref[:, lo:hi], preferred_element_type=F32)
            for k, o_ref in enumerate(outs):
                a, b = max(lo, int(starts[k])), min(hi, int(starts[k + 1]))
                if a < b:
                    o_ref[:, a - int(starts[k]):b - int(starts[k])] = r[:, a - lo:b - lo].astype(o_ref.dtype)


def _inproj(x, layer, gain, w_in):
    m = x.shape[0]
    assert RAW_REST == LANES and RAW_WIDTH % LANES == 0
    dt0 = RAW_WIDTH
    uv0 = RAW_WIDTH + SSD_HEADS
    w_tail = jnp.concatenate([w_in[layer, :, uv0:uv0 + UV_WIDTH],
                              jnp.pad(w_in[layer, :, dt0:uv0], ((0, 0), (0, DT_PAD - SSD_HEADS)))], axis=1)
    row = lambda w: pl.BlockSpec((ROW_TILE, w), lambda i: (jnp.maximum(i - N_W_STEPS, 0), 0))
    return pl.pallas_call(
        _inproj_kernel,
        grid=(N_W_STEPS + m // ROW_TILE,),
        in_specs=[row(D_MODEL), pl.BlockSpec((1, D_MODEL), lambda i: (0, 0)),
                  pl.BlockSpec((None, D_MODEL, RAW_CHUNK), lambda i: (layer, 0, jnp.minimum(i, N_RAW_CHUNKS - 1))),
                  pl.BlockSpec((None, D_MODEL, RAW_REST), lambda i: (layer, 0, RAW_WIDTH // RAW_REST - 1)),
                  pl.BlockSpec((D_MODEL, TAIL_WIDTH), lambda i: (0, 0))],
        out_specs=[row(w) for _, w, _ in _PROJ_PIECES],
        out_shape=[jax.ShapeDtypeStruct((m, w), dt) for _, w, dt in _PROJ_PIECES],
        scratch_shapes=[pltpu.VMEM((D_MODEL, PROJ_WIDTH), BF16), pltpu.VMEM((ROW_TILE, D_MODEL), BF16)],
        compiler_params=_params(1),
        name="inproj",
    )(x, gain[layer].reshape(1, D_MODEL), w_in, w_in, w_tail)


NAT, P4, P16 = 0, 1, 2


def _att_kernel(q_ref, k_ref, v_ref, o_ref, qa_ref, qb_ref, kk_ref, ve_ref,
                acc1_ref, m1_ref, l1_ref, acc3_ref, m3_ref, l3_ref, q4_ref, k4_ref, v4_ref, band_ref, cur_ref):
    seq = q_ref.shape[0]
    T = ATT_BLOCK
    d4, d16 = DILATED_PAIRS[1][1], DILATED_PAIRS[2][1]
    sub4 = seq // d4
    lane = lax.broadcasted_iota(jnp.int32, (1, LANES), 1)
    first = lane < HEAD_DIM
    qi = lax.broadcasted_iota(jnp.int32, (T, T), 0)
    kj = lax.broadcasted_iota(jnp.int32, (T, T), 1)
    cur_bias = jnp.where(kj <= qi, 0.0, -jnp.inf).astype(F32)
    prev_bias = jnp.where(kj >= qi, 0.0, -jnp.inf).astype(F32)
    for half in range(2):
        cur_ref[half * T:(half + 1) * T, :] = cur_bias
        band_ref[half * T:(half + 1) * T, 0:T] = prev_bias
        band_ref[half * T:(half + 1) * T, T:2 * T] = cur_bias
    q_scale = HEAD_DIM ** -0.5 * math.log2(math.e)

    def prep(layout, dst, q, k, v):
        q = q * q_scale
        qa_ref[layout, dst, :] = jnp.where(first, q, 0.0).astype(BF16)
        qb_ref[layout, dst, :] = jnp.where(first, 0.0, q).astype(BF16)
        kk_ref[layout, dst, :] = k.astype(BF16)
        ve_ref[layout, dst, 0:LANES] = v.astype(BF16)
        ve_ref[layout, dst, LANES:2 * LANES] = jnp.ones((T, LANES), BF16)

    def prep_nat_p4(c, carry):
        rows = pl.ds(pl.multiple_of(c * T, T), T)
        prep(NAT, rows, q_ref[rows, :], k_ref[rows, :], v_ref[rows, :])
        src = pl.ds(c // d4 + (c % d4) * (T * d4), T, stride=d4)
        q, k, v = q_ref[src, :], k_ref[src, :], v_ref[src, :]
        q4_ref[rows, :] = q
        k4_ref[rows, :] = k
        v4_ref[rows, :] = v
        prep(P4, rows, q, k, v)
        return carry

    lax.fori_loop(0, seq // T, prep_nat_p4, 0)

    def prep_p16(r16, carry):
        rows = pl.ds(pl.multiple_of(r16 * T, T), T)
        src = pl.ds((r16 % d4) * sub4 + r16 // d4, T, stride=d4)
        prep(P16, rows, q4_ref[src, :], k4_ref[src, :], v4_ref[src, :])
        return carry

    lax.fori_loop(0, d16, prep_p16, 0)

    def block(layout, qrows, krows, bias_ref):
        def scores():
            q2 = jnp.concatenate([qa_ref[layout, qrows, :], qb_ref[layout, qrows, :]], axis=0)
            s = lax.dot_general(q2, kk_ref[layout, krows, :], (((1,), (1,)), ((), ())),
                                preferred_element_type=F32) + bias_ref[...]
            m = jnp.max(s, axis=-1, keepdims=True)
            return jnp.exp2(s - m).astype(BF16), m

        def values(p, m):
            r = jnp.dot(p, ve_ref[layout, krows, :], preferred_element_type=F32)
            acc = jnp.where(first, r[0:T, 0:LANES], r[T:2 * T, 0:LANES])
            lsum = jnp.where(first, r[0:T, LANES:2 * LANES], r[T:2 * T, LANES:2 * LANES])
            return acc, jnp.where(first, m[0:T], m[T:2 * T]), lsum

        return scores, values

    def rows_of(start, n=T):
        return pl.ds(start, n)

    work = []

    def sink1(rows):
        def store(acc, mb, lsum):
            acc1_ref[rows, :] = acc
            m1_ref[rows, :] = mb
            l1_ref[rows, :] = lsum
        return store

    work.append((*block(NAT, rows_of(0), rows_of(0), cur_ref), sink1(rows_of(0))))
    for n in range(1, seq // T):
        work.append((*block(NAT, rows_of(n * T), rows_of((n - 1) * T, 2 * T), band_ref), sink1(rows_of(n * T))))

    def sink3(r16):
        def store(acc, mb, lsum):
            dst = pl.ds((r16 % d4) * sub4 + r16 // d4, T, stride=d4)
            acc3_ref[dst, :] = acc
            m3_ref[dst, :] = mb
            l3_ref[dst, :] = lsum
        return store

    for r16 in range(d16):
        work.append((*block(P16, rows_of(r16 * T), rows_of(r16 * T), cur_ref), sink3(r16)))

    def sink2(r4, n):
        def merge(acc2, mb2, l2):
            prow = rows_of(r4 * sub4 + n * T)
            trow = pl.ds(r4 + n * (T * d4), T, stride=d4)
            acc1, mb1, l1 = acc1_ref[trow, :], m1_ref[trow, :], l1_ref[trow, :]
            acc3, mb3, l3 = acc3_ref[prow, :], m3_ref[prow, :], l3_ref[prow, :]
            m = jnp.maximum(mb1, jnp.maximum(mb2, mb3))
            w1, w2, w3 = jnp.exp2(mb1 - m), jnp.exp2(mb2 - m), jnp.exp2(mb3 - m)
            num = w1 * acc1 + w2 * acc2 + w3 * acc3
            den = w1 * l1 + w2 * l2 + w3 * l3
            o_ref[trow, :] = num / den
        return merge

    for r4 in range(d4):
        work.append((*block(P4, rows_of(r4 * sub4), rows_of(r4 * sub4), cur_ref), sink2(r4, 0)))
        for n in range(1, sub4 // T):
            work.append((*block(P4, rows_of(r4 * sub4 + n * T), rows_of(r4 * sub4 + (n - 1) * T, 2 * T), band_ref),
                         sink2(r4, n)))

    staged = []
    for scores, values, sink in work:
        staged.append((values, sink, scores()))
        if len(staged) > ATT_PIPE:
            values0, sink0, pm = staged.pop(0)
            sink0(*values0(*pm))
    for values0, sink0, pm in staged:
        sink0(*values0(*pm))


def _attention(qkv):
    b, s, _ = qkv.shape
    for window, dil in DILATED_PAIRS:
        assert window // dil == ATT_BLOCK and s % (ATT_BLOCK * dil) == 0
    assert DILATED_PAIRS[0][1] == 1 and DILATED_PAIRS[2][1] == DILATED_PAIRS[1][1] ** 2
    n_pairs = ATT_WIDTH // LANES
    spec = lambda part: pl.BlockSpec((None, s, LANES), lambda bi, hp: (bi, 0, part * n_pairs + hp))
    return pl.pallas_call(
        _att_kernel,
        grid=(b, n_pairs),
        in_specs=[spec(0), spec(1), spec(2)],
        out_specs=pl.BlockSpec((None, s, LANES), lambda bi, hp: (bi, 0, hp)),
        out_shape=jax.ShapeDtypeStruct((b, s, ATT_WIDTH), F32),
        scratch_shapes=[pltpu.VMEM((3, s, LANES), BF16)] * 3 + [pltpu.VMEM((3, s, 2 * LANES), BF16)]
        + [pltpu.VMEM((s, LANES), F32)] * 9
        + [pltpu.VMEM((2 * ATT_BLOCK, 2 * ATT_BLOCK), F32), pltpu.VMEM((2 * ATT_BLOCK, ATT_BLOCK), F32)],
        compiler_params=_params(2),
        name="dilated_attention",
    )(qkv, qkv, qkv)


SSD_HALO = 8
HEADS_PER_GROUP = SSD_HEADS // SSD_GROUPS
GROUP_LANES = HEADS_PER_GROUP * SSD_HEADDIM


def _ssd_tile(fresh, z_ref, xbc_ref, dt_ref, cw_ref, cb_ref, dtb_ref, alog_ref, dsk_ref, ng_ref,
              o_ref, state_ref, halo_ref, ext_ref):
    n_chunks = z_ref.shape[0] // SSD_CHUNK
    L = SSD_CHUNK
    row = lax.broadcasted_iota(jnp.int32, (L, L), 0)
    col = lax.broadcasted_iota(jnp.int32, (L, L), 1)
    tril = row >= col
    cumsum_mat = tril.astype(F32)
    lane = lax.broadcasted_iota(jnp.int32, (1, LANES), 1)
    lane_w = lax.broadcasted_iota(jnp.int32, (1, SSD_WIDTH), 1)
    first_group = lane_w < GROUP_LANES
    first_head = lane < SSD_HEADDIM
    a_neg = -jnp.exp(alog_ref[...])
    n_b = SSD_GROUPS * SSD_STATE

    state = jnp.where(fresh, 0.0, state_ref[...])
    for c in range(n_chunks):
        rows = slice(c * L, (c + 1) * L)
        if c == 0:
            halo = jnp.where(fresh, 0.0, halo_ref[...])
        else:
            halo = xbc_ref[c * L - SSD_HALO:c * L, :]
        ext_ref[c, 0:SSD_HALO, :] = halo
        ext_ref[c, SSD_HALO:, :] = xbc_ref[rows, :]
        conv = cb_ref[...]
        for w in range(SSD_CONV):
            o = SSD_HALO - (SSD_CONV - 1) + w
            conv = conv + cw_ref[w:w + 1, :] * ext_ref[c, o:o + L, :]
        xact = _silu(conv)
        xs = xact[:, :SSD_WIDTH]
        bm = [xact[:, SSD_WIDTH + g * SSD_STATE:SSD_WIDTH + (g + 1) * SSD_STATE] for g in range(SSD_GROUPS)]
        cm = [xact[:, SSD_WIDTH + n_b + g * SSD_STATE:SSD_WIDTH + n_b + (g + 1) * SSD_STATE]
              for g in range(SSD_GROUPS)]
        bmt16 = [t.T.astype(BF16) for t in bm]
        cm16 = [t.astype(BF16) for t in cm]

        dt = jax.nn.softplus(dt_ref[rows, :] + dtb_ref[...])
        a = dt * a_neg
        acs = jnp.dot(cumsum_mat, a, precision=lax.Precision.HIGHEST, preferred_element_type=F32)
        acs_t = acs.T
        dt_t = dt.T
        acs_last = acs[L - 1:L, :]
        exp_acs_h = jnp.exp(acs)
        to_end_h = jnp.exp(acs_last - acs) * dt
        chunk_decay_h = jnp.exp(acs_last)
        cb = [jnp.dot(cm16[g], bmt16[g], preferred_element_type=F32) for g in range(SSD_GROUPS)]

        y_diag, e_pairs, w_pairs, d_pairs = [], [], [], []
        for p in range(SSD_HEADS // 2):
            xs_pair = xs[:, p * LANES:(p + 1) * LANES].astype(BF16)
            yd, ecol, wcol, dcol = [], [], [], []
            for h in (2 * p, 2 * p + 1):
                g = h // HEADS_PER_GROUP
                acs_col = jnp.broadcast_to(acs[:, h:h + 1], (L, L))
                seg = acs_col - acs_t[h:h + 1, :]
                decay = jnp.exp(jnp.where(tril, seg, -jnp.inf))
                mix = (cb[g] * decay * dt_t[h:h + 1, :]).astype(BF16)
                yd.append(jnp.dot(mix, xs_pair, preferred_element_type=F32))
                ecol.append(jnp.broadcast_to(exp_acs_h[:, h:h + 1], (L, LANES)))
                wcol.append(jnp.broadcast_to(to_end_h[:, h:h + 1], (L, LANES)))
                dcol.append(jnp.broadcast_to(chunk_decay_h[:, h:h + 1], (1, LANES)))
            y_diag.append(jnp.where(first_head, yd[0], yd[1]))
            e_pairs.append(jnp.where(first_head, ecol[0], ecol[1]))
            w_pairs.append(jnp.where(first_head, wcol[0], wcol[1]))
            d_pairs.append(jnp.where(first_head, dcol[0], dcol[1]))
        y_diag = jnp.concatenate(y_diag, axis=1)
        exp_acs = jnp.concatenate(e_pairs, axis=1)
        to_end = jnp.concatenate(w_pairs, axis=1)
        chunk_decay = jnp.concatenate(d_pairs, axis=1)

        st16 = state.astype(BF16)
        y_off = jnp.where(first_group,
                          jnp.dot(cm16[0], st16, preferred_element_type=F32),
                          jnp.dot(cm16[1], st16, preferred_element_type=F32)) * exp_acs
        xdd = (xs * to_end).astype(BF16)
        new = jnp.where(first_group,
                        jnp.dot(bmt16[0], xdd, preferred_element_type=F32),
                        jnp.dot(bmt16[1], xdd, preferred_element_type=F32))
        state = state * chunk_decay + new

        y = y_diag + y_off + dsk_ref[...] * xs
        y = y * _silu(z_ref[rows, :])
        ysq = y * y
        s0 = jnp.sum(jnp.where(first_group, ysq, 0.0), axis=-1, keepdims=True)
        s1 = jnp.sum(jnp.where(first_group, 0.0, ysq), axis=-1, keepdims=True)
        ms = jnp.where(first_group, s0, s1) * (1.0 / GROUP_LANES)
        o_ref[rows, :] = (y * lax.rsqrt(ms + RMS_EPS) * ng_ref[...]).astype(o_ref.dtype)
        if c == n_chunks - 1:
            state_ref[...] = state
            halo_ref[...] = xbc_ref[n_chunks * L - SSD_HALO:n_chunks * L, :]
        yield


def _ssd_operands(z, xbc, dt, conv_w, conv_b, dt_bias, a_log, d_skip, norm_g, row_spec, const_spec):
    pad = lambda v: jnp.pad(v, (0, DT_PAD - SSD_HEADS)).reshape(1, DT_PAD)
    arrays = [z, xbc, dt, conv_w, conv_b.reshape(1, SSD_CONV_DIM), pad(dt_bias), pad(a_log),
              jnp.repeat(d_skip, SSD_HEADDIM).reshape(1, SSD_WIDTH), norm_g.reshape(1, SSD_WIDTH)]
    specs = [row_spec(SSD_WIDTH), row_spec(SSD_CONV_DIM), row_spec(DT_PAD),
             const_spec(SSD_CONV, SSD_CONV_DIM), const_spec(1, SSD_CONV_DIM), const_spec(1, DT_PAD),
             const_spec(1, DT_PAD), const_spec(1, SSD_WIDTH), const_spec(1, SSD_WIDTH)]
    scratch = [pltpu.VMEM((ROW_TILE, SSD_WIDTH), BF16),
               pltpu.VMEM((SSD_STATE, SSD_WIDTH), F32),
               pltpu.VMEM((SSD_HALO, SSD_CONV_DIM), F32),
               pltpu.VMEM((ROW_TILE // SSD_CHUNK, SSD_HALO + SSD_CHUNK, SSD_CONV_DIM), F32)]
    return arrays, specs, scratch


def _sgu_tile(uv_ref, lng_ref, lnb_ref, w_ref, bs_ref, o_ref):
    uv = uv_ref[...]
    act = 0.5 * uv * (1.0 + lax.erf(uv * (1.0 / math.sqrt(2.0))))
    u = act[:, :SGU_WIDTH]
    v = act[:, SGU_WIDTH:]
    mu = jnp.mean(v, axis=-1, keepdims=True)
    var = jnp.mean(jnp.square(v - mu), axis=-1, keepdims=True)
    vn = (v - mu) * lax.rsqrt(var + LN_EPS) * lng_ref[...] + lnb_ref[...]
    row = lax.broadcasted_iota(jnp.int32, (SGU_CHUNK, SGU_CHUNK), 0)
    col = lax.broadcasted_iota(jnp.int32, (SGU_CHUNK, SGU_CHUNK), 1)
    w = [jnp.where(row >= col, w_ref[g], 0.0).astype(BF16) for g in range(SGU_GROUPS)]
    lane = lax.broadcasted_iota(jnp.int32, (1, LANES), 1)
    first = lane < SGU_GROUP_DIM
    for c in range(uv_ref.shape[0] // SGU_CHUNK):
        rows = slice(c * SGU_CHUNK, (c + 1) * SGU_CHUNK)
        mixed = []
        for p in range(SGU_WIDTH // LANES):
            vp = vn[rows, p * LANES:(p + 1) * LANES]
            lo = jnp.where(first, vp, 0.0).astype(BF16)
            hi = jnp.where(first, 0.0, vp).astype(BF16)
            mixed.append(jnp.dot(w[2 * p], lo, preferred_element_type=F32)
                         + jnp.dot(w[2 * p + 1], hi, preferred_element_type=F32))
        mixed = jnp.concatenate(mixed, axis=1) + bs_ref[...]
        o_ref[rows, :] = (u[rows, :] * mixed).astype(o_ref.dtype)


def _sgu_operands(uv, ln_g, ln_b, w_s, b_s, row_spec, const_spec):
    bias = jnp.repeat(b_s.T, SGU_GROUP_DIM, axis=1)
    arrays = [uv, ln_g.reshape(1, SGU_WIDTH), ln_b.reshape(1, SGU_WIDTH), w_s, bias]
    specs = [row_spec(UV_WIDTH), const_spec(1, SGU_WIDTH), const_spec(1, SGU_WIDTH),
             pl.BlockSpec((SGU_GROUPS, SGU_CHUNK, SGU_CHUNK), lambda i: (0, 0, 0)),
             const_spec(SGU_CHUNK, SGU_WIDTH)]
    scratch = [pltpu.VMEM((ROW_TILE, SGU_WIDTH), BF16)]
    return arrays, specs, scratch


def _mixers(x, b, s, layer, gain, w_in):
    qkv, z, xbc, uv, dt = _inproj(x, layer, gain, w_in)
    y_att = _attention(qkv.reshape(b, s, QKV_WIDTH))
    return y_att.reshape(b * s, ATT_WIDTH), (z, xbc, dt), uv


def kernel(x, ffn1_norm, ffn1_w_gate, ffn1_w_up, ffn1_w_down, mix_norm, w_in, conv_w, conv_b, dt_bias, a_log, d_skip, ssd_norm, sgu_ln_g, sgu_ln_b, sgu_w, sgu_b, w_out, ffn2_norm, ffn2_w_gate, ffn2_w_up, ffn2_w_down, final_norm):
    b, s, d = x.shape
    depth = ffn1_norm.shape[0]
    h = x.reshape(b * s, d)
    for i in range(depth):
        h = _ffn(h, i, ffn1_norm, ffn1_w_gate, ffn1_w_up, ffn1_w_down)
        y_att, ssd_proj, uv = _mixers(h, b, s, i, mix_norm, w_in)
        ssd_args = (*ssd_proj, conv_w[i], conv_b[i], dt_bias[i], a_log[i], d_skip[i], ssd_norm[i])
        sgu_args = (uv, sgu_ln_g[i], sgu_ln_b[i], sgu_w[i], sgu_b[i])
        h = _ffn(h, i, ffn2_norm, ffn2_w_gate, ffn2_w_up, ffn2_w_down,
                 mix=(y_att, w_out, s // ROW_TILE, ssd_args, sgu_args),
                 final_gain=final_norm if i == depth - 1 else None)
    return h.reshape(b, s, d)
```

```python
import functools
import math

import numpy as np
import jax
import jax.numpy as jnp
from jax import lax
from jax.experimental import pallas as pl
from jax.experimental.pallas import tpu as pltpu

F32 = jnp.float32
BF16 = jnp.bfloat16

D_MODEL = 1024
D_FF = 2816
HEAD_DIM = 64
ATT_HEADS = 6
ATT_WIDTH = ATT_HEADS * HEAD_DIM
DILATED_PAIRS = ((128, 1), (512, 4), (2048, 16))
SSD_HEADS = 6
SSD_HEADDIM = 64
SSD_WIDTH = SSD_HEADS * SSD_HEADDIM
SSD_GROUPS = 2
SSD_STATE = 128
SSD_CONV = 4
SSD_CHUNK = 128
SSD_CONV_DIM = SSD_WIDTH + 2 * SSD_GROUPS * SSD_STATE
SGU_GROUPS = 4
SGU_GROUP_DIM = 64
SGU_WIDTH = SGU_GROUPS * SGU_GROUP_DIM
SGU_CHUNK = 128
RMS_EPS = 1e-6
LN_EPS = 1e-5

LANES = 128
DT_PAD = LANES
QKV_WIDTH = 3 * ATT_WIDTH
UV_WIDTH = 2 * SGU_WIDTH
PROJ_WIDTH = QKV_WIDTH + SSD_WIDTH + SSD_CONV_DIM + UV_WIDTH + DT_PAD

VMEM_LIMIT = 56 * 1024 * 1024

ROW_TILE = 512
FF_CHUNK = 256
PROJ_CHUNK = 512

ATT_BLOCK = 128
ATT_PIPE = 4


def _params(n_axes):
    return pltpu.CompilerParams(dimension_semantics=("arbitrary",) * n_axes,
                                vmem_limit_bytes=VMEM_LIMIT)


def _rmsnorm_f32(x, g):
    ms = jnp.mean(x * x, axis=-1, keepdims=True)
    return x * lax.rsqrt(ms + RMS_EPS) * g


def _silu(x):
    return x * jax.nn.sigmoid(x)


N_FF_CHUNKS = D_FF // FF_CHUNK
N_WO_CHUNKS = D_MODEL // FF_CHUNK
N_SSD_REFS = 9
N_SGU_REFS = 5


def _ffn_kernel(*refs, mixed, final_norm, tiles_per_seq):
    refs = list(refs)
    x_ref = refs.pop(0)
    if mixed:
        ya_ref, wo_ref = refs[:2]
        ssd_in = refs[2:2 + N_SSD_REFS]
        sgu_in = refs[2 + N_SSD_REFS:2 + N_SSD_REFS + N_SGU_REFS]
        del refs[:2 + N_SSD_REFS + N_SGU_REFS]
    g_ref, wg_ref, wu_ref, wd_ref = refs[:4]
    del refs[:4]
    fg_ref = refs.pop(0) if final_norm else None
    o_ref, wg16_ref, wu16_ref, wd16_ref, xn_ref, h_ref, res_ref = refs[:7]
    if mixed:
        wo16_ref, ys_ref, state_ref, halo_ref, ext_ref, yg_ref = refs[7:]
    step = pl.program_id(0)

    @pl.when(step < N_FF_CHUNKS)
    def _load_weights():
        wg16_ref[step] = wg_ref[...].astype(BF16)
        wu16_ref[step] = wu_ref[...].astype(BF16)
        wd16_ref[pl.ds(pl.multiple_of(step * FF_CHUNK, FF_CHUNK), FF_CHUNK), :] = wd_ref[...].astype(BF16)
        if mixed:
            @pl.when(step < N_WO_CHUNKS)
            def _():
                wo16_ref[pl.ds(pl.multiple_of(step * FF_CHUNK, FF_CHUNK), FF_CHUNK), :] = wo_ref[...].astype(BF16)

            @pl.when(step == 0)
            def _():
                ys_ref[...] = jnp.zeros_like(ys_ref)
                yg_ref[...] = jnp.zeros_like(yg_ref)

    @pl.when(step >= N_FF_CHUNKS)
    def _row_tile():
        x = x_ref[...]
        if mixed:
            a, b = ATT_WIDTH, ATT_WIDTH + SSD_WIDTH
            x = x + (jnp.dot(ya_ref[...].astype(BF16), wo16_ref[0:a, :], preferred_element_type=F32)
                     + jnp.dot(ys_ref[...], wo16_ref[a:b, :], preferred_element_type=F32)
                     + jnp.dot(yg_ref[...], wo16_ref[b:, :], preferred_element_type=F32))
        res_ref[...] = x
        xn_ref[...] = _rmsnorm_f32(x, g_ref[...]).astype(BF16)
        ssd_chunks = iter(())
        if mixed:
            _sgu_tile(*sgu_in, yg_ref)
            fresh = lax.rem(step - N_FF_CHUNKS, tiles_per_seq) == 0
            ssd_chunks = _ssd_tile(fresh, *ssd_in, ys_ref, state_ref, halo_ref, ext_ref)
        for f in range(N_FF_CHUNKS):
            xn = xn_ref[...]
            gate = jnp.dot(xn, wg16_ref[f], preferred_element_type=F32)
            up = jnp.dot(xn, wu16_ref[f], preferred_element_type=F32)
            h_ref[:, f * FF_CHUNK:(f + 1) * FF_CHUNK] = (_silu(gate) * up).astype(BF16)
        pieces = []
        for n in range(N_WO_CHUNKS):
            pieces.append(jnp.dot(h_ref[...], wd16_ref[:, n * FF_CHUNK:(n + 1) * FF_CHUNK],
                                  preferred_element_type=F32))
            next(ssd_chunks, None)
        for _ in ssd_chunks:
            pass
        y = jnp.concatenate(pieces, axis=1)
        out = res_ref[...] + 0.5 * y
        if final_norm:
            out = _rmsnorm_f32(out, fg_ref[...])
        o_ref[...] = out


def _ffn(x, layer, gain, w_gate, w_up, w_down, mix=None, final_gain=None):
    m = x.shape[0]
    n_tiles = m // ROW_TILE
    lag = 0 if mix is None else 1
    tile = lambda i: jnp.clip(i - N_FF_CHUNKS - lag, 0, n_tiles - 1)
    ahead = lambda i: jnp.clip(i - N_FF_CHUNKS, 0, n_tiles - 1)
    chunk = lambda i: jnp.minimum(i, N_FF_CHUNKS - 1)
    row = lambda n: pl.BlockSpec((ROW_TILE, n), lambda i: (tile(i), 0))
    full = lambda r, c: pl.BlockSpec((r, c), lambda i: (0, 0))
    in_specs, args = [row(D_MODEL)], [x]
    scratch = [pltpu.VMEM((N_FF_CHUNKS, D_MODEL, FF_CHUNK), BF16), pltpu.VMEM((N_FF_CHUNKS, D_MODEL, FF_CHUNK), BF16),
               pltpu.VMEM((D_FF, D_MODEL), BF16),
               pltpu.VMEM((ROW_TILE, D_MODEL), BF16), pltpu.VMEM((ROW_TILE, D_FF), BF16),
               pltpu.VMEM((ROW_TILE, D_MODEL), F32)]
    tiles_per_seq = None
    if mix is not None:
        y_att, w_out, tiles_per_seq, ssd_args, sgu_args = mix
        next_row = lambda n: pl.BlockSpec((ROW_TILE, n), lambda i: (ahead(i), 0))
        ssd_arrays, ssd_specs, ssd_scratch = _ssd_operands(*ssd_args, row_spec=next_row, const_spec=full)
        sgu_arrays, sgu_specs, sgu_scratch = _sgu_operands(*sgu_args, row_spec=next_row, const_spec=full)
        in_specs += [row(ATT_WIDTH),
                     pl.BlockSpec((None, FF_CHUNK, D_MODEL), lambda i: (layer, jnp.minimum(i, N_WO_CHUNKS - 1), 0))]
        in_specs += ssd_specs + sgu_specs
        args += [y_att, w_out] + ssd_arrays + sgu_arrays
        scratch += [pltpu.VMEM((D_MODEL, D_MODEL), BF16)] + ssd_scratch + sgu_scratch
    in_specs += [full(1, D_MODEL),
                 pl.BlockSpec((None, D_MODEL, FF_CHUNK), lambda i: (layer, 0, chunk(i))),
                 pl.BlockSpec((None, D_MODEL, FF_CHUNK), lambda i: (layer, 0, chunk(i))),
                 pl.BlockSpec((None, FF_CHUNK, D_MODEL), lambda i: (layer, chunk(i), 0))]
    args += [gain[layer].reshape(1, D_MODEL), w_gate, w_up, w_down]
    if final_gain is not None:
        in_specs.append(full(1, D_MODEL))
        args.append(final_gain.reshape(1, D_MODEL))
    return pl.pallas_call(
        functools.partial(_ffn_kernel, mixed=mix is not None, final_norm=final_gain is not None,
                          tiles_per_seq=tiles_per_seq),
        grid=(N_FF_CHUNKS + n_tiles + lag,),
        in_specs=in_specs,
        out_specs=row(D_MODEL),
        out_shape=jax.ShapeDtypeStruct((m, D_MODEL), F32),
        scratch_shapes=scratch,
        compiler_params=_params(1),
        name="ffn",
    )(*args)


_PROJ_PIECES = (("qkv", QKV_WIDTH, F32),("z", SSD_WIDTH, F32), ("xbc", SSD_CONV_DIM, F32),
                ("uv", UV_WIDTH, F32), ("dt", DT_PAD, F32))


RAW_WIDTH = QKV_WIDTH + SSD_WIDTH + SSD_CONV_DIM
RAW_CHUNK = 256
N_RAW_CHUNKS = RAW_WIDTH // RAW_CHUNK
RAW_REST = RAW_WIDTH - N_RAW_CHUNKS * RAW_CHUNK
TAIL_WIDTH = UV_WIDTH + DT_PAD
N_W_STEPS = N_RAW_CHUNKS + 1
N_PROJ_CHUNKS = PROJ_WIDTH // PROJ_CHUNK


def _inproj_kernel(x_ref, g_ref, w_ref, wrest_ref, wtail_ref, qkv_ref, z_ref, xbc_ref, uv_ref, dt_ref,
                   w16_ref, xn_ref):
    outs = (qkv_ref, z_ref, xbc_ref, uv_ref, dt_ref)
    step = pl.program_id(0)

    for c in range(N_RAW_CHUNKS):
        @pl.when(step == c)
        def _(c=c):
            w16_ref[:, c * RAW_CHUNK:(c + 1) * RAW_CHUNK] = w_ref[...].astype(BF16)

    @pl.when(step == N_RAW_CHUNKS)
    def _():
        w16_ref[:, N_RAW_CHUNKS * RAW_CHUNK:RAW_WIDTH] = wrest_ref[...].astype(BF16)
        w16_ref[:, RAW_WIDTH:] = wtail_ref[...].astype(BF16)

    @pl.when(step >= N_W_STEPS)
    def _row_tile():
        xn_ref[...] = _rmsnorm_f32(x_ref[...], g_ref[...]).astype(BF16)
        starts = np.cumsum([0] + [p[1] for p in _PROJ_PIECES])
        for c in range(N_PROJ_CHUNKS):
            lo, hi = c * PROJ_CHUNK, (c + 1) * PROJ_CHUNK
            r = jnp.dot(xn_ref[...], w16_ref[:, lo:hi], preferred_element_type=F32)
            for k, o_ref in enumerate(outs):
                a, b = max(lo, int(starts[k])), min(hi, int(starts[k + 1]))
                if a < b:
                    o_ref[:, a - int(starts[k]):b - int(starts[k])] = r[:, a - lo:b - lo].astype(o_ref.dtype)


def _inproj(x, layer, gain, w_in):
    m = x.shape[0]
    assert RAW_REST == LANES and RAW_WIDTH % LANES == 0
    dt0 = RAW_WIDTH
    uv0 = RAW_WIDTH + SSD_HEADS
    w_tail = jnp.concatenate([w_in[layer, :, uv0:uv0 + UV_WIDTH],
                              jnp.pad(w_in[layer, :, dt0:uv0], ((0, 0), (0, DT_PAD - SSD_HEADS)))], axis=1)
    row = lambda w: pl.BlockSpec((ROW_TILE, w), lambda i: (jnp.maximum(i - N_W_STEPS, 0), 0))
    return pl.pallas_call(
        _inproj_kernel,
        grid=(N_W_STEPS + m // ROW_TILE,),
        in_specs=[row(D_MODEL), pl.BlockSpec((1, D_MODEL), lambda i: (0, 0)),
                  pl.BlockSpec((None, D_MODEL, RAW_CHUNK), lambda i: (layer, 0, jnp.minimum(i, N_RAW_CHUNKS - 1))),
                  pl.BlockSpec((None, D_MODEL, RAW_REST), lambda i: (layer, 0, RAW_WIDTH // RAW_REST - 1)),
                  pl.BlockSpec((D_MODEL, TAIL_WIDTH), lambda i: (0, 0))],
        out_specs=[row(w) for _, w, _ in _PROJ_PIECES],
        out_shape=[jax.ShapeDtypeStruct((m, w), dt) for _, w, dt in _PROJ_PIECES],
        scratch_shapes=[pltpu.VMEM((D_MODEL, PROJ_WIDTH), BF16), pltpu.VMEM((ROW_TILE, D_MODEL), BF16)],
        compiler_params=_params(1),
        name="inproj",
    )(x, gain[layer].reshape(1, D_MODEL), w_in, w_in, w_tail)


NAT, P4, P16 = 0, 1, 2


def _att_kernel(q_ref, k_ref, v_ref, o_ref, qa_ref, qb_ref, kk_ref, ve_ref,
                acc1_ref, m1_ref, l1_ref, acc3_ref, m3_ref, l3_ref, q4_ref, k4_ref, v4_ref, band_ref, cur_ref):
    seq = q_ref.shape[0]
    T = ATT_BLOCK
    d4, d16 = DILATED_PAIRS[1][1], DILATED_PAIRS[2][1]
    sub4 = seq // d4
    lane = lax.broadcasted_iota(jnp.int32, (1, LANES), 1)
    first = lane < HEAD_DIM
    qi = lax.broadcasted_iota(jnp.int32, (T, T), 0)
    kj = lax.broadcasted_iota(jnp.int32, (T, T), 1)
    cur_bias = jnp.where(kj <= qi, 0.0, -jnp.inf).astype(F32)
    prev_bias = jnp.where(kj >= qi, 0.0, -jnp.inf).astype(F32)
    for half in range(2):
        cur_ref[half * T:(half + 1) * T, :] = cur_bias
        band_ref[half * T:(half + 1) * T, 0:T] = prev_bias
        band_ref[half * T:(half + 1) * T, T:2 * T] = cur_bias
    q_scale = HEAD_DIM ** -0.5 * math.log2(math.e)

    def prep(layout, dst, q, k, v):
        q = q * q_scale
        qa_ref[layout, dst, :] = jnp.where(first, q, 0.0).astype(BF16)
        qb_ref[layout, dst, :] = jnp.where(first, 0.0, q).astype(BF16)
        kk_ref[layout, dst, :] = k.astype(BF16)
        ve_ref[layout, dst, 0:LANES] = v.astype(BF16)
        ve_ref[layout, dst, LANES:2 * LANES] = jnp.ones((T, LANES), BF16)

    for c in range(seq // T):
        rows = pl.ds(c * T, T)
        prep(NAT, rows, q_ref[rows, :], k_ref[rows, :], v_ref[rows, :])
        src = pl.ds(c // d4 + (c % d4) * (T * d4), T, stride=d4)
        q, k, v = q_ref[src, :], k_ref[src, :], v_ref[src, :]
        q4_ref[rows, :] = q
        k4_ref[rows, :] = k
        v4_ref[rows, :] = v
        prep(P4, rows, q, k, v)

    for r16 in range(d16):
        rows = pl.ds(r16 * T, T)
        src = pl.ds((r16 % d4) * sub4 + r16 // d4, T, stride=d4)
        prep(P16, rows, q4_ref[src, :], k4_ref[src, :], v4_ref[src, :])

    def block(layout, qrows, krows, bias_ref):
        def scores():
            q2 = jnp.concatenate([qa_ref[layout, qrows, :], qb_ref[layout, qrows, :]], axis=0)
            s = lax.dot_general(q2, kk_ref[layout, krows, :], (((1,), (1,)), ((), ())),
                                preferred_element_type=F32) + bias_ref[...]
            m = jnp.max(s, axis=-1, keepdims=True)
            return jnp.exp2(s - m).astype(BF16), m

        def values(p, m):
            r = jnp.dot(p, ve_ref[layout, krows, :], preferred_element_type=F32)
            acc = jnp.where(first, r[0:T, 0:LANES], r[T:2 * T, 0:LANES])
            lsum = jnp.where(first, r[0:T, LANES:2 * LANES], r[T:2 * T, LANES:2 * LANES])
            return acc, jnp.where(first, m[0:T], m[T:2 * T]), lsum

        return scores, values

    def rows_of(start, n=T):
        return pl.ds(start, n)

    work = []

    def sink1(rows):
        def store(acc, mb, lsum):
            acc1_ref[rows, :] = acc
            m1_ref[rows, :] = mb
            l1_ref[rows, :] = lsum
        return store

    work.append((*block(NAT, rows_of(0), rows_of(0), cur_ref), sink1(rows_of(0))))
    for n in range(1, seq // T):
        work.append((*block(NAT, rows_of(n * T), rows_of((n - 1) * T, 2 * T), band_ref), sink1(rows_of(n * T))))

    def sink3(r16):
        def store(acc, mb, lsum):
            dst = pl.ds((r16 % d4) * sub4 + r16 // d4, T, stride=d4)
            acc3_ref[dst, :] = acc
            m3_ref[dst, :] = mb
            l3_ref[dst, :] = lsum
        return store

    for r16 in range(d16):
        work.append((*block(P16, rows_of(r16 * T), rows_of(r16 * T), cur_ref), sink3(r16)))

    def sink2(r4, n):
        def merge(acc2, mb2, l2):
            prow = rows_of(r4 * sub4 + n * T)
            trow = pl.ds(r4 + n * (T * d4), T, stride=d4)
            acc1, mb1, l1 = acc1_ref[trow, :], m1_ref[trow, :], l1_ref[trow, :]
            acc3, mb3, l3 = acc3_ref[prow, :], m3_ref[prow, :], l3_ref[prow, :]
            m = jnp.maximum(mb1, jnp.maximum(mb2, mb3))
            w1, w2, w3 = jnp.exp2(mb1 - m), jnp.exp2(mb2 - m), jnp.exp2(mb3 - m)
            num = w1 * acc1 + w2 * acc2 + w3 * acc3
            den = w1 * l1 + w2 * l2 + w3 * l3
            o_ref[trow, :] = num / den
        return merge

    for r4 in range(d4):
        work.append((*block(P4, rows_of(r4 * sub4), rows_of(r4 * sub4), cur_ref), sink2(r4, 0)))
        for n in range(1, sub4 // T):
            work.append((*block(P4, rows_of(r4 * sub4 + n * T), rows_of(r4 * sub4 + (n - 1) * T, 2 * T), band_ref),
                         sink2(r4, n)))

    staged = []
    for scores, values, sink in work:
        staged.append((values, sink, scores()))
        if len(staged) > ATT_PIPE:
            values0, sink0, pm = staged.pop(0)
            sink0(*values0(*pm))
    for values0, sink0, pm in staged:
        sink0(*values0(*pm))


def _attention(qkv):
    b, s, _ = qkv.shape
    for window, dil in DILATED_PAIRS:
        assert window // dil == ATT_BLOCK and s % (ATT_BLOCK * dil) == 0
    assert DILATED_PAIRS[0][1] == 1 and DILATED_PAIRS[2][1] == DILATED_PAIRS[1][1] ** 2
    n_pairs = ATT_WIDTH // LANES
    spec = lambda part: pl.BlockSpec((None, s, LANES), lambda bi, hp: (bi, 0, part * n_pairs + hp))
    return pl.pallas_call(
        _att_kernel,
        grid=(b, n_pairs),
        in_specs=[spec(0), spec(1), spec(2)],
        out_specs=pl.BlockSpec((None, s, LANES), lambda bi, hp: (bi, 0, hp)),
        out_shape=jax.ShapeDtypeStruct((b, s, ATT_WIDTH), F32),
        scratch_shapes=[pltpu.VMEM((3, s, LANES), BF16)] * 3 + [pltpu.VMEM((3, s, 2 * LANES), BF16)]
        + [pltpu.VMEM((s, LANES), F32)] * 9
        + [pltpu.VMEM((2 * ATT_BLOCK, 2 * ATT_BLOCK), F32), pltpu.VMEM((2 * ATT_BLOCK, ATT_BLOCK), F32)],
        compiler_params=_params(2),
        name="dilated_attention",
    )(qkv, qkv, qkv)


SSD_HALO = 8
HEADS_PER_GROUP = SSD_HEADS // SSD_GROUPS
GROUP_LANES = HEADS_PER_GROUP * SSD_HEADDIM


def _ssd_tile(fresh, z_ref, xbc_ref, dt_ref, cw_ref, cb_ref, dtb_ref, alog_ref, dsk_ref, ng_ref,
              o_ref, state_ref, halo_ref, ext_ref):
    n_chunks = z_ref.shape[0] // SSD_CHUNK
    L = SSD_CHUNK
    row = lax.broadcasted_iota(jnp.int32, (L, L), 0)
    col = lax.broadcasted_iota(jnp.int32, (L, L), 1)
    tril = row >= col
    cumsum_mat = tril.astype(F32)
    lane = lax.broadcasted_iota(jnp.int32, (1, LANES), 1)
    lane_w = lax.broadcasted_iota(jnp.int32, (1, SSD_WIDTH), 1)
    first_group = lane_w < GROUP_LANES
    first_head = lane < SSD_HEADDIM
    a_neg = -jnp.exp(alog_ref[...])
    n_b = SSD_GROUPS * SSD_STATE

    state = jnp.where(fresh, 0.0, state_ref[...])
    for c in range(n_chunks):
        rows = slice(c * L, (c + 1) * L)
        if c == 0:
            halo = jnp.where(fresh, 0.0, halo_ref[...])
        else:
            halo = xbc_ref[c * L - SSD_HALO:c * L, :]
        ext_ref[c, 0:SSD_HALO, :] = halo
        ext_ref[c, SSD_HALO:, :] = xbc_ref[rows, :]
        conv = cb_ref[...]
        for w in range(SSD_CONV):
            o = SSD_HALO - (SSD_CONV - 1) + w
            conv = conv + cw_ref[w:w + 1, :] * ext_ref[c, o:o + L, :]
        xact = _silu(conv)
        xs = xact[:, :SSD_WIDTH]
        bm = [xact[:, SSD_WIDTH + g * SSD_STATE:SSD_WIDTH + (g + 1) * SSD_STATE] for g in range(SSD_GROUPS)]
        cm = [xact[:, SSD_WIDTH + n_b + g * SSD_STATE:SSD_WIDTH + n_b + (g + 1) * SSD_STATE]
              for g in range(SSD_GROUPS)]
        bmt16 = [t.T.astype(BF16) for t in bm]
        cm16 = [t.astype(BF16) for t in cm]

        dt = jax.nn.softplus(dt_ref[rows, :] + dtb_ref[...])
        a = dt * a_neg
        acs = jnp.dot(cumsum_mat, a, precision=lax.Precision.HIGHEST, preferred_element_type=F32)
        acs_t = acs.T
        dt_t = dt.T
        acs_last = acs[L - 1:L, :]
        exp_acs_h = jnp.exp(acs)
        to_end_h = jnp.exp(acs_last - acs) * dt
        chunk_decay_h = jnp.exp(acs_last)
        cb = [jnp.dot(cm16[g], bmt16[g], preferred_element_type=F32) for g in range(SSD_GROUPS)]

        y_diag, e_pairs, w_pairs, d_pairs = [], [], [], []
        for p in range(SSD_HEADS // 2):
            xs_pair = xs[:, p * LANES:(p + 1) * LANES].astype(BF16)
            yd, ecol, wcol, dcol = [], [], [], []
            for h in (2 * p, 2 * p + 1):
                g = h // HEADS_PER_GROUP
                acs_col = jnp.broadcast_to(acs[:, h:h + 1], (L, L))
                seg = acs_col - acs_t[h:h + 1, :]
                decay = jnp.exp(jnp.where(tril, seg, -jnp.inf))
                mix = (cb[g] * decay * dt_t[h:h + 1, :]).astype(BF16)
                yd.append(jnp.dot(mix, xs_pair, preferred_element_type=F32))
                ecol.append(jnp.broadcast_to(exp_acs_h[:, h:h + 1], (L, LANES)))
                wcol.append(jnp.broadcast_to(to_end_h[:, h:h + 1], (L, LANES)))
                dcol.append(jnp.broadcast_to(chunk_decay_h[:, h:h + 1], (1, LANES)))
            y_diag.append(jnp.where(first_head, yd[0], yd[1]))
            e_pairs.append(jnp.where(first_head, ecol[0], ecol[1]))
            w_pairs.append(jnp.where(first_head, wcol[0], wcol[1]))
            d_pairs.append(jnp.where(first_head, dcol[0], dcol[1]))
        y_diag = jnp.concatenate(y_diag, axis=1)
        exp_acs = jnp.concatenate(e_pairs, axis=1)
        to_end = jnp.concatenate(w_pairs, axis=1)
        chunk_decay = jnp.concatenate(d_pairs, axis=1)

        st16 = state.astype(BF16)
        y_off = jnp.where(first_group,
                          jnp.dot(cm16[0], st16, preferred_element_type=F32),
                          jnp.dot(cm16[1], st16, preferred_element_type=F32)) * exp_acs
        xdd = (xs * to_end).astype(BF16)
        new = jnp.where(first_group,
                        jnp.dot(bmt16[0], xdd, preferred_element_type=F32),
                        jnp.dot(bmt16[1], xdd, preferred_element_type=F32))
        state = state * chunk_decay + new

        y = y_diag + y_off + dsk_ref[...] * xs
        y = y * _silu(z_ref[rows, :])
        ysq = y * y
        s0 = jnp.sum(jnp.where(first_group, ysq, 0.0), axis=-1, keepdims=True)
        s1 = jnp.sum(jnp.where(first_group, 0.0, ysq), axis=-1, keepdims=True)
        ms = jnp.where(first_group, s0, s1) * (1.0 / GROUP_LANES)
        o_ref[rows, :] = (y * lax.rsqrt(ms + RMS_EPS) * ng_ref[...]).astype(o_ref.dtype)
        if c == n_chunks - 1:
            state_ref[...] = state
            halo_ref[...] = xbc_ref[n_chunks * L - SSD_HALO:n_chunks * L, :]
        yield


def _ssd_operands(z, xbc, dt, conv_w, conv_b, dt_bias, a_log, d_skip, norm_g, row_spec, const_spec):
    pad = lambda v: jnp.pad(v, (0, DT_PAD - SSD_HEADS)).reshape(1, DT_PAD)
    arrays = [z, xbc, dt, conv_w, conv_b.reshape(1, SSD_CONV_DIM), pad(dt_bias), pad(a_log),
              jnp.repeat(d_skip, SSD_HEADDIM).reshape(1, SSD_WIDTH), norm_g.reshape(1, SSD_WIDTH)]
    specs = [row_spec(SSD_WIDTH), row_spec(SSD_CONV_DIM), row_spec(DT_PAD),
             const_spec(SSD_CONV, SSD_CONV_DIM), const_spec(1, SSD_CONV_DIM), const_spec(1, DT_PAD),
             const_spec(1, DT_PAD), const_spec(1, SSD_WIDTH), const_spec(1, SSD_WIDTH)]
    scratch = [pltpu.VMEM((ROW_TILE, SSD_WIDTH), BF16),
               pltpu.VMEM((SSD_STATE, SSD_WIDTH), F32),
               pltpu.VMEM((SSD_HALO, SSD_CONV_DIM), F32),
               pltpu.VMEM((ROW_TILE // SSD_CHUNK, SSD_HALO + SSD_CHUNK, SSD_CONV_DIM), F32)]
    return arrays, specs, scratch


def _sgu_tile(uv_ref, lng_ref, lnb_ref, w_ref, bs_ref, o_ref):
    uv = uv_ref[...]
    act = 0.5 * uv * (1.0 + lax.erf(uv * (1.0 / math.sqrt(2.0))))
    u = act[:, :SGU_WIDTH]
    v = act[:, SGU_WIDTH:]
    mu = jnp.mean(v, axis=-1, keepdims=True)
    var = jnp.mean(jnp.square(v - mu), axis=-1, keepdims=True)
    vn = (v - mu) * lax.rsqrt(var + LN_EPS) * lng_ref[...] + lnb_ref[...]
    row = lax.broadcasted_iota(jnp.int32, (SGU_CHUNK, SGU_CHUNK), 0)
    col = lax.broadcasted_iota(jnp.int32, (SGU_CHUNK, SGU_CHUNK), 1)
    w = [jnp.where(row >= col, w_ref[g], 0.0).astype(BF16) for g in range(SGU_GROUPS)]
    lane = lax.broadcasted_iota(jnp.int32, (1, LANES), 1)
    first = lane < SGU_GROUP_DIM
    for c in range(uv_ref.shape[0] // SGU_CHUNK):
        rows = slice(c * SGU_CHUNK, (c + 1) * SGU_CHUNK)
        mixed = []
        for p in range(SGU_WIDTH // LANES):
            vp = vn[rows, p * LANES:(p + 1) * LANES]
            lo = jnp.where(first, vp, 0.0).astype(BF16)
            hi = jnp.where(first, 0.0, vp).astype(BF16)
            mixed.append(jnp.dot(w[2 * p], lo, preferred_element_type=F32)
                         + jnp.dot(w[2 * p + 1], hi, preferred_element_type=F32))
        mixed = jnp.concatenate(mixed, axis=1) + bs_ref[...]
        o_ref[rows, :] = (u[rows, :] * mixed).astype(o_ref.dtype)


def _sgu_operands(uv, ln_g, ln_b, w_s, b_s, row_spec, const_spec):
    bias = jnp.repeat(b_s.T, SGU_GROUP_DIM, axis=1)
    arrays = [uv, ln_g.reshape(1, SGU_WIDTH), ln_b.reshape(1, SGU_WIDTH), w_s, bias]
    specs = [row_spec(UV_WIDTH), const_spec(1, SGU_WIDTH), const_spec(1, SGU_WIDTH),
             pl.BlockSpec((SGU_GROUPS, SGU_CHUNK, SGU_CHUNK), lambda i: (0, 0, 0)),
             const_spec(SGU_CHUNK, SGU_WIDTH)]
    scratch = [pltpu.VMEM((ROW_TILE, SGU_WIDTH), BF16)]
    return arrays, specs, scratch


def _mixers(x, b, s, layer, gain, w_in):
    qkv, z, xbc, uv, dt = _inproj(x, layer, gain, w_in)
    y_att = _attention(qkv.reshape(b, s, QKV_WIDTH))
    return y_att.reshape(b * s, ATT_WIDTH), (z, xbc, dt), uv


def kernel(x, ffn1_norm, ffn1_w_gate, ffn1_w_up, ffn1_w_down, mix_norm, w_in, conv_w, conv_b, dt_bias, a_log, d_skip, ssd_norm, sgu_ln_g, sgu_ln_b, sgu_w, sgu_b, w_out, ffn2_norm, ffn2_w_gate, ffn2_w_up, ffn2_w_down, final_norm):
    b, s, d = x.shape
    depth = ffn1_norm.shape[0]
    h = x.reshape(b * s, d)
    for i in range(depth):
        h = _ffn(h, i, ffn1_norm, ffn1_w_gate, ffn1_w_up, ffn1_w_down)
        y_att, ssd_proj, uv = _mixers(h, b, s, i, mix_norm, w_in)
        ssd_args = (*ssd_proj, conv_w[i], conv_b[i], dt_bias[i], a_log[i], d_skip[i], ssd_norm[i])
        sgu_args = (uv, sgu_ln_g[i], sgu_ln_b[i], sgu_w[i], sgu_b[i])
        h = _ffn(h, i, ffn2_norm, ffn2_w_gate, ffn2_w_up, ffn2_w_down,
                 mix=(y_att, w_out, s // ROW_TILE, ssd_args, sgu_args),
                 final_gain=final_norm if i == depth - 1 else None)
    return h.reshape(b, s, d)
```

```python
import functools
import math

import numpy as np
import jax
import jax.numpy as jnp
from jax import lax
from jax.experimental import pallas as pl
from jax.experimental.pallas import tpu as pltpu

F32 = jnp.float32
BF16 = jnp.bfloat16

D_MODEL = 1024
D_FF = 2816
HEAD_DIM = 64
ATT_HEADS = 6
ATT_WIDTH = ATT_HEADS * HEAD_DIM
DILATED_PAIRS = ((128, 1), (512, 4), (2048, 16))
SSD_HEADS = 6
SSD_HEADDIM = 64
SSD_WIDTH = SSD_HEADS * SSD_HEADDIM
SSD_GROUPS = 2
SSD_STATE = 128
SSD_CONV = 4
SSD_CHUNK = 128
SSD_CONV_DIM = SSD_WIDTH + 2 * SSD_GROUPS * SSD_STATE
SGU_GROUPS = 4
SGU_GROUP_DIM = 64
SGU_WIDTH = SGU_GROUPS * SGU_GROUP_DIM
SGU_CHUNK = 128
RMS_EPS = 1e-6
LN_EPS = 1e-5

LANES = 128
DT_PAD = LANES
QKV_WIDTH = 3 * ATT_WIDTH
UV_WIDTH = 2 * SGU_WIDTH
PROJ_WIDTH = QKV_WIDTH + SSD_WIDTH + SSD_CONV_DIM + UV_WIDTH + DT_PAD

VMEM_LIMIT = 56 * 1024 * 1024

ROW_TILE = 512
FF_CHUNK = 256
PROJ_CHUNK = 512

ATT_BLOCK = 128
ATT_PIPE = 4


def _params(n_axes):
    return pltpu.CompilerParams(dimension_semantics=("arbitrary",) * n_axes,
                                vmem_limit_bytes=VMEM_LIMIT)


def _rmsnorm_f32(x, g):
    ms = jnp.mean(x * x, axis=-1, keepdims=True)
    return x * lax.rsqrt(ms + RMS_EPS) * g


def _silu(x):
    return x * jax.nn.sigmoid(x)


N_FF_CHUNKS = D_FF // FF_CHUNK
N_WO_CHUNKS = D_MODEL // FF_CHUNK
N_SSD_REFS = 9
N_SGU_REFS = 5


def _ffn_kernel(*refs, mixed, final_norm, next_norm, tiles_per_seq):
    refs = list(refs)
    x_ref = refs.pop(0)
    if mixed:
        ya_ref, wo_ref = refs[:2]
        ssd_in = refs[2:2 + N_SSD_REFS]
        sgu_in = refs[2 + N_SSD_REFS:2 + N_SSD_REFS + N_SGU_REFS]
        del refs[:2 + N_SSD_REFS + N_SGU_REFS]
    g_ref, wg_ref, wu_ref, wd_ref = refs[:4]
    del refs[:4]
    fg_ref = refs.pop(0) if final_norm else None
    ng_ref = refs.pop(0) if next_norm else None
    o_ref = refs.pop(0)
    on_ref = refs.pop(0) if next_norm else None
    wg16_ref, wu16_ref, wd16_ref, xn_ref, h_ref, res_ref = refs[:6]
    if mixed:
        wo16_ref, ys_ref, state_ref, halo_ref, ext_ref, yg_ref = refs[6:]
    step = pl.program_id(0)

    @pl.when(step < N_FF_CHUNKS)
    def _load_weights():
        wg16_ref[step] = wg_ref[...].astype(BF16)
        wu16_ref[step] = wu_ref[...].astype(BF16)
        wd16_ref[pl.ds(pl.multiple_of(step * FF_CHUNK, FF_CHUNK), FF_CHUNK), :] = wd_ref[...].astype(BF16)
        if mixed:
            @pl.when(step < N_WO_CHUNKS)
            def _():
                wo16_ref[pl.ds(pl.multiple_of(step * FF_CHUNK, FF_CHUNK), FF_CHUNK), :] = wo_ref[...].astype(BF16)

            @pl.when(step == 0)
            def _():
                ys_ref[...] = jnp.zeros_like(ys_ref)
                yg_ref[...] = jnp.zeros_like(yg_ref)

    @pl.when(step >= N_FF_CHUNKS)
    def _row_tile():
        x = x_ref[...]
        if mixed:
            a, b = ATT_WIDTH, ATT_WIDTH + SSD_WIDTH
            x = x + (jnp.dot(ya_ref[...].astype(BF16), wo16_ref[0:a, :], preferred_element_type=F32)
                     + jnp.dot(ys_ref[...], wo16_ref[a:b, :], preferred_element_type=F32)
                     + jnp.dot(yg_ref[...], wo16_ref[b:, :], preferred_element_type=F32))
        res_ref[...] = x
        xn_ref[...] = _rmsnorm_f32(x, g_ref[...]).astype(BF16)
        ssd_chunks = iter(())
        if mixed:
            _sgu_tile(*sgu_in, yg_ref)
            fresh = lax.rem(step - N_FF_CHUNKS, tiles_per_seq) == 0
            ssd_chunks = _ssd_tile(fresh, *ssd_in, ys_ref, state_ref, halo_ref, ext_ref)
        for f in range(N_FF_CHUNKS):
            xn = xn_ref[...]
            gate = jnp.dot(xn, wg16_ref[f], preferred_element_type=F32)
            up = jnp.dot(xn, wu16_ref[f], preferred_element_type=F32)
            h_ref[:, f * FF_CHUNK:(f + 1) * FF_CHUNK] = (_silu(gate) * up).astype(BF16)
        pieces = []
        for n in range(N_WO_CHUNKS):
            pieces.append(jnp.dot(h_ref[...], wd16_ref[:, n * FF_CHUNK:(n + 1) * FF_CHUNK],
                                  preferred_element_type=F32))
            next(ssd_chunks, None)
        for _ in ssd_chunks:
            pass
        y = jnp.concatenate(pieces, axis=1)
        out = res_ref[...] + 0.5 * y
        if final_norm:
            out = _rmsnorm_f32(out, fg_ref[...])
        o_ref[...] = out
        if next_norm:
            on_ref[...] = _rmsnorm_f32(out, ng_ref[...]).astype(BF16)


def _ffn(x, layer, gain, w_gate, w_up, w_down, mix=None, final_gain=None, next_gain=None):
    m = x.shape[0]
    n_tiles = m // ROW_TILE
    lag = 0 if mix is None else 1
    tile = lambda i: jnp.clip(i - N_FF_CHUNKS - lag, 0, n_tiles - 1)
    ahead = lambda i: jnp.clip(i - N_FF_CHUNKS, 0, n_tiles - 1)
    chunk = lambda i: jnp.minimum(i, N_FF_CHUNKS - 1)
    row = lambda n: pl.BlockSpec((ROW_TILE, n), lambda i: (tile(i), 0))
    full = lambda r, c: pl.BlockSpec((r, c), lambda i: (0, 0))
    in_specs, args = [row(D_MODEL)], [x]
    scratch = [pltpu.VMEM((N_FF_CHUNKS, D_MODEL, FF_CHUNK), BF16), pltpu.VMEM((N_FF_CHUNKS, D_MODEL, FF_CHUNK), BF16),
               pltpu.VMEM((D_FF, D_MODEL), BF16),
               pltpu.VMEM((ROW_TILE, D_MODEL), BF16), pltpu.VMEM((ROW_TILE, D_FF), BF16),
               pltpu.VMEM((ROW_TILE, D_MODEL), F32)]
    tiles_per_seq = None
    if mix is not None:
        y_att, w_out, tiles_per_seq, ssd_args, sgu_args = mix
        next_row = lambda n: pl.BlockSpec((ROW_TILE, n), lambda i: (ahead(i), 0))
        ssd_arrays, ssd_specs, ssd_scratch = _ssd_operands(*ssd_args, row_spec=next_row, const_spec=full)
        sgu_arrays, sgu_specs, sgu_scratch = _sgu_operands(*sgu_args, row_spec=next_row, const_spec=full)
        in_specs += [row(ATT_WIDTH),
                     pl.BlockSpec((None, FF_CHUNK, D_MODEL), lambda i: (layer, jnp.minimum(i, N_WO_CHUNKS - 1), 0))]
        in_specs += ssd_specs + sgu_specs
        args += [y_att, w_out] + ssd_arrays + sgu_arrays
        scratch += [pltpu.VMEM((D_MODEL, D_MODEL), BF16)] + ssd_scratch + sgu_scratch
    in_specs += [full(1, D_MODEL),
                 pl.BlockSpec((None, D_MODEL, FF_CHUNK), lambda i: (layer, 0, chunk(i))),
                 pl.BlockSpec((None, D_MODEL, FF_CHUNK), lambda i: (layer, 0, chunk(i))),
                 pl.BlockSpec((None, FF_CHUNK, D_MODEL), lambda i: (layer, chunk(i), 0))]
    args += [gain[layer].reshape(1, D_MODEL), w_gate, w_up, w_down]
    out_specs, out_shape = [row(D_MODEL)], [jax.ShapeDtypeStruct((m, D_MODEL), F32)]
    for extra_gain in (final_gain, next_gain):
        if extra_gain is not None:
            in_specs.append(full(1, D_MODEL))
            args.append(extra_gain.reshape(1, D_MODEL))
    if next_gain is not None:
        out_specs.append(row(D_MODEL))
        out_shape.append(jax.ShapeDtypeStruct((m, D_MODEL), BF16))
    outs = pl.pallas_call(
        functools.partial(_ffn_kernel, mixed=mix is not None, final_norm=final_gain is not None,
                          next_norm=next_gain is not None, tiles_per_seq=tiles_per_seq),
        grid=(N_FF_CHUNKS + n_tiles + lag,),
        in_specs=in_specs,
        out_specs=out_specs,
        out_shape=out_shape,
        scratch_shapes=scratch,
        compiler_params=_params(1),
        name="ffn",
    )(*args)
    return outs if next_gain is not None else outs[0]


_PROJ_PIECES = (("qkv", QKV_WIDTH, F32),("z", SSD_WIDTH, F32), ("xbc", SSD_CONV_DIM, F32),
                ("uv", UV_WIDTH, F32), ("dt", DT_PAD, F32))


RAW_WIDTH = QKV_WIDTH + SSD_WIDTH + SSD_CONV_DIM
RAW_CHUNK = 256
N_RAW_CHUNKS = RAW_WIDTH // RAW_CHUNK
RAW_REST = RAW_WIDTH - N_RAW_CHUNKS * RAW_CHUNK
TAIL_WIDTH = UV_WIDTH + DT_PAD
N_W_STEPS = N_RAW_CHUNKS + 1
N_PROJ_CHUNKS = PROJ_WIDTH // PROJ_CHUNK


def _inproj_kernel(xn_ref, w_ref, wrest_ref, wtail_ref, qkv_ref, z_ref, xbc_ref, uv_ref, dt_ref, w16_ref):
    outs = (qkv_ref, z_ref, xbc_ref, uv_ref, dt_ref)
    step = pl.program_id(0)

    for c in range(N_RAW_CHUNKS):
        @pl.when(step == c)
        def _(c=c):
            w16_ref[:, c * RAW_CHUNK:(c + 1) * RAW_CHUNK] = w_ref[...].astype(BF16)

    @pl.when(step == N_RAW_CHUNKS)
    def _():
        w16_ref[:, N_RAW_CHUNKS * RAW_CHUNK:RAW_WIDTH] = wrest_ref[...].astype(BF16)
        w16_ref[:, RAW_WIDTH:] = wtail_ref[...].astype(BF16)

    @pl.when(step >= N_W_STEPS)
    def _row_tile():
        starts = np.cumsum([0] + [p[1] for p in _PROJ_PIECES])
        for c in range(N_PROJ_CHUNKS):
            lo, hi = c * PROJ_CHUNK, (c + 1) * PROJ_CHUNK
            r = jnp.dot(xn_ref[...], w16_ref[:, lo:hi], preferred_element_type=F32)
            for k, o_ref in enumerate(outs):
                a, b = max(lo, int(starts[k])), min(hi, int(starts[k + 1]))
                if a < b:
                    o_ref[:, a - int(starts[k]):b - int(starts[k])] = r[:, a - lo:b - lo].astype(o_ref.dtype)


def _inproj(xn, w_in):
    m = xn.shape[0]
    assert RAW_REST == LANES and RAW_WIDTH % LANES == 0
    dt0 = RAW_WIDTH
    uv0 = RAW_WIDTH + SSD_HEADS
    w_main = w_in[:, :RAW_WIDTH]
    w_tail = jnp.concatenate([w_in[:, uv0:uv0 + UV_WIDTH],
                              jnp.pad(w_in[:, dt0:uv0], ((0, 0), (0, DT_PAD - SSD_HEADS)))], axis=1)
    row = lambda w: pl.BlockSpec((ROW_TILE, w), lambda i: (jnp.maximum(i - N_W_STEPS, 0), 0))
    return pl.pallas_call(
        _inproj_kernel,
        grid=(N_W_STEPS + m // ROW_TILE,),
        in_specs=[row(D_MODEL),
                  pl.BlockSpec((D_MODEL, RAW_CHUNK), lambda i: (0, jnp.minimum(i, N_RAW_CHUNKS - 1))),
                  pl.BlockSpec((D_MODEL, RAW_REST), lambda i: (0, RAW_WIDTH // RAW_REST - 1)),
                  pl.BlockSpec((D_MODEL, TAIL_WIDTH), lambda i: (0, 0))],
        out_specs=[row(w) for _, w, _ in _PROJ_PIECES],
        out_shape=[jax.ShapeDtypeStruct((m, w), dt) for _, w, dt in _PROJ_PIECES],
        scratch_shapes=[pltpu.VMEM((D_MODEL, PROJ_WIDTH), BF16)],
        compiler_params=_params(1),
        name="inproj",
    )(xn, w_main, w_main, w_tail)


NAT, P4, P16 = 0, 1, 2


def _att_kernel(q_ref, k_ref, v_ref, o_ref, qa_ref, qb_ref, kk_ref, ve_ref,
                acc1_ref, m1_ref, l1_ref, acc3_ref, m3_ref, l3_ref, q4_ref, k4_ref, v4_ref, band_ref, cur_ref):
    seq = q_ref.shape[0]
    T = ATT_BLOCK
    d4, d16 = DILATED_PAIRS[1][1], DILATED_PAIRS[2][1]
    sub4 = seq // d4
    lane = lax.broadcasted_iota(jnp.int32, (1, LANES), 1)
    first = lane < HEAD_DIM
    qi = lax.broadcasted_iota(jnp.int32, (T, T), 0)
    kj = lax.broadcasted_iota(jnp.int32, (T, T), 1)
    cur_bias = jnp.where(kj <= qi, 0.0, -jnp.inf).astype(F32)
    prev_bias = jnp.where(kj >= qi, 0.0, -jnp.inf).astype(F32)
    for half in range(2):
        cur_ref[half * T:(half + 1) * T, :] = cur_bias
        band_ref[half * T:(half + 1) * T, 0:T] = prev_bias
        band_ref[half * T:(half + 1) * T, T:2 * T] = cur_bias
    q_scale = HEAD_DIM ** -0.5 * math.log2(math.e)

    def prep(layout, dst, q, k, v):
        q = q * q_scale
        qa_ref[layout, dst, :] = jnp.where(first, q, 0.0).astype(BF16)
        qb_ref[layout, dst, :] = jnp.where(first, 0.0, q).astype(BF16)
        kk_ref[layout, dst, :] = k.astype(BF16)
        ve_ref[layout, dst, 0:LANES] = v.astype(BF16)
        ve_ref[layout, dst, LANES:2 * LANES] = jnp.ones((T, LANES), BF16)

    for c in range(seq // T):
        rows = pl.ds(c * T, T)
        prep(NAT, rows, q_ref[rows, :], k_ref[rows, :], v_ref[rows, :])
        src = pl.ds(c // d4 + (c % d4) * (T * d4), T, stride=d4)
        q, k, v = q_ref[src, :], k_ref[src, :], v_ref[src, :]
        q4_ref[rows, :] = q
        k4_ref[rows, :] = k
        v4_ref[rows, :] = v
        prep(P4, rows, q, k, v)

    for r16 in range(d16):
        rows = pl.ds(r16 * T, T)
        src = pl.ds((r16 % d4) * sub4 + r16 // d4, T, stride=d4)
        prep(P16, rows, q4_ref[src, :], k4_ref[src, :], v4_ref[src, :])

    def block(layout, qrows, krows, bias_ref):
        def scores():
            q2 = jnp.concatenate([qa_ref[layout, qrows, :], qb_ref[layout, qrows, :]], axis=0)
            s = lax.dot_general(q2, kk_ref[layout, krows, :], (((1,), (1,)), ((), ())),
                                preferred_element_type=F32) + bias_ref[...]
            m = jnp.max(s, axis=-1, keepdims=True)
            return jnp.exp2(s - m).astype(BF16), m

        def values(p, m):
            r = jnp.dot(p, ve_ref[layout, krows, :], preferred_element_type=F32)
            acc = jnp.where(first, r[0:T, 0:LANES], r[T:2 * T, 0:LANES])
            lsum = jnp.where(first, r[0:T, LANES:2 * LANES], r[T:2 * T, LANES:2 * LANES])
            return acc, jnp.where(first, m[0:T], m[T:2 * T]), lsum

        return scores, values

    def rows_of(start, n=T):
        return pl.ds(start, n)

    work = []

    def sink1(rows):
        def store(acc, mb, lsum):
            acc1_ref[rows, :] = acc
            m1_ref[rows, :] = mb
            l1_ref[rows, :] = lsum
        return store

    work.append((*block(NAT, rows_of(0), rows_of(0), cur_ref), sink1(rows_of(0))))
    for n in range(1, seq // T):
        work.append((*block(NAT, rows_of(n * T), rows_of((n - 1) * T, 2 * T), band_ref), sink1(rows_of(n * T))))

    def sink3(r16):
        def store(acc, mb, lsum):
            dst = pl.ds((r16 % d4) * sub4 + r16 // d4, T, stride=d4)
            acc3_ref[dst, :] = acc
            m3_ref[dst, :] = mb
            l3_ref[dst, :] = lsum
        return store

    for r16 in range(d16):
        work.append((*block(P16, rows_of(r16 * T), rows_of(r16 * T), cur_ref), sink3(r16)))

    def sink2(r4, n):
        def merge(acc2, mb2, l2):
            prow = rows_of(r4 * sub4 + n * T)
            trow = pl.ds(r4 + n * (T * d4), T, stride=d4)
            acc1, mb1, l1 = acc1_ref[trow, :], m1_ref[trow, :], l1_ref[trow, :]
            acc3, mb3, l3 = acc3_ref[prow, :], m3_ref[prow, :], l3_ref[prow, :]
            m = jnp.maximum(mb1, jnp.maximum(mb2, mb3))
            w1, w2, w3 = jnp.exp2(mb1 - m), jnp.exp2(mb2 - m), jnp.exp2(mb3 - m)
            num = w1 * acc1 + w2 * acc2 + w3 * acc3
            den = w1 * l1 + w2 * l2 + w3 * l3
            o_ref[trow, :] = num / den
        return merge

    for r4 in range(d4):
        work.append((*block(P4, rows_of(r4 * sub4), rows_of(r4 * sub4), cur_ref), sink2(r4, 0)))
        for n in range(1, sub4 // T):
            work.append((*block(P4, rows_of(r4 * sub4 + n * T), rows_of(r4 * sub4 + (n - 1) * T, 2 * T), band_ref),
                         sink2(r4, n)))

    staged = []
    for scores, values, sink in work:
        staged.append((values, sink, scores()))
        if len(staged) > ATT_PIPE:
            values0, sink0, pm = staged.pop(0)
            sink0(*values0(*pm))
    for values0, sink0, pm in staged:
        sink0(*values0(*pm))


def _attention(qkv):
    b, s, _ = qkv.shape
    for window, dil in DILATED_PAIRS:
        assert window // dil == ATT_BLOCK and s % (ATT_BLOCK * dil) == 0
    assert DILATED_PAIRS[0][1] == 1 and DILATED_PAIRS[2][1] == DILATED_PAIRS[1][1] ** 2
    n_pairs = ATT_WIDTH // LANES
    spec = lambda part: pl.BlockSpec((None, s, LANES), lambda bi, hp: (bi, 0, part * n_pairs + hp))
    return pl.pallas_call(
        _att_kernel,
        grid=(b, n_pairs),
        in_specs=[spec(0), spec(1), spec(2)],
        out_specs=pl.BlockSpec((None, s, LANES), lambda bi, hp: (bi, 0, hp)),
        out_shape=jax.ShapeDtypeStruct((b, s, ATT_WIDTH), F32),
        scratch_shapes=[pltpu.VMEM((3, s, LANES), BF16)] * 3 + [pltpu.VMEM((3, s, 2 * LANES), BF16)]
        + [pltpu.VMEM((s, LANES), F32)] * 9
        + [pltpu.VMEM((2 * ATT_BLOCK, 2 * ATT_BLOCK), F32), pltpu.VMEM((2 * ATT_BLOCK, ATT_BLOCK), F32)],
        compiler_params=_params(2),
        name="dilated_attention",
    )(qkv, qkv, qkv)


SSD_HALO = 8
HEADS_PER_GROUP = SSD_HEADS // SSD_GROUPS
GROUP_LANES = HEADS_PER_GROUP * SSD_HEADDIM


def _ssd_tile(fresh, z_ref, xbc_ref, dt_ref, cw_ref, cb_ref, dtb_ref, alog_ref, dsk_ref, ng_ref,
              o_ref, state_ref, halo_ref, ext_ref):
    n_chunks = z_ref.shape[0] // SSD_CHUNK
    L = SSD_CHUNK
    row = lax.broadcasted_iota(jnp.int32, (L, L), 0)
    col = lax.broadcasted_iota(jnp.int32, (L, L), 1)
    tril = row >= col
    cumsum_mat = tril.astype(F32)
    lane = lax.broadcasted_iota(jnp.int32, (1, LANES), 1)
    lane_w = lax.broadcasted_iota(jnp.int32, (1, SSD_WIDTH), 1)
    first_group = lane_w < GROUP_LANES
    first_head = lane < SSD_HEADDIM
    a_neg = -jnp.exp(alog_ref[...])
    n_b = SSD_GROUPS * SSD_STATE

    state = jnp.where(fresh, 0.0, state_ref[...])
    for c in range(n_chunks):
        rows = slice(c * L, (c + 1) * L)
        if c == 0:
            halo = jnp.where(fresh, 0.0, halo_ref[...])
        else:
            halo = xbc_ref[c * L - SSD_HALO:c * L, :]
        ext_ref[c, 0:SSD_HALO, :] = halo
        ext_ref[c, SSD_HALO:, :] = xbc_ref[rows, :]
        conv = cb_ref[...]
        for w in range(SSD_CONV):
            o = SSD_HALO - (SSD_CONV - 1) + w
            conv = conv + cw_ref[w:w + 1, :] * ext_ref[c, o:o + L, :]
        xact = _silu(conv)
        xs = xact[:, :SSD_WIDTH]
        bm = [xact[:, SSD_WIDTH + g * SSD_STATE:SSD_WIDTH + (g + 1) * SSD_STATE] for g in range(SSD_GROUPS)]
        cm = [xact[:, SSD_WIDTH + n_b + g * SSD_STATE:SSD_WIDTH + n_b + (g + 1) * SSD_STATE]
              for g in range(SSD_GROUPS)]
        bmt16 = [t.T.astype(BF16) for t in bm]
        cm16 = [t.astype(BF16) for t in cm]

        dt = jax.nn.softplus(dt_ref[rows, :] + dtb_ref[...])
        a = dt * a_neg
        acs = jnp.dot(cumsum_mat, a, precision=lax.Precision.HIGHEST, preferred_element_type=F32)
        acs_t = acs.T
        dt_t = dt.T
        acs_last = acs[L - 1:L, :]
        exp_acs_h = jnp.exp(acs)
        to_end_h = jnp.exp(acs_last - acs) * dt
        chunk_decay_h = jnp.exp(acs_last)
        cb = [jnp.dot(cm16[g], bmt16[g], preferred_element_type=F32) for g in range(SSD_GROUPS)]

        y_diag, e_pairs, w_pairs, d_pairs = [], [], [], []
        for p in range(SSD_HEADS // 2):
            xs_pair = xs[:, p * LANES:(p + 1) * LANES].astype(BF16)
            yd, ecol, wcol, dcol = [], [], [], []
            for h in (2 * p, 2 * p + 1):
                g = h // HEADS_PER_GROUP
                acs_col = jnp.broadcast_to(acs[:, h:h + 1], (L, L))
                seg = acs_col - acs_t[h:h + 1, :]
                decay = jnp.exp(jnp.where(tril, seg, -jnp.inf))
                mix = (cb[g] * decay * dt_t[h:h + 1, :]).astype(BF16)
                yd.append(jnp.dot(mix, xs_pair, preferred_element_type=F32))
                ecol.append(jnp.broadcast_to(exp_acs_h[:, h:h + 1], (L, LANES)))
                wcol.append(jnp.broadcast_to(to_end_h[:, h:h + 1], (L, LANES)))
                dcol.append(jnp.broadcast_to(chunk_decay_h[:, h:h + 1], (1, LANES)))
            y_diag.append(jnp.where(first_head, yd[0], yd[1]))
            e_pairs.append(jnp.where(first_head, ecol[0], ecol[1]))
            w_pairs.append(jnp.where(first_head, wcol[0], wcol[1]))
            d_pairs.append(jnp.where(first_head, dcol[0], dcol[1]))
        y_diag = jnp.concatenate(y_diag, axis=1)
        exp_acs = jnp.concatenate(e_pairs, axis=1)
        to_end = jnp.concatenate(w_pairs, axis=1)
        chunk_decay = jnp.concatenate(d_pairs, axis=1)

        st16 = state.astype(BF16)
        y_off = jnp.where(first_group,
                          jnp.dot(cm16[0], st16, preferred_element_type=F32),
                          jnp.dot(cm16[1], st16, preferred_element_type=F32)) * exp_acs
        xdd = (xs * to_end).astype(BF16)
        new = jnp.where(first_group,
                        jnp.dot(bmt16[0], xdd, preferred_element_type=F32),
                        jnp.dot(bmt16[1], xdd, preferred_element_type=F32))
        state = state * chunk_decay + new

        y = y_diag + y_off + dsk_ref[...] * xs
        y = y * _silu(z_ref[rows, :])
        ysq = y * y
        s0 = jnp.sum(jnp.where(first_group, ysq, 0.0), axis=-1, keepdims=True)
        s1 = jnp.sum(jnp.where(first_group, 0.0, ysq), axis=-1, keepdims=True)
        ms = jnp.where(first_group, s0, s1) * (1.0 / GROUP_LANES)
        o_ref[rows, :] = (y * lax.rsqrt(ms + RMS_EPS) * ng_ref[...]).astype(o_ref.dtype)
        if c == n_chunks - 1:
            state_ref[...] = state
            halo_ref[...] = xbc_ref[n_chunks * L - SSD_HALO:n_chunks * L, :]
        yield


def _ssd_operands(z, xbc, dt, conv_w, conv_b, dt_bias, a_log, d_skip, norm_g, row_spec, const_spec):
    pad = lambda v: jnp.pad(v, (0, DT_PAD - SSD_HEADS)).reshape(1, DT_PAD)
    arrays = [z, xbc, dt, conv_w, conv_b.reshape(1, SSD_CONV_DIM), pad(dt_bias), pad(a_log),
              jnp.repeat(d_skip, SSD_HEADDIM).reshape(1, SSD_WIDTH), norm_g.reshape(1, SSD_WIDTH)]
    specs = [row_spec(SSD_WIDTH), row_spec(SSD_CONV_DIM), row_spec(DT_PAD),
             const_spec(SSD_CONV, SSD_CONV_DIM), const_spec(1, SSD_CONV_DIM), const_spec(1, DT_PAD),
             const_spec(1, DT_PAD), const_spec(1, SSD_WIDTH), const_spec(1, SSD_WIDTH)]
    scratch = [pltpu.VMEM((ROW_TILE, SSD_WIDTH), BF16),
               pltpu.VMEM((SSD_STATE, SSD_WIDTH), F32),
               pltpu.VMEM((SSD_HALO, SSD_CONV_DIM), F32),
               pltpu.VMEM((ROW_TILE // SSD_CHUNK, SSD_HALO + SSD_CHUNK, SSD_CONV_DIM), F32)]
    return arrays, specs, scratch


def _sgu_tile(uv_ref, lng_ref, lnb_ref, w_ref, bs_ref, o_ref):
    uv = uv_ref[...]
    act = 0.5 * uv * (1.0 + lax.erf(uv * (1.0 / math.sqrt(2.0))))
    u = act[:, :SGU_WIDTH]
    v = act[:, SGU_WIDTH:]
    mu = jnp.mean(v, axis=-1, keepdims=True)
    var = jnp.mean(jnp.square(v - mu), axis=-1, keepdims=True)
    vn = (v - mu) * lax.rsqrt(var + LN_EPS) * lng_ref[...] + lnb_ref[...]
    row = lax.broadcasted_iota(jnp.int32, (SGU_CHUNK, SGU_CHUNK), 0)
    col = lax.broadcasted_iota(jnp.int32, (SGU_CHUNK, SGU_CHUNK), 1)
    w = [jnp.where(row >= col, w_ref[g], 0.0).astype(BF16) for g in range(SGU_GROUPS)]
    lane = lax.broadcasted_iota(jnp.int32, (1, LANES), 1)
    first = lane < SGU_GROUP_DIM
    for c in range(uv_ref.shape[0] // SGU_CHUNK):
        rows = slice(c * SGU_CHUNK, (c + 1) * SGU_CHUNK)
        mixed = []
        for p in range(SGU_WIDTH // LANES):
            vp = vn[rows, p * LANES:(p + 1) * LANES]
            lo = jnp.where(first, vp, 0.0).astype(BF16)
            hi = jnp.where(first, 0.0, vp).astype(BF16)
            mixed.append(jnp.dot(w[2 * p], lo, preferred_element_type=F32)
                         + jnp.dot(w[2 * p + 1], hi, preferred_element_type=F32))
        mixed = jnp.concatenate(mixed, axis=1) + bs_ref[...]
        o_ref[rows, :] = (u[rows, :] * mixed).astype(o_ref.dtype)


def _sgu_operands(uv, ln_g, ln_b, w_s, b_s, row_spec, const_spec):
    bias = jnp.repeat(b_s.T, SGU_GROUP_DIM, axis=1)
    arrays = [uv, ln_g.reshape(1, SGU_WIDTH), ln_b.reshape(1, SGU_WIDTH), w_s, bias]
    specs = [row_spec(UV_WIDTH), const_spec(1, SGU_WIDTH), const_spec(1, SGU_WIDTH),
             pl.BlockSpec((SGU_GROUPS, SGU_CHUNK, SGU_CHUNK), lambda i: (0, 0, 0)),
             const_spec(SGU_CHUNK, SGU_WIDTH)]
    scratch = [pltpu.VMEM((ROW_TILE, SGU_WIDTH), BF16)]
    return arrays, specs, scratch


def _mixers(xn, b, s, w_in):
    qkv, z, xbc, uv, dt = _inproj(xn, w_in)
    y_att = _attention(qkv.reshape(b, s, QKV_WIDTH))
    return y_att.reshape(b * s, ATT_WIDTH), (z, xbc, dt), uv


def kernel(x, ffn1_norm, ffn1_w_gate, ffn1_w_up, ffn1_w_down, mix_norm, w_in, conv_w, conv_b, dt_bias, a_log, d_skip, ssd_norm, sgu_ln_g, sgu_ln_b, sgu_w, sgu_b, w_out, ffn2_norm, ffn2_w_gate, ffn2_w_up, ffn2_w_down, final_norm):
    b, s, d = x.shape
    depth = ffn1_norm.shape[0]
    h = x.reshape(b * s, d)
    for i in range(depth):
        h, hn = _ffn(h, i, ffn1_norm, ffn1_w_gate, ffn1_w_up, ffn1_w_down, next_gain=mix_norm[i])
        y_att, ssd_proj, uv = _mixers(hn, b, s, w_in[i])
        ssd_args = (*ssd_proj, conv_w[i], conv_b[i], dt_bias[i], a_log[i], d_skip[i], ssd_norm[i])
        sgu_args = (uv, sgu_ln_g[i], sgu_ln_b[i], sgu_w[i], sgu_b[i])
        h = _ffn(h, i, ffn2_norm, ffn2_w_gate, ffn2_w_up, ffn2_w_down,
                 mix=(y_att, w_out, s // ROW_TILE, ssd_args, sgu_args),
                 final_gain=final_norm if i == depth - 1 else None)
    return h.reshape(b, s, d)
```

```python
import functools
import math

import numpy as np
import jax
import jax.numpy as jnp
from jax import lax
from jax.experimental import pallas as pl
from jax.experimental.pallas import tpu as pltpu

F32 = jnp.float32
BF16 = jnp.bfloat16

D_MODEL = 1024
D_FF = 2816
HEAD_DIM = 64
ATT_HEADS = 6
ATT_WIDTH = ATT_HEADS * HEAD_DIM
DILATED_PAIRS = ((128, 1), (512, 4), (2048, 16))
SSD_HEADS = 6
SSD_HEADDIM = 64
SSD_WIDTH = SSD_HEADS * SSD_HEADDIM
SSD_GROUPS = 2
SSD_STATE = 128
SSD_CONV = 4
SSD_CHUNK = 128
SSD_CONV_DIM = SSD_WIDTH + 2 * SSD_GROUPS * SSD_STATE
SGU_GROUPS = 4
SGU_GROUP_DIM = 64
SGU_WIDTH = SGU_GROUPS * SGU_GROUP_DIM
SGU_CHUNK = 128
RMS_EPS = 1e-6
LN_EPS = 1e-5

LANES = 128
DT_PAD = LANES
QKV_WIDTH = 3 * ATT_WIDTH
UV_WIDTH = 2 * SGU_WIDTH
PROJ_WIDTH = QKV_WIDTH + SSD_WIDTH + SSD_CONV_DIM + UV_WIDTH + DT_PAD

VMEM_LIMIT = 56 * 1024 * 1024

ROW_TILE = 512
FF_CHUNK = 256
PROJ_CHUNK = 512

ATT_BLOCK = 128
ATT_PIPE = 4


def _params(n_axes):
    return pltpu.CompilerParams(dimension_semantics=("arbitrary",) * n_axes,
                                vmem_limit_bytes=VMEM_LIMIT)


def _rmsnorm_f32(x, g):
    ms = jnp.mean(x * x, axis=-1, keepdims=True)
    return x * lax.rsqrt(ms + RMS_EPS) * g


def _silu(x):
    return x * jax.nn.sigmoid(x)


N_FF_CHUNKS = D_FF // FF_CHUNK
N_WO_CHUNKS = D_MODEL // FF_CHUNK
N_SSD_REFS = 9
N_SGU_REFS = 5


def _ffn_kernel(*refs, mixed, final_norm, tiles_per_seq):
    refs = list(refs)
    x_ref = refs.pop(0)
    if mixed:
        ya_ref, wo_ref = refs[:2]
        ssd_in = refs[2:2 + N_SSD_REFS]
        sgu_in = refs[2 + N_SSD_REFS:2 + N_SSD_REFS + N_SGU_REFS]
        del refs[:2 + N_SSD_REFS + N_SGU_REFS]
    g_ref, wg_ref, wu_ref, wd_ref = refs[:4]
    del refs[:4]
    fg_ref = refs.pop(0) if final_norm else None
    o_ref, wg16_ref, wu16_ref, wd16_ref, xn_ref, h_ref, res_ref = refs[:7]
    if mixed:
        wo16_ref, ys_ref, state_ref, halo_ref, ext_ref, yg_ref = refs[7:]
    step = pl.program_id(0)

    @pl.when(step < N_FF_CHUNKS)
    def _load_weights():
        wg16_ref[step] = wg_ref[...].astype(BF16)
        wu16_ref[step] = wu_ref[...].astype(BF16)
        wd16_ref[pl.ds(pl.multiple_of(step * FF_CHUNK, FF_CHUNK), FF_CHUNK), :] = wd_ref[...].astype(BF16)
        if mixed:
            @pl.when(step < N_WO_CHUNKS)
            def _():
                wo16_ref[pl.ds(pl.multiple_of(step * FF_CHUNK, FF_CHUNK), FF_CHUNK), :] = wo_ref[...].astype(BF16)

            @pl.when(step == 0)
            def _():
                ys_ref[...] = jnp.zeros_like(ys_ref)
                yg_ref[...] = jnp.zeros_like(yg_ref)

    @pl.when(step >= N_FF_CHUNKS)
    def _row_tile():
        x = x_ref[...]
        if mixed:
            a, b = ATT_WIDTH, ATT_WIDTH + SSD_WIDTH
            ya = jnp.concatenate([ya_ref[p] for p in range(a // LANES)], axis=1)
            x = x + (jnp.dot(ya.astype(BF16), wo16_ref[0:a, :], preferred_element_type=F32)
                     + jnp.dot(ys_ref[...], wo16_ref[a:b, :], preferred_element_type=F32)
                     + jnp.dot(yg_ref[...], wo16_ref[b:, :], preferred_element_type=F32))
        res_ref[...] = x
        xn_ref[...] = _rmsnorm_f32(x, g_ref[...]).astype(BF16)
        ssd_chunks = iter(())
        if mixed:
            _sgu_tile(*sgu_in, yg_ref)
            fresh = lax.rem(step - N_FF_CHUNKS, tiles_per_seq) == 0
            ssd_chunks = _ssd_tile(fresh, *ssd_in, ys_ref, state_ref, halo_ref, ext_ref)
        for f in range(N_FF_CHUNKS):
            xn = xn_ref[...]
            gate = jnp.dot(xn, wg16_ref[f], preferred_element_type=F32)
            up = jnp.dot(xn, wu16_ref[f], preferred_element_type=F32)
            h_ref[:, f * FF_CHUNK:(f + 1) * FF_CHUNK] = (_silu(gate) * up).astype(BF16)
        pieces = []
        for n in range(N_WO_CHUNKS):
            pieces.append(jnp.dot(h_ref[...], wd16_ref[:, n * FF_CHUNK:(n + 1) * FF_CHUNK],
                                  preferred_element_type=F32))
            next(ssd_chunks, None)
        for _ in ssd_chunks:
            pass
        y = jnp.concatenate(pieces, axis=1)
        out = res_ref[...] + 0.5 * y
        if final_norm:
            out = _rmsnorm_f32(out, fg_ref[...])
        o_ref[...] = out


def _ffn(x, layer, gain, w_gate, w_up, w_down, mix=None, final_gain=None):
    m = x.shape[0]
    n_tiles = m // ROW_TILE
    lag = 0 if mix is None else 1
    tile = lambda i: jnp.clip(i - N_FF_CHUNKS - lag, 0, n_tiles - 1)
    ahead = lambda i: jnp.clip(i - N_FF_CHUNKS, 0, n_tiles - 1)
    chunk = lambda i: jnp.minimum(i, N_FF_CHUNKS - 1)
    row = lambda n: pl.BlockSpec((ROW_TILE, n), lambda i: (tile(i), 0))
    full = lambda r, c: pl.BlockSpec((r, c), lambda i: (0, 0))
    in_specs, args = [row(D_MODEL)], [x]
    scratch = [pltpu.VMEM((N_FF_CHUNKS, D_MODEL, FF_CHUNK), BF16), pltpu.VMEM((N_FF_CHUNKS, D_MODEL, FF_CHUNK), BF16),
               pltpu.VMEM((D_FF, D_MODEL), BF16),
               pltpu.VMEM((ROW_TILE, D_MODEL), BF16), pltpu.VMEM((ROW_TILE, D_FF), BF16),
               pltpu.VMEM((ROW_TILE, D_MODEL), F32)]
    tiles_per_seq = None
    if mix is not None:
        y_att, w_out, tiles_per_seq, ssd_args, sgu_args = mix
        next_row = lambda n: pl.BlockSpec((ROW_TILE, n), lambda i: (ahead(i), 0))
        ssd_arrays, ssd_specs, ssd_scratch = _ssd_operands(*ssd_args, row_spec=next_row, const_spec=full)
        sgu_arrays, sgu_specs, sgu_scratch = _sgu_operands(*sgu_args, row_spec=next_row, const_spec=full)
        in_specs += [pl.BlockSpec((ATT_WIDTH // LANES, ROW_TILE, LANES), lambda i: (0, tile(i), 0)),
                     pl.BlockSpec((None, FF_CHUNK, D_MODEL), lambda i: (layer, jnp.minimum(i, N_WO_CHUNKS - 1), 0))]
        in_specs += ssd_specs + sgu_specs
        args += [y_att, w_out] + ssd_arrays + sgu_arrays
        scratch += [pltpu.VMEM((D_MODEL, D_MODEL), BF16)] + ssd_scratch + sgu_scratch
    in_specs += [full(1, D_MODEL),
                 pl.BlockSpec((None, D_MODEL, FF_CHUNK), lambda i: (layer, 0, chunk(i))),
                 pl.BlockSpec((None, D_MODEL, FF_CHUNK), lambda i: (layer, 0, chunk(i))),
                 pl.BlockSpec((None, FF_CHUNK, D_MODEL), lambda i: (layer, chunk(i), 0))]
    args += [gain[layer].reshape(1, D_MODEL), w_gate, w_up, w_down]
    if final_gain is not None:
        in_specs.append(full(1, D_MODEL))
        args.append(final_gain.reshape(1, D_MODEL))
    return pl.pallas_call(
        functools.partial(_ffn_kernel, mixed=mix is not None, final_norm=final_gain is not None,
                          tiles_per_seq=tiles_per_seq),
        grid=(N_FF_CHUNKS + n_tiles + lag,),
        in_specs=in_specs,
        out_specs=row(D_MODEL),
        out_shape=jax.ShapeDtypeStruct((m, D_MODEL), F32),
        scratch_shapes=scratch,
        compiler_params=_params(1),
        name="ffn",
    )(*args)


_PROJ_PIECES = (("qkv", QKV_WIDTH, F32),("z", SSD_WIDTH, F32), ("xbc", SSD_CONV_DIM, F32),
                ("uv", UV_WIDTH, F32), ("dt", DT_PAD, F32))


RAW_WIDTH = QKV_WIDTH + SSD_WIDTH + SSD_CONV_DIM
RAW_CHUNK = 256
N_RAW_CHUNKS = RAW_WIDTH // RAW_CHUNK
RAW_REST = RAW_WIDTH - N_RAW_CHUNKS * RAW_CHUNK
TAIL_WIDTH = UV_WIDTH + DT_PAD
N_W_STEPS = N_RAW_CHUNKS + 1
N_PROJ_CHUNKS = PROJ_WIDTH // PROJ_CHUNK


def _inproj_kernel(x_ref, g_ref, w_ref, wrest_ref, wtail_ref, qkv_ref, z_ref, xbc_ref, uv_ref, dt_ref,
                   w16_ref, xn_ref):
    outs = (qkv_ref, z_ref, xbc_ref, uv_ref, dt_ref)
    step = pl.program_id(0)

    for c in range(N_RAW_CHUNKS):
        @pl.when(step == c)
        def _(c=c):
            w16_ref[:, c * RAW_CHUNK:(c + 1) * RAW_CHUNK] = w_ref[...].astype(BF16)

    @pl.when(step == N_RAW_CHUNKS)
    def _():
        w16_ref[:, N_RAW_CHUNKS * RAW_CHUNK:RAW_WIDTH] = wrest_ref[...].astype(BF16)
        w16_ref[:, RAW_WIDTH:] = wtail_ref[...].astype(BF16)

    @pl.when(step >= N_W_STEPS)
    def _row_tile():
        xn_ref[...] = _rmsnorm_f32(x_ref[...], g_ref[...]).astype(BF16)
        starts = np.cumsum([0] + [p[1] for p in _PROJ_PIECES])
        for c in range(N_PROJ_CHUNKS):
            lo, hi = c * PROJ_CHUNK, (c + 1) * PROJ_CHUNK
            r = jnp.dot(xn_ref[...], w16_ref[:, lo:hi], preferred_element_type=F32)
            for k, o_ref in enumerate(outs):
                a, b = max(lo, int(starts[k])), min(hi, int(starts[k + 1]))
                if a >= b:
                    continue
                if o_ref is qkv_ref:
                    for col in range(a, b, LANES):
                        o_ref[col // LANES] = r[:, col - lo:col - lo + LANES]
                else:
                    o_ref[:, a - int(starts[k]):b - int(starts[k])] = r[:, a - lo:b - lo].astype(o_ref.dtype)


def _inproj(x, layer, gain, w_in):
    m = x.shape[0]
    assert RAW_REST == LANES and RAW_WIDTH % LANES == 0
    dt0 = RAW_WIDTH
    uv0 = RAW_WIDTH + SSD_HEADS
    w_tail = jnp.concatenate([w_in[layer, :, uv0:uv0 + UV_WIDTH],
                              jnp.pad(w_in[layer, :, dt0:uv0], ((0, 0), (0, DT_PAD - SSD_HEADS)))], axis=1)
    tile = lambda i: jnp.maximum(i - N_W_STEPS, 0)
    row = lambda w: pl.BlockSpec((ROW_TILE, w), lambda i: (tile(i), 0))
    n_qkv = QKV_WIDTH // LANES
    out_specs = [row(w) for _, w, _ in _PROJ_PIECES]
    out_shape = [jax.ShapeDtypeStruct((m, w), dt) for _, w, dt in _PROJ_PIECES]
    out_specs[0] = pl.BlockSpec((n_qkv, ROW_TILE, LANES), lambda i: (0, tile(i), 0))
    out_shape[0] = jax.ShapeDtypeStruct((n_qkv, m, LANES), F32)
    return pl.pallas_call(
        _inproj_kernel,
        grid=(N_W_STEPS + m // ROW_TILE,),
        in_specs=[row(D_MODEL), pl.BlockSpec((1, D_MODEL), lambda i: (0, 0)),
                  pl.BlockSpec((None, D_MODEL, RAW_CHUNK), lambda i: (layer, 0, jnp.minimum(i, N_RAW_CHUNKS - 1))),
                  pl.BlockSpec((None, D_MODEL, RAW_REST), lambda i: (layer, 0, RAW_WIDTH // RAW_REST - 1)),
                  pl.BlockSpec((D_MODEL, TAIL_WIDTH), lambda i: (0, 0))],
        out_specs=out_specs,
        out_shape=out_shape,
        scratch_shapes=[pltpu.VMEM((D_MODEL, PROJ_WIDTH), BF16), pltpu.VMEM((ROW_TILE, D_MODEL), BF16)],
        compiler_params=_params(1),
        name="inproj",
    )(x, gain[layer].reshape(1, D_MODEL), w_in, w_in, w_tail)


NAT, P4, P16 = 0, 1, 2


def _att_kernel(q_ref, k_ref, v_ref, o_ref, qa_ref, qb_ref, kk_ref, ve_ref,
                acc1_ref, m1_ref, l1_ref, acc3_ref, m3_ref, l3_ref, q4_ref, k4_ref, v4_ref, band_ref, cur_ref):
    seq = q_ref.shape[0]
    T = ATT_BLOCK
    d4, d16 = DILATED_PAIRS[1][1], DILATED_PAIRS[2][1]
    sub4 = seq // d4
    lane = lax.broadcasted_iota(jnp.int32, (1, LANES), 1)
    first = lane < HEAD_DIM
    qi = lax.broadcasted_iota(jnp.int32, (T, T), 0)
    kj = lax.broadcasted_iota(jnp.int32, (T, T), 1)
    cur_bias = jnp.where(kj <= qi, 0.0, -jnp.inf).astype(F32)
    prev_bias = jnp.where(kj >= qi, 0.0, -jnp.inf).astype(F32)
    for half in range(2):
        cur_ref[half * T:(half + 1) * T, :] = cur_bias
        band_ref[half * T:(half + 1) * T, 0:T] = prev_bias
        band_ref[half * T:(half + 1) * T, T:2 * T] = cur_bias
    q_scale = HEAD_DIM ** -0.5 * math.log2(math.e)

    def prep(layout, dst, q, k, v):
        q = q * q_scale
        qa_ref[layout, dst, :] = jnp.where(first, q, 0.0).astype(BF16)
        qb_ref[layout, dst, :] = jnp.where(first, 0.0, q).astype(BF16)
        kk_ref[layout, dst, :] = k.astype(BF16)
        ve_ref[layout, dst, 0:LANES] = v.astype(BF16)
        ve_ref[layout, dst, LANES:2 * LANES] = jnp.ones((T, LANES), BF16)

    for c in range(seq // T):
        rows = pl.ds(c * T, T)
        prep(NAT, rows, q_ref[rows, :], k_ref[rows, :], v_ref[rows, :])
        src = pl.ds(c // d4 + (c % d4) * (T * d4), T, stride=d4)
        q, k, v = q_ref[src, :], k_ref[src, :], v_ref[src, :]
        q4_ref[rows, :] = q
        k4_ref[rows, :] = k
        v4_ref[rows, :] = v
        prep(P4, rows, q, k, v)

    for r16 in range(d16):
        rows = pl.ds(r16 * T, T)
        src = pl.ds((r16 % d4) * sub4 + r16 // d4, T, stride=d4)
        prep(P16, rows, q4_ref[src, :], k4_ref[src, :], v4_ref[src, :])

    def block(layout, qrows, krows, bias_ref):
        def scores():
            q2 = jnp.concatenate([qa_ref[layout, qrows, :], qb_ref[layout, qrows, :]], axis=0)
            s = lax.dot_general(q2, kk_ref[layout, krows, :], (((1,), (1,)), ((), ())),
                                preferred_element_type=F32) + bias_ref[...]
            m = jnp.max(s, axis=-1, keepdims=True)
            return jnp.exp2(s - m).astype(BF16), m

        def values(p, m):
            r = jnp.dot(p, ve_ref[layout, krows, :], preferred_element_type=F32)
            acc = jnp.where(first, r[0:T, 0:LANES], r[T:2 * T, 0:LANES])
            lsum = jnp.where(first, r[0:T, LANES:2 * LANES], r[T:2 * T, LANES:2 * LANES])
            return acc, jnp.where(first, m[0:T], m[T:2 * T]), lsum

        return scores, values

    def rows_of(start, n=T):
        return pl.ds(start, n)

    work = []

    def sink1(rows):
        def store(acc, mb, lsum):
            acc1_ref[rows, :] = acc
            m1_ref[rows, :] = mb
            l1_ref[rows, :] = lsum
        return store

    work.append((*block(NAT, rows_of(0), rows_of(0), cur_ref), sink1(rows_of(0))))
    for n in range(1, seq // T):
        work.append((*block(NAT, rows_of(n * T), rows_of((n - 1) * T, 2 * T), band_ref), sink1(rows_of(n * T))))

    def sink3(r16):
        def store(acc, mb, lsum):
            dst = pl.ds((r16 % d4) * sub4 + r16 // d4, T, stride=d4)
            acc3_ref[dst, :] = acc
            m3_ref[dst, :] = mb
            l3_ref[dst, :] = lsum
        return store

    for r16 in range(d16):
        work.append((*block(P16, rows_of(r16 * T), rows_of(r16 * T), cur_ref), sink3(r16)))

    def sink2(r4, n):
        def merge(acc2, mb2, l2):
            prow = rows_of(r4 * sub4 + n * T)
            trow = pl.ds(r4 + n * (T * d4), T, stride=d4)
            acc1, mb1, l1 = acc1_ref[trow, :], m1_ref[trow, :], l1_ref[trow, :]
            acc3, mb3, l3 = acc3_ref[prow, :], m3_ref[prow, :], l3_ref[prow, :]
            m = jnp.maximum(mb1, jnp.maximum(mb2, mb3))
            w1, w2, w3 = jnp.exp2(mb1 - m), jnp.exp2(mb2 - m), jnp.exp2(mb3 - m)
            num = w1 * acc1 + w2 * acc2 + w3 * acc3
            den = w1 * l1 + w2 * l2 + w3 * l3
            o_ref[trow, :] = num / den
        return merge

    for r4 in range(d4):
        work.append((*block(P4, rows_of(r4 * sub4), rows_of(r4 * sub4), cur_ref), sink2(r4, 0)))
        for n in range(1, sub4 // T):
            work.append((*block(P4, rows_of(r4 * sub4 + n * T), rows_of(r4 * sub4 + (n - 1) * T, 2 * T), band_ref),
                         sink2(r4, n)))

    staged = []
    for scores, values, sink in work:
        staged.append((values, sink, scores()))
        if len(staged) > ATT_PIPE:
            values0, sink0, pm = staged.pop(0)
            sink0(*values0(*pm))
    for values0, sink0, pm in staged:
        sink0(*values0(*pm))


def _attention(qkv):
    _, b, s, _ = qkv.shape
    for window, dil in DILATED_PAIRS:
        assert window // dil == ATT_BLOCK and s % (ATT_BLOCK * dil) == 0
    assert DILATED_PAIRS[0][1] == 1 and DILATED_PAIRS[2][1] == DILATED_PAIRS[1][1] ** 2
    n_pairs = ATT_WIDTH // LANES
    spec = lambda part: pl.BlockSpec((None, None, s, LANES), lambda bi, hp: (part * n_pairs + hp, bi, 0, 0))
    return pl.pallas_call(
        _att_kernel,
        grid=(b, n_pairs),
        in_specs=[spec(0), spec(1), spec(2)],
        out_specs=pl.BlockSpec((None, None, s, LANES), lambda bi, hp: (hp, bi, 0, 0)),
        out_shape=jax.ShapeDtypeStruct((n_pairs, b, s, LANES), F32),
        scratch_shapes=[pltpu.VMEM((3, s, LANES), BF16)] * 3 + [pltpu.VMEM((3, s, 2 * LANES), BF16)]
        + [pltpu.VMEM((s, LANES), F32)] * 9
        + [pltpu.VMEM((2 * ATT_BLOCK, 2 * ATT_BLOCK), F32), pltpu.VMEM((2 * ATT_BLOCK, ATT_BLOCK), F32)],
        compiler_params=_params(2),
        name="dilated_attention",
    )(qkv, qkv, qkv)


SSD_HALO = 8
HEADS_PER_GROUP = SSD_HEADS // SSD_GROUPS
GROUP_LANES = HEADS_PER_GROUP * SSD_HEADDIM


def _ssd_tile(fresh, z_ref, xbc_ref, dt_ref, cw_ref, cb_ref, dtb_ref, alog_ref, dsk_ref, ng_ref,
              o_ref, state_ref, halo_ref, ext_ref):
    n_chunks = z_ref.shape[0] // SSD_CHUNK
    L = SSD_CHUNK
    row = lax.broadcasted_iota(jnp.int32, (L, L), 0)
    col = lax.broadcasted_iota(jnp.int32, (L, L), 1)
    tril = row >= col
    cumsum_mat = tril.astype(F32)
    lane = lax.broadcasted_iota(jnp.int32, (1, LANES), 1)
    lane_w = lax.broadcasted_iota(jnp.int32, (1, SSD_WIDTH), 1)
    first_group = lane_w < GROUP_LANES
    first_head = lane < SSD_HEADDIM
    a_neg = -jnp.exp(alog_ref[...])
    n_b = SSD_GROUPS * SSD_STATE

    state = jnp.where(fresh, 0.0, state_ref[...])
    for c in range(n_chunks):
        rows = slice(c * L, (c + 1) * L)
        if c == 0:
            halo = jnp.where(fresh, 0.0, halo_ref[...])
        else:
            halo = xbc_ref[c * L - SSD_HALO:c * L, :]
        ext_ref[c, 0:SSD_HALO, :] = halo
        ext_ref[c, SSD_HALO:, :] = xbc_ref[rows, :]
        conv = cb_ref[...]
        for w in range(SSD_CONV):
            o = SSD_HALO - (SSD_CONV - 1) + w
            conv = conv + cw_ref[w:w + 1, :] * ext_ref[c, o:o + L, :]
        xact = _silu(conv)
        xs = xact[:, :SSD_WIDTH]
        bm = [xact[:, SSD_WIDTH + g * SSD_STATE:SSD_WIDTH + (g + 1) * SSD_STATE] for g in range(SSD_GROUPS)]
        cm = [xact[:, SSD_WIDTH + n_b + g * SSD_STATE:SSD_WIDTH + n_b + (g + 1) * SSD_STATE]
              for g in range(SSD_GROUPS)]
        bmt16 = [t.T.astype(BF16) for t in bm]
        cm16 = [t.astype(BF16) for t in cm]

        dt = jax.nn.softplus(dt_ref[rows, :] + dtb_ref[...])
        a = dt * a_neg
        acs = jnp.dot(cumsum_mat, a, precision=lax.Precision.HIGHEST, preferred_element_type=F32)
        acs_t = acs.T
        dt_t = dt.T
        acs_last = acs[L - 1:L, :]
        exp_acs_h = jnp.exp(acs)
        to_end_h = jnp.exp(acs_last - acs) * dt
        chunk_decay_h = jnp.exp(acs_last)
        cb = [jnp.dot(cm16[g], bmt16[g], preferred_element_type=F32) for g in range(SSD_GROUPS)]

        y_diag, e_pairs, w_pairs, d_pairs = [], [], [], []
        for p in range(SSD_HEADS // 2):
            xs_pair = xs[:, p * LANES:(p + 1) * LANES].astype(BF16)
            yd, ecol, wcol, dcol = [], [], [], []
            for h in (2 * p, 2 * p + 1):
                g = h // HEADS_PER_GROUP
                acs_col = jnp.broadcast_to(acs[:, h:h + 1], (L, L))
                seg = acs_col - acs_t[h:h + 1, :]
                decay = jnp.exp(jnp.where(tril, seg, -jnp.inf))
                mix = (cb[g] * decay * dt_t[h:h + 1, :]).astype(BF16)
                yd.append(jnp.dot(mix, xs_pair, preferred_element_type=F32))
                ecol.append(jnp.broadcast_to(exp_acs_h[:, h:h + 1], (L, LANES)))
                wcol.append(jnp.broadcast_to(to_end_h[:, h:h + 1], (L, LANES)))
                dcol.append(jnp.broadcast_to(chunk_decay_h[:, h:h + 1], (1, LANES)))
            y_diag.append(jnp.where(first_head, yd[0], yd[1]))
            e_pairs.append(jnp.where(first_head, ecol[0], ecol[1]))
            w_pairs.append(jnp.where(first_head, wcol[0], wcol[1]))
            d_pairs.append(jnp.where(first_head, dcol[0], dcol[1]))
        y_diag = jnp.concatenate(y_diag, axis=1)
        exp_acs = jnp.concatenate(e_pairs, axis=1)
        to_end = jnp.concatenate(w_pairs, axis=1)
        chunk_decay = jnp.concatenate(d_pairs, axis=1)

        st16 = state.astype(BF16)
        y_off = jnp.where(first_group,
                          jnp.dot(cm16[0], st16, preferred_element_type=F32),
                          jnp.dot(cm16[1], st16, preferred_element_type=F32)) * exp_acs
        xdd = (xs * to_end).astype(BF16)
        new = jnp.where(first_group,
                        jnp.dot(bmt16[0], xdd, preferred_element_type=F32),
                        jnp.dot(bmt16[1], xdd, preferred_element_type=F32))
        state = state * chunk_decay + new

        y = y_diag + y_off + dsk_ref[...] * xs
        y = y * _silu(z_ref[rows, :])
        ysq = y * y
        s0 = jnp.sum(jnp.where(first_group, ysq, 0.0), axis=-1, keepdims=True)
        s1 = jnp.sum(jnp.where(first_group, 0.0, ysq), axis=-1, keepdims=True)
        ms = jnp.where(first_group, s0, s1) * (1.0 / GROUP_LANES)
        o_ref[rows, :] = (y * lax.rsqrt(ms + RMS_EPS) * ng_ref[...]).astype(o_ref.dtype)
        if c == n_chunks - 1:
            state_ref[...] = state
            halo_ref[...] = xbc_ref[n_chunks * L - SSD_HALO:n_chunks * L, :]
        yield


def _ssd_operands(z, xbc, dt, conv_w, conv_b, dt_bias, a_log, d_skip, norm_g, row_spec, const_spec):
    pad = lambda v: jnp.pad(v, (0, DT_PAD - SSD_HEADS)).reshape(1, DT_PAD)
    arrays = [z, xbc, dt, conv_w, conv_b.reshape(1, SSD_CONV_DIM), pad(dt_bias), pad(a_log),
              jnp.repeat(d_skip, SSD_HEADDIM).reshape(1, SSD_WIDTH), norm_g.reshape(1, SSD_WIDTH)]
    specs = [row_spec(SSD_WIDTH), row_spec(SSD_CONV_DIM), row_spec(DT_PAD),
             const_spec(SSD_CONV, SSD_CONV_DIM), const_spec(1, SSD_CONV_DIM), const_spec(1, DT_PAD),
             const_spec(1, DT_PAD), const_spec(1, SSD_WIDTH), const_spec(1, SSD_WIDTH)]
    scratch = [pltpu.VMEM((ROW_TILE, SSD_WIDTH), BF16),
               pltpu.VMEM((SSD_STATE, SSD_WIDTH), F32),
               pltpu.VMEM((SSD_HALO, SSD_CONV_DIM), F32),
               pltpu.VMEM((ROW_TILE // SSD_CHUNK, SSD_HALO + SSD_CHUNK, SSD_CONV_DIM), F32)]
    return arrays, specs, scratch


def _sgu_tile(uv_ref, lng_ref, lnb_ref, w_ref, bs_ref, o_ref):
    uv = uv_ref[...]
    act = 0.5 * uv * (1.0 + lax.erf(uv * (1.0 / math.sqrt(2.0))))
    u = act[:, :SGU_WIDTH]
    v = act[:, SGU_WIDTH:]
    mu = jnp.mean(v, axis=-1, keepdims=True)
    var = jnp.mean(jnp.square(v - mu), axis=-1, keepdims=True)
    vn = (v - mu) * lax.rsqrt(var + LN_EPS) * lng_ref[...] + lnb_ref[...]
    row = lax.broadcasted_iota(jnp.int32, (SGU_CHUNK, SGU_CHUNK), 0)
    col = lax.broadcasted_iota(jnp.int32, (SGU_CHUNK, SGU_CHUNK), 1)
    w = [jnp.where(row >= col, w_ref[g], 0.0).astype(BF16) for g in range(SGU_GROUPS)]
    lane = lax.broadcasted_iota(jnp.int32, (1, LANES), 1)
    first = lane < SGU_GROUP_DIM
    for c in range(uv_ref.shape[0] // SGU_CHUNK):
        rows = slice(c * SGU_CHUNK, (c + 1) * SGU_CHUNK)
        mixed = []
        for p in range(SGU_WIDTH // LANES):
            vp = vn[rows, p * LANES:(p + 1) * LANES]
            lo = jnp.where(first, vp, 0.0).astype(BF16)
            hi = jnp.where(first, 0.0, vp).astype(BF16)
            mixed.append(jnp.dot(w[2 * p], lo, preferred_element_type=F32)
                         + jnp.dot(w[2 * p + 1], hi, preferred_element_type=F32))
        mixed = jnp.concatenate(mixed, axis=1) + bs_ref[...]
        o_ref[rows, :] = (u[rows, :] * mixed).astype(o_ref.dtype)


def _sgu_operands(uv, ln_g, ln_b, w_s, b_s, row_spec, const_spec):
    bias = jnp.repeat(b_s.T, SGU_GROUP_DIM, axis=1)
    arrays = [uv, ln_g.reshape(1, SGU_WIDTH), ln_b.reshape(1, SGU_WIDTH), w_s, bias]
    specs = [row_spec(UV_WIDTH), const_spec(1, SGU_WIDTH), const_spec(1, SGU_WIDTH),
             pl.BlockSpec((SGU_GROUPS, SGU_CHUNK, SGU_CHUNK), lambda i: (0, 0, 0)),
             const_spec(SGU_CHUNK, SGU_WIDTH)]
    scratch = [pltpu.VMEM((ROW_TILE, SGU_WIDTH), BF16)]
    return arrays, specs, scratch


def _mixers(x, b, s, layer, gain, w_in):
    qkv, z, xbc, uv, dt = _inproj(x, layer, gain, w_in)
    y_att = _attention(qkv.reshape(QKV_WIDTH // LANES, b, s, LANES))
    return y_att.reshape(ATT_WIDTH // LANES, b * s, LANES), (z, xbc, dt), uv


def kernel(x, ffn1_norm, ffn1_w_gate, ffn1_w_up, ffn1_w_down, mix_norm, w_in, conv_w, conv_b, dt_bias, a_log, d_skip, ssd_norm, sgu_ln_g, sgu_ln_b, sgu_w, sgu_b, w_out, ffn2_norm, ffn2_w_gate, ffn2_w_up, ffn2_w_down, final_norm):
    b, s, d = x.shape
    depth = ffn1_norm.shape[0]
    h = x.reshape(b * s, d)
    for i in range(depth):
        h = _ffn(h, i, ffn1_norm, ffn1_w_gate, ffn1_w_up, ffn1_w_down)
        y_att, ssd_proj, uv = _mixers(h, b, s, i, mix_norm, w_in)
        ssd_args = (*ssd_proj, conv_w[i], conv_b[i], dt_bias[i], a_log[i], d_skip[i], ssd_norm[i])
        sgu_args = (uv, sgu_ln_g[i], sgu_ln_b[i], sgu_w[i], sgu_b[i])
        h = _ffn(h, i, ffn2_norm, ffn2_w_gate, ffn2_w_up, ffn2_w_down,
                 mix=(y_att, w_out, s // ROW_TILE, ssd_args, sgu_args),
                 final_gain=final_norm if i == depth - 1 else None)
    return h.reshape(b, s, d)
```

```python
import functools
import math

import numpy as np
import jax
import jax.numpy as jnp
from jax import lax
from jax.experimental import pallas as pl
from jax.experimental.pallas import tpu as pltpu

F32 = jnp.float32
BF16 = jnp.bfloat16

D_MODEL = 1024
D_FF = 2816
HEAD_DIM = 64
ATT_HEADS = 6
ATT_WIDTH = ATT_HEADS * HEAD_DIM
DILATED_PAIRS = ((128, 1), (512, 4), (2048, 16))
SSD_HEADS = 6
SSD_HEADDIM = 64
SSD_WIDTH = SSD_HEADS * SSD_HEADDIM
SSD_GROUPS = 2
SSD_STATE = 128
SSD_CONV = 4
SSD_CHUNK = 128
SSD_CONV_DIM = SSD_WIDTH + 2 * SSD_GROUPS * SSD_STATE
SGU_GROUPS = 4
SGU_GROUP_DIM = 64
SGU_WIDTH = SGU_GROUPS * SGU_GROUP_DIM
SGU_CHUNK = 128
RMS_EPS = 1e-6
LN_EPS = 1e-5

LANES = 128
DT_PAD = LANES
QKV_WIDTH = 3 * ATT_WIDTH
UV_WIDTH = 2 * SGU_WIDTH
PROJ_WIDTH = QKV_WIDTH + SSD_WIDTH + SSD_CONV_DIM + UV_WIDTH + DT_PAD

VMEM_LIMIT = 56 * 1024 * 1024

ROW_TILE = 512
FF_CHUNK = 256
PROJ_CHUNK = 512

ATT_BLOCK = 128
ATT_PIPE = 2


def _params(n_axes):
    return pltpu.CompilerParams(dimension_semantics=("arbitrary",) * n_axes,
                                vmem_limit_bytes=VMEM_LIMIT)


def _rmsnorm_f32(x, g):
    ms = jnp.mean(x * x, axis=-1, keepdims=True)
    return x * lax.rsqrt(ms + RMS_EPS) * g


def _silu(x):
    return x * jax.nn.sigmoid(x)


N_FF_CHUNKS = D_FF // FF_CHUNK
N_WO_CHUNKS = D_MODEL // FF_CHUNK
N_SSD_REFS = 9
N_SGU_REFS = 5


def _ffn_kernel(*refs, mixed, final_norm, tiles_per_seq):
    refs = list(refs)
    x_ref = refs.pop(0)
    if mixed:
        ya_ref, wo_ref = refs[:2]
        ssd_in = refs[2:2 + N_SSD_REFS]
        sgu_in = refs[2 + N_SSD_REFS:2 + N_SSD_REFS + N_SGU_REFS]
        del refs[:2 + N_SSD_REFS + N_SGU_REFS]
    g_ref, wg_ref, wu_ref, wd_ref = refs[:4]
    del refs[:4]
    fg_ref = refs.pop(0) if final_norm else None
    o_ref, wg16_ref, wu16_ref, wd16_ref, xn_ref, h_ref, res_ref = refs[:7]
    if mixed:
        wo16_ref, ys_ref, state_ref, halo_ref, ext_ref, yg_ref = refs[7:]
    step = pl.program_id(0)

    @pl.when(step < N_FF_CHUNKS)
    def _load_weights():
        wg16_ref[step] = wg_ref[...].astype(BF16)
        wu16_ref[step] = wu_ref[...].astype(BF16)
        wd16_ref[pl.ds(pl.multiple_of(step * FF_CHUNK, FF_CHUNK), FF_CHUNK), :] = wd_ref[...].astype(BF16)
        if mixed:
            @pl.when(step < N_WO_CHUNKS)
            def _():
                wo16_ref[pl.ds(pl.multiple_of(step * FF_CHUNK, FF_CHUNK), FF_CHUNK), :] = wo_ref[...].astype(BF16)

            @pl.when(step == 0)
            def _():
                ys_ref[...] = jnp.zeros_like(ys_ref)
                yg_ref[...] = jnp.zeros_like(yg_ref)

    @pl.when(step >= N_FF_CHUNKS)
    def _row_tile():
        x = x_ref[...]
        if mixed:
            a, b = ATT_WIDTH, ATT_WIDTH + SSD_WIDTH
            x = x + (jnp.dot(ya_ref[...].astype(BF16), wo16_ref[0:a, :], preferred_element_type=F32)
                     + jnp.dot(ys_ref[...], wo16_ref[a:b, :], preferred_element_type=F32)
                     + jnp.dot(yg_ref[...], wo16_ref[b:, :], preferred_element_type=F32))
        res_ref[...] = x
        xn_ref[...] = _rmsnorm_f32(x, g_ref[...]).astype(BF16)
        ssd_chunks = iter(())
        if mixed:
            _sgu_tile(*sgu_in, yg_ref)
            fresh = lax.rem(step - N_FF_CHUNKS, tiles_per_seq) == 0
            ssd_chunks = _ssd_tile(fresh, *ssd_in, ys_ref, state_ref, halo_ref, ext_ref)
        for f in range(N_FF_CHUNKS):
            xn = xn_ref[...]
            gate = jnp.dot(xn, wg16_ref[f], preferred_element_type=F32)
            up = jnp.dot(xn, wu16_ref[f], preferred_element_type=F32)
            h_ref[:, f * FF_CHUNK:(f + 1) * FF_CHUNK] = (_silu(gate) * up).astype(BF16)
        pieces = []
        for n in range(N_WO_CHUNKS):
            pieces.append(jnp.dot(h_ref[...], wd16_ref[:, n * FF_CHUNK:(n + 1) * FF_CHUNK],
                                  preferred_element_type=F32))
            next(ssd_chunks, None)
        for _ in ssd_chunks:
            pass
        y = jnp.concatenate(pieces, axis=1)
        out = res_ref[...] + 0.5 * y
        if final_norm:
            out = _rmsnorm_f32(out, fg_ref[...])
        o_ref[...] = out


def _ffn(x, layer, gain, w_gate, w_up, w_down, mix=None, final_gain=None):
    m = x.shape[0]
    n_tiles = m // ROW_TILE
    lag = 0 if mix is None else 1
    tile = lambda i: jnp.clip(i - N_FF_CHUNKS - lag, 0, n_tiles - 1)
    ahead = lambda i: jnp.clip(i - N_FF_CHUNKS, 0, n_tiles - 1)
    chunk = lambda i: jnp.minimum(i, N_FF_CHUNKS - 1)
    row = lambda n: pl.BlockSpec((ROW_TILE, n), lambda i: (tile(i), 0))
    full = lambda r, c: pl.BlockSpec((r, c), lambda i: (0, 0))
    in_specs, args = [row(D_MODEL)], [x]
    scratch = [pltpu.VMEM((N_FF_CHUNKS, D_MODEL, FF_CHUNK), BF16), pltpu.VMEM((N_FF_CHUNKS, D_MODEL, FF_CHUNK), BF16),
               pltpu.VMEM((D_FF, D_MODEL), BF16),
               pltpu.VMEM((ROW_TILE, D_MODEL), BF16), pltpu.VMEM((ROW_TILE, D_FF), BF16),
               pltpu.VMEM((ROW_TILE, D_MODEL), F32)]
    tiles_per_seq = None
    if mix is not None:
        y_att, w_out, tiles_per_seq, ssd_args, sgu_args = mix
        next_row = lambda n: pl.BlockSpec((ROW_TILE, n), lambda i: (ahead(i), 0))
        ssd_arrays, ssd_specs, ssd_scratch = _ssd_operands(*ssd_args, row_spec=next_row, const_spec=full)
        sgu_arrays, sgu_specs, sgu_scratch = _sgu_operands(*sgu_args, row_spec=next_row, const_spec=full)
        in_specs += [row(ATT_WIDTH),
                     pl.BlockSpec((None, FF_CHUNK, D_MODEL), lambda i: (layer, jnp.minimum(i, N_WO_CHUNKS - 1), 0))]
        in_specs += ssd_specs + sgu_specs
        args += [y_att, w_out] + ssd_arrays + sgu_arrays
        scratch += [pltpu.VMEM((D_MODEL, D_MODEL), BF16)] + ssd_scratch + sgu_scratch
    in_specs += [full(1, D_MODEL),
                 pl.BlockSpec((None, D_MODEL, FF_CHUNK), lambda i: (layer, 0, chunk(i))),
                 pl.BlockSpec((None, D_MODEL, FF_CHUNK), lambda i: (layer, 0, chunk(i))),
                 pl.BlockSpec((None, FF_CHUNK, D_MODEL), lambda i: (layer, chunk(i), 0))]
    args += [gain[layer].reshape(1, D_MODEL), w_gate, w_up, w_down]
    if final_gain is not None:
        in_specs.append(full(1, D_MODEL))
        args.append(final_gain.reshape(1, D_MODEL))
    return pl.pallas_call(
        functools.partial(_ffn_kernel, mixed=mix is not None, final_norm=final_gain is not None,
                          tiles_per_seq=tiles_per_seq),
        grid=(N_FF_CHUNKS + n_tiles + lag,),
        in_specs=in_specs,
        out_specs=row(D_MODEL),
        out_shape=jax.ShapeDtypeStruct((m, D_MODEL), F32),
        scratch_shapes=scratch,
        compiler_params=_params(1),
        name="ffn",
    )(*args)


_PROJ_PIECES = (("qkv", QKV_WIDTH, F32),("z", SSD_WIDTH, F32), ("xbc", SSD_CONV_DIM, F32),
                ("uv", UV_WIDTH, F32), ("dt", DT_PAD, F32))


RAW_WIDTH = QKV_WIDTH + SSD_WIDTH + SSD_CONV_DIM
RAW_CHUNK = 256
N_RAW_CHUNKS = RAW_WIDTH // RAW_CHUNK
RAW_REST = RAW_WIDTH - N_RAW_CHUNKS * RAW_CHUNK
TAIL_WIDTH = UV_WIDTH + DT_PAD
N_W_STEPS = N_RAW_CHUNKS + 1
N_PROJ_CHUNKS = PROJ_WIDTH // PROJ_CHUNK


def _inproj_kernel(x_ref, g_ref, w_ref, wrest_ref, wtail_ref, qkv_ref, z_ref, xbc_ref, uv_ref, dt_ref,
                   w16_ref, xn_ref):
    outs = (qkv_ref, z_ref, xbc_ref, uv_ref, dt_ref)
    step = pl.program_id(0)

    for c in range(N_RAW_CHUNKS):
        @pl.when(step == c)
        def _(c=c):
            w16_ref[:, c * RAW_CHUNK:(c + 1) * RAW_CHUNK] = w_ref[...].astype(BF16)

    @pl.when(step == N_RAW_CHUNKS)
    def _():
        w16_ref[:, N_RAW_CHUNKS * RAW_CHUNK:RAW_WIDTH] = wrest_ref[...].astype(BF16)
        w16_ref[:, RAW_WIDTH:] = wtail_ref[...].astype(BF16)

    @pl.when(step >= N_W_STEPS)
    def _row_tile():
        xn_ref[...] = _rmsnorm_f32(x_ref[...], g_ref[...]).astype(BF16)
        starts = np.cumsum([0] + [p[1] for p in _PROJ_PIECES])
        for c in range(N_PROJ_CHUNKS):
            lo, hi = c * PROJ_CHUNK, (c + 1) * PROJ_CHUNK
            r = jnp.dot(xn_ref[...], w16_ref[:, lo:hi], preferred_element_type=F32)
            for k, o_ref in enumerate(outs):
                a, b = max(lo, int(starts[k])), min(hi, int(starts[k + 1]))
                if a < b:
                    o_ref[:, a - int(starts[k]):b - int(starts[k])] = r[:, a - lo:b - lo].astype(o_ref.dtype)


def _inproj(x, layer, gain, w_in):
    m = x.shape[0]
    assert RAW_REST == LANES and RAW_WIDTH % LANES == 0
    dt0 = RAW_WIDTH
    uv0 = RAW_WIDTH + SSD_HEADS
    w_tail = jnp.concatenate([w_in[layer, :, uv0:uv0 + UV_WIDTH],
                              jnp.pad(w_in[layer, :, dt0:uv0], ((0, 0), (0, DT_PAD - SSD_HEADS)))], axis=1)
    row = lambda w: pl.BlockSpec((ROW_TILE, w), lambda i: (jnp.maximum(i - N_W_STEPS, 0), 0))
    return pl.pallas_call(
        _inproj_kernel,
        grid=(N_W_STEPS + m // ROW_TILE,),
        in_specs=[row(D_MODEL), pl.BlockSpec((1, D_MODEL), lambda i: (0, 0)),
                  pl.BlockSpec((None, D_MODEL, RAW_CHUNK), lambda i: (layer, 0, jnp.minimum(i, N_RAW_CHUNKS - 1))),
                  pl.BlockSpec((None, D_MODEL, RAW_REST), lambda i: (layer, 0, RAW_WIDTH // RAW_REST - 1)),
                  pl.BlockSpec((D_MODEL, TAIL_WIDTH), lambda i: (0, 0))],
        out_specs=[row(w) for _, w, _ in _PROJ_PIECES],
        out_shape=[jax.ShapeDtypeStruct((m, w), dt) for _, w, dt in _PROJ_PIECES],
        scratch_shapes=[pltpu.VMEM((D_MODEL, PROJ_WIDTH), BF16), pltpu.VMEM((ROW_TILE, D_MODEL), BF16)],
        compiler_params=_params(1),
        name="inproj",
    )(x, gain[layer].reshape(1, D_MODEL), w_in, w_in, w_tail)


NAT, P4, P16 = 0, 1, 2


def _att_kernel(q_ref, k_ref, v_ref, o_ref, qa_ref, qb_ref, kk_ref, ve_ref,
                acc1_ref, m1_ref, l1_ref, acc3_ref, m3_ref, l3_ref, q4_ref, k4_ref, v4_ref, band_ref, cur_ref):
    seq = q_ref.shape[0]
    T = ATT_BLOCK
    d4, d16 = DILATED_PAIRS[1][1], DILATED_PAIRS[2][1]
    sub4 = seq // d4
    lane = lax.broadcasted_iota(jnp.int32, (1, LANES), 1)
    first = lane < HEAD_DIM
    qi = lax.broadcasted_iota(jnp.int32, (T, T), 0)
    kj = lax.broadcasted_iota(jnp.int32, (T, T), 1)
    cur_bias = jnp.where(kj <= qi, 0.0, -jnp.inf).astype(F32)
    prev_bias = jnp.where(kj >= qi, 0.0, -jnp.inf).astype(F32)
    for half in range(2):
        cur_ref[half * T:(half + 1) * T, :] = cur_bias
        band_ref[half * T:(half + 1) * T, 0:T] = prev_bias
        band_ref[half * T:(half + 1) * T, T:2 * T] = cur_bias
    q_scale = HEAD_DIM ** -0.5 * math.log2(math.e)

    def prep(layout, dst, q, k, v):
        q = q * q_scale
        qa_ref[layout, dst, :] = jnp.where(first, q, 0.0).astype(BF16)
        qb_ref[layout, dst, :] = jnp.where(first, 0.0, q).astype(BF16)
        kk_ref[layout, dst, :] = k.astype(BF16)
        ve_ref[layout, dst, 0:LANES] = v.astype(BF16)
        ve_ref[layout, dst, LANES:2 * LANES] = jnp.ones((T, LANES), BF16)

    for c in range(seq // T):
        rows = pl.ds(c * T, T)
        prep(NAT, rows, q_ref[rows, :], k_ref[rows, :], v_ref[rows, :])
        src = pl.ds(c // d4 + (c % d4) * (T * d4), T, stride=d4)
        q, k, v = q_ref[src, :], k_ref[src, :], v_ref[src, :]
        q4_ref[rows, :] = q
        k4_ref[rows, :] = k
        v4_ref[rows, :] = v
        prep(P4, rows, q, k, v)

    for r16 in range(d16):
        rows = pl.ds(r16 * T, T)
        src = pl.ds((r16 % d4) * sub4 + r16 // d4, T, stride=d4)
        prep(P16, rows, q4_ref[src, :], k4_ref[src, :], v4_ref[src, :])

    def block(layout, qrows, krows, bias_ref):
        def scores():
            q2 = jnp.concatenate([qa_ref[layout, qrows, :], qb_ref[layout, qrows, :]], axis=0)
            s = lax.dot_general(q2, kk_ref[layout, krows, :], (((1,), (1,)), ((), ())),
                                preferred_element_type=F32) + bias_ref[...]
            m = jnp.max(s, axis=-1, keepdims=True)
            return jnp.exp2(s - m).astype(BF16), m

        def values(p, m):
            r = jnp.dot(p, ve_ref[layout, krows, :], preferred_element_type=F32)
            acc = jnp.where(first, r[0:T, 0:LANES], r[T:2 * T, 0:LANES])
            lsum = jnp.where(first, r[0:T, LANES:2 * LANES], r[T:2 * T, LANES:2 * LANES])
            return acc, jnp.where(first, m[0:T], m[T:2 * T]), lsum

        return scores, values

    def rows_of(start, n=T):
        return pl.ds(start, n)

    work = []

    def sink1(rows):
        def store(acc, mb, lsum):
            acc1_ref[rows, :] = acc
            m1_ref[rows, :] = mb
            l1_ref[rows, :] = lsum
        return store

    work.append((*block(NAT, rows_of(0), rows_of(0), cur_ref), sink1(rows_of(0))))
    for n in range(1, seq // T):
        work.append((*block(NAT, rows_of(n * T), rows_of((n - 1) * T, 2 * T), band_ref), sink1(rows_of(n * T))))

    def sink3(r16):
        def store(acc, mb, lsum):
            dst = pl.ds((r16 % d4) * sub4 + r16 // d4, T, stride=d4)
            acc3_ref[dst, :] = acc
            m3_ref[dst, :] = mb
            l3_ref[dst, :] = lsum
        return store

    for r16 in range(d16):
        work.append((*block(P16, rows_of(r16 * T), rows_of(r16 * T), cur_ref), sink3(r16)))

    def sink2(r4, n):
        def merge(acc2, mb2, l2):
            prow = rows_of(r4 * sub4 + n * T)
            trow = pl.ds(r4 + n * (T * d4), T, stride=d4)
            acc1, mb1, l1 = acc1_ref[trow, :], m1_ref[trow, :], l1_ref[trow, :]
            acc3, mb3, l3 = acc3_ref[prow, :], m3_ref[prow, :], l3_ref[prow, :]
            m = jnp.maximum(mb1, jnp.maximum(mb2, mb3))
            w1, w2, w3 = jnp.exp2(mb1 - m), jnp.exp2(mb2 - m), jnp.exp2(mb3 - m)
            num = w1 * acc1 + w2 * acc2 + w3 * acc3
            den = w1 * l1 + w2 * l2 + w3 * l3
            o_ref[trow, :] = num / den
        return merge

    for r4 in range(d4):
        work.append((*block(P4, rows_of(r4 * sub4), rows_of(r4 * sub4), cur_ref), sink2(r4, 0)))
        for n in range(1, sub4 // T):
            work.append((*block(P4, rows_of(r4 * sub4 + n * T), rows_of(r4 * sub4 + (n - 1) * T, 2 * T), band_ref),
                         sink2(r4, n)))

    staged = []
    for scores, values, sink in work:
        staged.append((values, sink, scores()))
        if len(staged) > ATT_PIPE:
            values0, sink0, pm = staged.pop(0)
            sink0(*values0(*pm))
    for values0, sink0, pm in staged:
        sink0(*values0(*pm))


def _attention(qkv):
    b, s, _ = qkv.shape
    for window, dil in DILATED_PAIRS:
        assert window // dil == ATT_BLOCK and s % (ATT_BLOCK * dil) == 0
    assert DILATED_PAIRS[0][1] == 1 and DILATED_PAIRS[2][1] == DILATED_PAIRS[1][1] ** 2
    n_pairs = ATT_WIDTH // LANES
    spec = lambda part: pl.BlockSpec((None, s, LANES), lambda bi, hp: (bi, 0, part * n_pairs + hp))
    return pl.pallas_call(
        _att_kernel,
        grid=(b, n_pairs),
        in_specs=[spec(0), spec(1), spec(2)],
        out_specs=pl.BlockSpec((None, s, LANES), lambda bi, hp: (bi, 0, hp)),
        out_shape=jax.ShapeDtypeStruct((b, s, ATT_WIDTH), F32),
        scratch_shapes=[pltpu.VMEM((3, s, LANES), BF16)] * 3 + [pltpu.VMEM((3, s, 2 * LANES), BF16)]
        + [pltpu.VMEM((s, LANES), F32)] * 9
        + [pltpu.VMEM((2 * ATT_BLOCK, 2 * ATT_BLOCK), F32), pltpu.VMEM((2 * ATT_BLOCK, ATT_BLOCK), F32)],
        compiler_params=_params(2),
        name="dilated_attention",
    )(qkv, qkv, qkv)


SSD_HALO = 8
HEADS_PER_GROUP = SSD_HEADS // SSD_GROUPS
GROUP_LANES = HEADS_PER_GROUP * SSD_HEADDIM


def _ssd_tile(fresh, z_ref, xbc_ref, dt_ref, cw_ref, cb_ref, dtb_ref, alog_ref, dsk_ref, ng_ref,
              o_ref, state_ref, halo_ref, ext_ref):
    n_chunks = z_ref.shape[0] // SSD_CHUNK
    L = SSD_CHUNK
    row = lax.broadcasted_iota(jnp.int32, (L, L), 0)
    col = lax.broadcasted_iota(jnp.int32, (L, L), 1)
    tril = row >= col
    cumsum_mat = tril.astype(F32)
    lane = lax.broadcasted_iota(jnp.int32, (1, LANES), 1)
    lane_w = lax.broadcasted_iota(jnp.int32, (1, SSD_WIDTH), 1)
    first_group = lane_w < GROUP_LANES
    first_head = lane < SSD_HEADDIM
    a_neg = -jnp.exp(alog_ref[...])
    n_b = SSD_GROUPS * SSD_STATE

    state = jnp.where(fresh, 0.0, state_ref[...])
    for c in range(n_chunks):
        rows = slice(c * L, (c + 1) * L)
        if c == 0:
            halo = jnp.where(fresh, 0.0, halo_ref[...])
        else:
            halo = xbc_ref[c * L - SSD_HALO:c * L, :]
        ext_ref[c, 0:SSD_HALO, :] = halo
        ext_ref[c, SSD_HALO:, :] = xbc_ref[rows, :]
        conv = cb_ref[...]
        for w in range(SSD_CONV):
            o = SSD_HALO - (SSD_CONV - 1) + w
            conv = conv + cw_ref[w:w + 1, :] * ext_ref[c, o:o + L, :]
        xact = _silu(conv)
        xs = xact[:, :SSD_WIDTH]
        bm = [xact[:, SSD_WIDTH + g * SSD_STATE:SSD_WIDTH + (g + 1) * SSD_STATE] for g in range(SSD_GROUPS)]
        cm = [xact[:, SSD_WIDTH + n_b + g * SSD_STATE:SSD_WIDTH + n_b + (g + 1) * SSD_STATE]
              for g in range(SSD_GROUPS)]
        bmt16 = [t.T.astype(BF16) for t in bm]
        cm16 = [t.astype(BF16) for t in cm]

        dt = jax.nn.softplus(dt_ref[rows, :] + dtb_ref[...])
        a = dt * a_neg
        acs = jnp.dot(cumsum_mat, a, precision=lax.Precision.HIGHEST, preferred_element_type=F32)
        acs_t = acs.T
        dt_t = dt.T
        acs_last = acs[L - 1:L, :]
        exp_acs_h = jnp.exp(acs)
        to_end_h = jnp.exp(acs_last - acs) * dt
        chunk_decay_h = jnp.exp(acs_last)
        cb = [jnp.dot(cm16[g], bmt16[g], preferred_element_type=F32) for g in range(SSD_GROUPS)]

        y_diag, e_pairs, w_pairs, d_pairs = [], [], [], []
        for p in range(SSD_HEADS // 2):
            xs_pair = xs[:, p * LANES:(p + 1) * LANES].astype(BF16)
            yd, ecol, wcol, dcol = [], [], [], []
            for h in (2 * p, 2 * p + 1):
                g = h // HEADS_PER_GROUP
                acs_col = jnp.broadcast_to(acs[:, h:h + 1], (L, L))
                seg = acs_col - acs_t[h:h + 1, :]
                decay = jnp.exp(jnp.where(tril, seg, -jnp.inf))
                mix = (cb[g] * decay * dt_t[h:h + 1, :]).astype(BF16)
                yd.append(jnp.dot(mix, xs_pair, preferred_element_type=F32))
                ecol.append(jnp.broadcast_to(exp_acs_h[:, h:h + 1], (L, LANES)))
                wcol.append(jnp.broadcast_to(to_end_h[:, h:h + 1], (L, LANES)))
                dcol.append(jnp.broadcast_to(chunk_decay_h[:, h:h + 1], (1, LANES)))
            y_diag.append(jnp.where(first_head, yd[0], yd[1]))
            e_pairs.append(jnp.where(first_head, ecol[0], ecol[1]))
            w_pairs.append(jnp.where(first_head, wcol[0], wcol[1]))
            d_pairs.append(jnp.where(first_head, dcol[0], dcol[1]))
        y_diag = jnp.concatenate(y_diag, axis=1)
        exp_acs = jnp.concatenate(e_pairs, axis=1)
        to_end = jnp.concatenate(w_pairs, axis=1)
        chunk_decay = jnp.concatenate(d_pairs, axis=1)

        st16 = state.astype(BF16)
        y_off = jnp.where(first_group,
                          jnp.dot(cm16[0], st16, preferred_element_type=F32),
                          jnp.dot(cm16[1], st16, preferred_element_type=F32)) * exp_acs
        xdd = (xs * to_end).astype(BF16)
        new = jnp.where(first_group,
                        jnp.dot(bmt16[0], xdd, preferred_element_type=F32),
                        jnp.dot(bmt16[1], xdd, preferred_element_type=F32))
        state = state * chunk_decay + new

        y = y_diag + y_off + dsk_ref[...] * xs
        y = y * _silu(z_ref[rows, :])
        ysq = y * y
        s0 = jnp.sum(jnp.where(first_group, ysq, 0.0), axis=-1, keepdims=True)
        s1 = jnp.sum(jnp.where(first_group, 0.0, ysq), axis=-1, keepdims=True)
        ms = jnp.where(first_group, s0, s1) * (1.0 / GROUP_LANES)
        o_ref[rows, :] = (y * lax.rsqrt(ms + RMS_EPS) * ng_ref[...]).astype(o_ref.dtype)
        if c == n_chunks - 1:
            state_ref[...] = state
            halo_ref[...] = xbc_ref[n_chunks * L - SSD_HALO:n_chunks * L, :]
        yield


def _ssd_operands(z, xbc, dt, conv_w, conv_b, dt_bias, a_log, d_skip, norm_g, row_spec, const_spec):
    pad = lambda v: jnp.pad(v, (0, DT_PAD - SSD_HEADS)).reshape(1, DT_PAD)
    arrays = [z, xbc, dt, conv_w, conv_b.reshape(1, SSD_CONV_DIM), pad(dt_bias), pad(a_log),
              jnp.repeat(d_skip, SSD_HEADDIM).reshape(1, SSD_WIDTH), norm_g.reshape(1, SSD_WIDTH)]
    specs = [row_spec(SSD_WIDTH), row_spec(SSD_CONV_DIM), row_spec(DT_PAD),
             const_spec(SSD_CONV, SSD_CONV_DIM), const_spec(1, SSD_CONV_DIM), const_spec(1, DT_PAD),
             const_spec(1, DT_PAD), const_spec(1, SSD_WIDTH), const_spec(1, SSD_WIDTH)]
    scratch = [pltpu.VMEM((ROW_TILE, SSD_WIDTH), BF16),
               pltpu.VMEM((SSD_STATE, SSD_WIDTH), F32),
               pltpu.VMEM((SSD_HALO, SSD_CONV_DIM), F32),
               pltpu.VMEM((ROW_TILE // SSD_CHUNK, SSD_HALO + SSD_CHUNK, SSD_CONV_DIM), F32)]
    return arrays, specs, scratch


def _sgu_tile(uv_ref, lng_ref, lnb_ref, w_ref, bs_ref, o_ref):
    uv = uv_ref[...]
    act = 0.5 * uv * (1.0 + lax.erf(uv * (1.0 / math.sqrt(2.0))))
    u = act[:, :SGU_WIDTH]
    v = act[:, SGU_WIDTH:]
    mu = jnp.mean(v, axis=-1, keepdims=True)
    var = jnp.mean(jnp.square(v - mu), axis=-1, keepdims=True)
    vn = (v - mu) * lax.rsqrt(var + LN_EPS) * lng_ref[...] + lnb_ref[...]
    row = lax.broadcasted_iota(jnp.int32, (SGU_CHUNK, SGU_CHUNK), 0)
    col = lax.broadcasted_iota(jnp.int32, (SGU_CHUNK, SGU_CHUNK), 1)
    w = [jnp.where(row >= col, w_ref[g], 0.0).astype(BF16) for g in range(SGU_GROUPS)]
    lane = lax.broadcasted_iota(jnp.int32, (1, LANES), 1)
    first = lane < SGU_GROUP_DIM
    for c in range(uv_ref.shape[0] // SGU_CHUNK):
        rows = slice(c * SGU_CHUNK, (c + 1) * SGU_CHUNK)
        mixed = []
        for p in range(SGU_WIDTH // LANES):
            vp = vn[rows, p * LANES:(p + 1) * LANES]
            lo = jnp.where(first, vp, 0.0).astype(BF16)
            hi = jnp.where(first, 0.0, vp).astype(BF16)
            mixed.append(jnp.dot(w[2 * p], lo, preferred_element_type=F32)
                         + jnp.dot(w[2 * p + 1], hi, preferred_element_type=F32))
        mixed = jnp.concatenate(mixed, axis=1) + bs_ref[...]
        o_ref[rows, :] = (u[rows, :] * mixed).astype(o_ref.dtype)


def _sgu_operands(uv, ln_g, ln_b, w_s, b_s, row_spec, const_spec):
    bias = jnp.repeat(b_s.T, SGU_GROUP_DIM, axis=1)
    arrays = [uv, ln_g.reshape(1, SGU_WIDTH), ln_b.reshape(1, SGU_WIDTH), w_s, bias]
    specs = [row_spec(UV_WIDTH), const_spec(1, SGU_WIDTH), const_spec(1, SGU_WIDTH),
             pl.BlockSpec((SGU_GROUPS, SGU_CHUNK, SGU_CHUNK), lambda i: (0, 0, 0)),
             const_spec(SGU_CHUNK, SGU_WIDTH)]
    scratch = [pltpu.VMEM((ROW_TILE, SGU_WIDTH), BF16)]
    return arrays, specs, scratch


def _mixers(x, b, s, layer, gain, w_in):
    qkv, z, xbc, uv, dt = _inproj(x, layer, gain, w_in)
    y_att = _attention(qkv.reshape(b, s, QKV_WIDTH))
    return y_att.reshape(b * s, ATT_WIDTH), (z, xbc, dt), uv


def kernel(x, ffn1_norm, ffn1_w_gate, ffn1_w_up, ffn1_w_down, mix_norm, w_in, conv_w, conv_b, dt_bias, a_log, d_skip, ssd_norm, sgu_ln_g, sgu_ln_b, sgu_w, sgu_b, w_out, ffn2_norm, ffn2_w_gate, ffn2_w_up, ffn2_w_down, final_norm):
    b, s, d = x.shape
    depth = ffn1_norm.shape[0]
    h = x.reshape(b * s, d)
    for i in range(depth):
        h = _ffn(h, i, ffn1_norm, ffn1_w_gate, ffn1_w_up, ffn1_w_down)
        y_att, ssd_proj, uv = _mixers(h, b, s, i, mix_norm, w_in)
        ssd_args = (*ssd_proj, conv_w[i], conv_b[i], dt_bias[i], a_log[i], d_skip[i], ssd_norm[i])
        sgu_args = (uv, sgu_ln_g[i], sgu_ln_b[i], sgu_w[i], sgu_b[i])
        h = _ffn(h, i, ffn2_norm, ffn2_w_gate, ffn2_w_up, ffn2_w_down,
                 mix=(y_att, w_out, s // ROW_TILE, ssd_args, sgu_args),
                 final_gain=final_norm if i == depth - 1 else None)
    return h.reshape(b, s, d)
```

```python
import functools
import math

import numpy as np
import jax
import jax.numpy as jnp
from jax import lax
from jax.experimental import pallas as pl
from jax.experimental.pallas import tpu as pltpu

F32 = jnp.float32
BF16 = jnp.bfloat16

D_MODEL = 1024
D_FF = 2816
HEAD_DIM = 64
ATT_HEADS = 6
ATT_WIDTH = ATT_HEADS * HEAD_DIM
DILATED_PAIRS = ((128, 1), (512, 4), (2048, 16))
SSD_HEADS = 6
SSD_HEADDIM = 64
SSD_WIDTH = SSD_HEADS * SSD_HEADDIM
SSD_GROUPS = 2
SSD_STATE = 128
SSD_CONV = 4
SSD_CHUNK = 128
SSD_CONV_DIM = SSD_WIDTH + 2 * SSD_GROUPS * SSD_STATE
SGU_GROUPS = 4
SGU_GROUP_DIM = 64
SGU_WIDTH = SGU_GROUPS * SGU_GROUP_DIM
SGU_CHUNK = 128
RMS_EPS = 1e-6
LN_EPS = 1e-5

LANES = 128
DT_PAD = LANES
QKV_WIDTH = 3 * ATT_WIDTH
UV_WIDTH = 2 * SGU_WIDTH
PROJ_WIDTH = QKV_WIDTH + SSD_WIDTH + SSD_CONV_DIM + UV_WIDTH + DT_PAD

VMEM_LIMIT = 56 * 1024 * 1024

ROW_TILE = 512
FF_CHUNK = 256
PROJ_CHUNK = 512

ATT_BLOCK = 128
ATT_PIPE = 4


def _params(n_axes):
    return pltpu.CompilerParams(dimension_semantics=("arbitrary",) * n_axes,
                                vmem_limit_bytes=VMEM_LIMIT)


def _rmsnorm_f32(x, g):
    ms = jnp.mean(x * x, axis=-1, keepdims=True)
    return x * lax.rsqrt(ms + RMS_EPS) * g


def _silu(x):
    return x * jax.nn.sigmoid(x)


N_FF_CHUNKS = D_FF // FF_CHUNK
N_WO_CHUNKS = D_MODEL // FF_CHUNK
N_SSD_REFS = 9
N_SGU_REFS = 5


def _ffn_kernel(*refs, mixed, final_norm, tiles_per_seq):
    refs = list(refs)
    x_ref = refs.pop(0)
    if mixed:
        ya_ref, wo_ref = refs[:2]
        ssd_in = refs[2:2 + N_SSD_REFS]
        sgu_in = refs[2 + N_SSD_REFS:2 + N_SSD_REFS + N_SGU_REFS]
        del refs[:2 + N_SSD_REFS + N_SGU_REFS]
    g_ref, wg_ref, wu_ref, wd_ref = refs[:4]
    del refs[:4]
    fg_ref = refs.pop(0) if final_norm else None
    o_ref, wg16_ref, wu16_ref, wd16_ref, xn_ref, h_ref, res_ref = refs[:7]
    if mixed:
        wo16_ref, ys_ref, state_ref, halo_ref, ext_ref, yg_ref = refs[7:]
    step = pl.program_id(0)

    @pl.when(step < N_FF_CHUNKS)
    def _load_weights():
        wg16_ref[step] = wg_ref[...].astype(BF16)
        wu16_ref[step] = wu_ref[...].astype(BF16)
        wd16_ref[pl.ds(pl.multiple_of(step * FF_CHUNK, FF_CHUNK), FF_CHUNK), :] = wd_ref[...].astype(BF16)
        if mixed:
            @pl.when(step < N_WO_CHUNKS)
            def _():
                wo16_ref[pl.ds(pl.multiple_of(step * FF_CHUNK, FF_CHUNK), FF_CHUNK), :] = wo_ref[...].astype(BF16)

            @pl.when(step == 0)
            def _():
                ys_ref[...] = jnp.zeros_like(ys_ref)
                yg_ref[...] = jnp.zeros_like(yg_ref)

    @pl.when(step >= N_FF_CHUNKS)
    def _row_tile():
        x = x_ref[...]
        if mixed:
            a, b = ATT_WIDTH, ATT_WIDTH + SSD_WIDTH
            x = x + (jnp.dot(ya_ref[...].astype(BF16), wo16_ref[0:a, :], preferred_element_type=F32)
                     + jnp.dot(ys_ref[...], wo16_ref[a:b, :], preferred_element_type=F32)
                     + jnp.dot(yg_ref[...], wo16_ref[b:, :], preferred_element_type=F32))
        res_ref[...] = x
        xn_ref[...] = _rmsnorm_f32(x, g_ref[...]).astype(BF16)
        ssd_chunks = iter(())
        if mixed:
            _sgu_tile(*sgu_in, yg_ref)
            fresh = lax.rem(step - N_FF_CHUNKS, tiles_per_seq) == 0
            ssd_chunks = _ssd_tile(fresh, *ssd_in, ys_ref, state_ref, halo_ref, ext_ref)
        for f in range(N_FF_CHUNKS):
            xn = xn_ref[...]
            gate = jnp.dot(xn, wg16_ref[f], preferred_element_type=F32)
            up = jnp.dot(xn, wu16_ref[f], preferred_element_type=F32)
            h_ref[:, f * FF_CHUNK:(f + 1) * FF_CHUNK] = (_silu(gate) * up).astype(BF16)
        pieces = []
        for n in range(N_WO_CHUNKS):
            pieces.append(jnp.dot(h_ref[...], wd16_ref[:, n * FF_CHUNK:(n + 1) * FF_CHUNK],
                                  preferred_element_type=F32))
            next(ssd_chunks, None)
        for _ in ssd_chunks:
            pass
        y = jnp.concatenate(pieces, axis=1)
        out = res_ref[...] + 0.5 * y
        if final_norm:
            out = _rmsnorm_f32(out, fg_ref[...])
        o_ref[...] = out


def _ffn(x, layer, gain, w_gate, w_up, w_down, mix=None, final_gain=None):
    m = x.shape[0]
    n_tiles = m // ROW_TILE
    lag = 0 if mix is None else 1
    tile = lambda i: jnp.clip(i - N_FF_CHUNKS - lag, 0, n_tiles - 1)
    ahead = lambda i: jnp.clip(i - N_FF_CHUNKS, 0, n_tiles - 1)
    chunk = lambda i: jnp.minimum(i, N_FF_CHUNKS - 1)
    row = lambda n: pl.BlockSpec((ROW_TILE, n), lambda i: (tile(i), 0))
    full = lambda r, c: pl.BlockSpec((r, c), lambda i: (0, 0))
    in_specs, args = [row(D_MODEL)], [x]
    scratch = [pltpu.VMEM((N_FF_CHUNKS, D_MODEL, FF_CHUNK), BF16), pltpu.VMEM((N_FF_CHUNKS, D_MODEL, FF_CHUNK), BF16),
               pltpu.VMEM((D_FF, D_MODEL), BF16),
               pltpu.VMEM((ROW_TILE, D_MODEL), BF16), pltpu.VMEM((ROW_TILE, D_FF), BF16),
               pltpu.VMEM((ROW_TILE, D_MODEL), F32)]
    tiles_per_seq = None
    if mix is not None:
        y_att, w_out, tiles_per_seq, ssd_args, sgu_args = mix
        next_row = lambda n: pl.BlockSpec((ROW_TILE, n), lambda i: (ahead(i), 0))
        ssd_arrays, ssd_specs, ssd_scratch = _ssd_operands(*ssd_args, row_spec=next_row, const_spec=full)
        sgu_arrays, sgu_specs, sgu_scratch = _sgu_operands(*sgu_args, row_spec=next_row, const_spec=full)
        in_specs += [row(ATT_WIDTH),
                     pl.BlockSpec((None, FF_CHUNK, D_MODEL), lambda i: (layer, jnp.minimum(i, N_WO_CHUNKS - 1), 0))]
        in_specs += ssd_specs + sgu_specs
        args += [y_att, w_out] + ssd_arrays + sgu_arrays
        scratch += [pltpu.VMEM((D_MODEL, D_MODEL), BF16)] + ssd_scratch + sgu_scratch
    in_specs += [full(1, D_MODEL),
                 pl.BlockSpec((None, D_MODEL, FF_CHUNK), lambda i: (layer, 0, chunk(i))),
                 pl.BlockSpec((None, D_MODEL, FF_CHUNK), lambda i: (layer, 0, chunk(i))),
                 pl.BlockSpec((None, FF_CHUNK, D_MODEL), lambda i: (layer, chunk(i), 0))]
    args += [gain[layer].reshape(1, D_MODEL), w_gate, w_up, w_down]
    if final_gain is not None:
        in_specs.append(full(1, D_MODEL))
        args.append(final_gain.reshape(1, D_MODEL))
    return pl.pallas_call(
        functools.partial(_ffn_kernel, mixed=mix is not None, final_norm=final_gain is not None,
                          tiles_per_seq=tiles_per_seq),
        grid=(N_FF_CHUNKS + n_tiles + lag,),
        in_specs=in_specs,
        out_specs=row(D_MODEL),
        out_shape=jax.ShapeDtypeStruct((m, D_MODEL), F32),
        scratch_shapes=scratch,
        compiler_params=_params(1),
        name="ffn",
    )(*args)


_PROJ_PIECES = (("qkv", QKV_WIDTH, F32),("z", SSD_WIDTH, F32), ("xbc", SSD_CONV_DIM, F32),
                ("uv", UV_WIDTH, F32), ("dt", DT_PAD, F32))


RAW_WIDTH = QKV_WIDTH + SSD_WIDTH + SSD_CONV_DIM
W_ROWS = 128
N_W_STEPS = D_MODEL // W_ROWS
N_PROJ_CHUNKS = PROJ_WIDTH // PROJ_CHUNK


def _inproj_kernel(x_ref, g_ref, w_ref, qkv_ref, z_ref, xbc_ref, uv_ref, dt_ref, w16_ref, xn_ref):
    outs = (qkv_ref, z_ref, xbc_ref, uv_ref, dt_ref)
    step = pl.program_id(0)

    @pl.when(step < N_W_STEPS)
    def _load_weights():
        rows = pl.ds(pl.multiple_of(step * W_ROWS, W_ROWS), W_ROWS)
        w = w_ref[...]
        uv0 = RAW_WIDTH + SSD_HEADS
        lane = lax.broadcasted_iota(jnp.int32, (1, LANES), 1)
        w16_ref[rows, 0:RAW_WIDTH] = w[:, 0:RAW_WIDTH].astype(BF16)
        w16_ref[rows, RAW_WIDTH:RAW_WIDTH + UV_WIDTH] = w[:, uv0:uv0 + UV_WIDTH].astype(BF16)
        dt_cols = jnp.where(lane < SSD_HEADS, w[:, RAW_WIDTH:RAW_WIDTH + LANES], 0.0)
        w16_ref[rows, RAW_WIDTH + UV_WIDTH:] = dt_cols.astype(BF16)

    @pl.when(step >= N_W_STEPS)
    def _row_tile():
        xn_ref[...] = _rmsnorm_f32(x_ref[...], g_ref[...]).astype(BF16)
        starts = np.cumsum([0] + [p[1] for p in _PROJ_PIECES])
        for c in range(N_PROJ_CHUNKS):
            lo, hi = c * PROJ_CHUNK, (c + 1) * PROJ_CHUNK
            r = jnp.dot(xn_ref[...], w16_ref[:, lo:hi], preferred_element_type=F32)
            for k, o_ref in enumerate(outs):
                a, b = max(lo, int(starts[k])), min(hi, int(starts[k + 1]))
                if a < b:
                    o_ref[:, a - int(starts[k]):b - int(starts[k])] = r[:, a - lo:b - lo].astype(o_ref.dtype)


def _inproj(x, layer, gain, w_in):
    m = x.shape[0]
    d_in = w_in.shape[-1]
    assert d_in == RAW_WIDTH + SSD_HEADS + UV_WIDTH and RAW_WIDTH % LANES == 0
    row = lambda w: pl.BlockSpec((ROW_TILE, w), lambda i: (jnp.maximum(i - N_W_STEPS, 0), 0))
    return pl.pallas_call(
        _inproj_kernel,
        grid=(N_W_STEPS + m // ROW_TILE,),
        in_specs=[row(D_MODEL), pl.BlockSpec((1, D_MODEL), lambda i: (0, 0)),
                  pl.BlockSpec((None, W_ROWS, d_in), lambda i: (layer, jnp.minimum(i, N_W_STEPS - 1), 0))],
        out_specs=[row(w) for _, w, _ in _PROJ_PIECES],
        out_shape=[jax.ShapeDtypeStruct((m, w), dt) for _, w, dt in _PROJ_PIECES],
        scratch_shapes=[pltpu.VMEM((D_MODEL, PROJ_WIDTH), BF16), pltpu.VMEM((ROW_TILE, D_MODEL), BF16)],
        compiler_params=_params(1),
        name="inproj",
    )(x, gain[layer].reshape(1, D_MODEL), w_in)


NAT, P4, P16 = 0, 1, 2


def _att_kernel(q_ref, k_ref, v_ref, o_ref, qa_ref, qb_ref, kk_ref, ve_ref,
                acc1_ref, m1_ref, l1_ref, acc3_ref, m3_ref, l3_ref, q4_ref, k4_ref, v4_ref, band_ref, cur_ref):
    seq = q_ref.shape[0]
    T = ATT_BLOCK
    d4, d16 = DILATED_PAIRS[1][1], DILATED_PAIRS[2][1]
    sub4 = seq // d4
    lane = lax.broadcasted_iota(jnp.int32, (1, LANES), 1)
    first = lane < HEAD_DIM
    qi = lax.broadcasted_iota(jnp.int32, (T, T), 0)
    kj = lax.broadcasted_iota(jnp.int32, (T, T), 1)
    cur_bias = jnp.where(kj <= qi, 0.0, -jnp.inf).astype(F32)
    prev_bias = jnp.where(kj >= qi, 0.0, -jnp.inf).astype(F32)
    for half in range(2):
        cur_ref[half * T:(half + 1) * T, :] = cur_bias
        band_ref[half * T:(half + 1) * T, 0:T] = prev_bias
        band_ref[half * T:(half + 1) * T, T:2 * T] = cur_bias
    q_scale = HEAD_DIM ** -0.5 * math.log2(math.e)

    def prep(layout, dst, q, k, v):
        q = q * q_scale
        qa_ref[layout, dst, :] = jnp.where(first, q, 0.0).astype(BF16)
        qb_ref[layout, dst, :] = jnp.where(first, 0.0, q).astype(BF16)
        kk_ref[layout, dst, :] = k.astype(BF16)
        ve_ref[layout, dst, 0:LANES] = v.astype(BF16)
        ve_ref[layout, dst, LANES:2 * LANES] = jnp.ones((T, LANES), BF16)

    for c in range(seq // T):
        rows = pl.ds(c * T, T)
        prep(NAT, rows, q_ref[rows, :], k_ref[rows, :], v_ref[rows, :])
        src = pl.ds(c // d4 + (c % d4) * (T * d4), T, stride=d4)
        q, k, v = q_ref[src, :], k_ref[src, :], v_ref[src, :]
        q4_ref[rows, :] = q
        k4_ref[rows, :] = k
        v4_ref[rows, :] = v
        prep(P4, rows, q, k, v)

    for r16 in range(d16):
        rows = pl.ds(r16 * T, T)
        src = pl.ds((r16 % d4) * sub4 + r16 // d4, T, stride=d4)
        prep(P16, rows, q4_ref[src, :], k4_ref[src, :], v4_ref[src, :])

    def block(layout, qrows, krows, bias_ref):
        def scores():
            q2 = jnp.concatenate([qa_ref[layout, qrows, :], qb_ref[layout, qrows, :]], axis=0)
            s = lax.dot_general(q2, kk_ref[layout, krows, :], (((1,), (1,)), ((), ())),
                                preferred_element_type=F32) + bias_ref[...]
            m = jnp.max(s, axis=-1, keepdims=True)
            return jnp.exp2(s - m).astype(BF16), m

        def values(p, m):
            r = jnp.dot(p, ve_ref[layout, krows, :], preferred_element_type=F32)
            acc = jnp.where(first, r[0:T, 0:LANES], r[T:2 * T, 0:LANES])
            lsum = jnp.where(first, r[0:T, LANES:2 * LANES], r[T:2 * T, LANES:2 * LANES])
            return acc, jnp.where(first, m[0:T], m[T:2 * T]), lsum

        return scores, values

    def rows_of(start, n=T):
        return pl.ds(start, n)

    work = []

    def sink1(rows):
        def store(acc, mb, lsum):
            acc1_ref[rows, :] = acc
            m1_ref[rows, :] = mb
            l1_ref[rows, :] = lsum
        return store

    work.append((*block(NAT, rows_of(0), rows_of(0), cur_ref), sink1(rows_of(0))))
    for n in range(1, seq // T):
        work.append((*block(NAT, rows_of(n * T), rows_of((n - 1) * T, 2 * T), band_ref), sink1(rows_of(n * T))))

    def sink3(r16):
        def store(acc, mb, lsum):
            dst = pl.ds((r16 % d4) * sub4 + r16 // d4, T, stride=d4)
            acc3_ref[dst, :] = acc
            m3_ref[dst, :] = mb
            l3_ref[dst, :] = lsum
        return store

    for r16 in range(d16):
        work.append((*block(P16, rows_of(r16 * T), rows_of(r16 * T), cur_ref), sink3(r16)))

    def sink2(r4, n):
        def merge(acc2, mb2, l2):
            prow = rows_of(r4 * sub4 + n * T)
            trow = pl.ds(r4 + n * (T * d4), T, stride=d4)
            acc1, mb1, l1 = acc1_ref[trow, :], m1_ref[trow, :], l1_ref[trow, :]
            acc3, mb3, l3 = acc3_ref[prow, :], m3_ref[prow, :], l3_ref[prow, :]
            m = jnp.maximum(mb1, jnp.maximum(mb2, mb3))
            w1, w2, w3 = jnp.exp2(mb1 - m), jnp.exp2(mb2 - m), jnp.exp2(mb3 - m)
            num = w1 * acc1 + w2 * acc2 + w3 * acc3
            den = w1 * l1 + w2 * l2 + w3 * l3
            o_ref[trow, :] = num / den
        return merge

    for r4 in range(d4):
        work.append((*block(P4, rows_of(r4 * sub4), rows_of(r4 * sub4), cur_ref), sink2(r4, 0)))
        for n in range(1, sub4 // T):
            work.append((*block(P4, rows_of(r4 * sub4 + n * T), rows_of(r4 * sub4 + (n - 1) * T, 2 * T), band_ref),
                         sink2(r4, n)))

    staged = []
    for scores, values, sink in work:
        staged.append((values, sink, scores()))
        if len(staged) > ATT_PIPE:
            values0, sink0, pm = staged.pop(0)
            sink0(*values0(*pm))
    for values0, sink0, pm in staged:
        sink0(*values0(*pm))


def _attention(qkv):
    b, s, _ = qkv.shape
    for window, dil in DILATED_PAIRS:
        assert window // dil == ATT_BLOCK and s % (ATT_BLOCK * dil) == 0
    assert DILATED_PAIRS[0][1] == 1 and DILATED_PAIRS[2][1] == DILATED_PAIRS[1][1] ** 2
    n_pairs = ATT_WIDTH // LANES
    spec = lambda part: pl.BlockSpec((None, s, LANES), lambda bi, hp: (bi, 0, part * n_pairs + hp))
    return pl.pallas_call(
        _att_kernel,
        grid=(b, n_pairs),
        in_specs=[spec(0), spec(1), spec(2)],
        out_specs=pl.BlockSpec((None, s, LANES), lambda bi, hp: (bi, 0, hp)),
        out_shape=jax.ShapeDtypeStruct((b, s, ATT_WIDTH), F32),
        scratch_shapes=[pltpu.VMEM((3, s, LANES), BF16)] * 3 + [pltpu.VMEM((3, s, 2 * LANES), BF16)]
        + [pltpu.VMEM((s, LANES), F32)] * 9
        + [pltpu.VMEM((2 * ATT_BLOCK, 2 * ATT_BLOCK), F32), pltpu.VMEM((2 * ATT_BLOCK, ATT_BLOCK), F32)],
        compiler_params=_params(2),
        name="dilated_attention",
    )(qkv, qkv, qkv)


SSD_HALO = 8
HEADS_PER_GROUP = SSD_HEADS // SSD_GROUPS
GROUP_LANES = HEADS_PER_GROUP * SSD_HEADDIM


def _ssd_tile(fresh, z_ref, xbc_ref, dt_ref, cw_ref, cb_ref, dtb_ref, alog_ref, dsk_ref, ng_ref,
              o_ref, state_ref, halo_ref, ext_ref):
    n_chunks = z_ref.shape[0] // SSD_CHUNK
    L = SSD_CHUNK
    row = lax.broadcasted_iota(jnp.int32, (L, L), 0)
    col = lax.broadcasted_iota(jnp.int32, (L, L), 1)
    tril = row >= col
    cumsum_mat = tril.astype(F32)
    lane = lax.broadcasted_iota(jnp.int32, (1, LANES), 1)
    lane_w = lax.broadcasted_iota(jnp.int32, (1, SSD_WIDTH), 1)
    first_group = lane_w < GROUP_LANES
    first_head = lane < SSD_HEADDIM
    a_neg = -jnp.exp(alog_ref[...])
    n_b = SSD_GROUPS * SSD_STATE

    state = jnp.where(fresh, 0.0, state_ref[...])
    for c in range(n_chunks):
        rows = slice(c * L, (c + 1) * L)
        if c == 0:
            halo = jnp.where(fresh, 0.0, halo_ref[...])
        else:
            halo = xbc_ref[c * L - SSD_HALO:c * L, :]
        ext_ref[c, 0:SSD_HALO, :] = halo
        ext_ref[c, SSD_HALO:, :] = xbc_ref[rows, :]
        conv = cb_ref[...]
        for w in range(SSD_CONV):
            o = SSD_HALO - (SSD_CONV - 1) + w
            conv = conv + cw_ref[w:w + 1, :] * ext_ref[c, o:o + L, :]
        xact = _silu(conv)
        xs = xact[:, :SSD_WIDTH]
        bm = [xact[:, SSD_WIDTH + g * SSD_STATE:SSD_WIDTH + (g + 1) * SSD_STATE] for g in range(SSD_GROUPS)]
        cm = [xact[:, SSD_WIDTH + n_b + g * SSD_STATE:SSD_WIDTH + n_b + (g + 1) * SSD_STATE]
              for g in range(SSD_GROUPS)]
        bmt16 = [t.T.astype(BF16) for t in bm]
        cm16 = [t.astype(BF16) for t in cm]

        dt = jax.nn.softplus(dt_ref[rows, :] + dtb_ref[...])
        a = dt * a_neg
        acs = jnp.dot(cumsum_mat, a, precision=lax.Precision.HIGHEST, preferred_element_type=F32)
        acs_t = acs.T
        dt_t = dt.T
        acs_last = acs[L - 1:L, :]
        exp_acs_h = jnp.exp(acs)
        to_end_h = jnp.exp(acs_last - acs) * dt
        chunk_decay_h = jnp.exp(acs_last)
        cb = [jnp.dot(cm16[g], bmt16[g], preferred_element_type=F32) for g in range(SSD_GROUPS)]

        y_diag, e_pairs, w_pairs, d_pairs = [], [], [], []
        for p in range(SSD_HEADS // 2):
            xs_pair = xs[:, p * LANES:(p + 1) * LANES].astype(BF16)
            yd, ecol, wcol, dcol = [], [], [], []
            for h in (2 * p, 2 * p + 1):
                g = h // HEADS_PER_GROUP
                acs_col = jnp.broadcast_to(acs[:, h:h + 1], (L, L))
                seg = acs_col - acs_t[h:h + 1, :]
                decay = jnp.exp(jnp.where(tril, seg, -jnp.inf))
                mix = (cb[g] * decay * dt_t[h:h + 1, :]).astype(BF16)
                yd.append(jnp.dot(mix, xs_pair, preferred_element_type=F32))
                ecol.append(jnp.broadcast_to(exp_acs_h[:, h:h + 1], (L, LANES)))
                wcol.append(jnp.broadcast_to(to_end_h[:, h:h + 1], (L, LANES)))
                dcol.append(jnp.broadcast_to(chunk_decay_h[:, h:h + 1], (1, LANES)))
            y_diag.append(jnp.where(first_head, yd[0], yd[1]))
            e_pairs.append(jnp.where(first_head, ecol[0], ecol[1]))
            w_pairs.append(jnp.where(first_head, wcol[0], wcol[1]))
            d_pairs.append(jnp.where(first_head, dcol[0], dcol[1]))
        y_diag = jnp.concatenate(y_diag, axis=1)
        exp_acs = jnp.concatenate(e_pairs, axis=1)
        to_end = jnp.concatenate(w_pairs, axis=1)
        chunk_decay = jnp.concatenate(d_pairs, axis=1)

        st16 = state.astype(BF16)
        y_off = jnp.where(first_group,
                          jnp.dot(cm16[0], st16, preferred_element_type=F32),
                          jnp.dot(cm16[1], st16, preferred_element_type=F32)) * exp_acs
        xdd = (xs * to_end).astype(BF16)
        new = jnp.where(first_group,
                        jnp.dot(bmt16[0], xdd, preferred_element_type=F32),
                        jnp.dot(bmt16[1], xdd, preferred_element_type=F32))
        state = state * chunk_decay + new

        y = y_diag + y_off + dsk_ref[...] * xs
        y = y * _silu(z_ref[rows, :])
        ysq = y * y
        s0 = jnp.sum(jnp.where(first_group, ysq, 0.0), axis=-1, keepdims=True)
        s1 = jnp.sum(jnp.where(first_group, 0.0, ysq), axis=-1, keepdims=True)
        ms = jnp.where(first_group, s0, s1) * (1.0 / GROUP_LANES)
        o_ref[rows, :] = (y * lax.rsqrt(ms + RMS_EPS) * ng_ref[...]).astype(o_ref.dtype)
        if c == n_chunks - 1:
            state_ref[...] = state
            halo_ref[...] = xbc_ref[n_chunks * L - SSD_HALO:n_chunks * L, :]
        yield


def _ssd_operands(z, xbc, dt, conv_w, conv_b, dt_bias, a_log, d_skip, norm_g, row_spec, const_spec):
    pad = lambda v: jnp.pad(v, (0, DT_PAD - SSD_HEADS)).reshape(1, DT_PAD)
    arrays = [z, xbc, dt, conv_w, conv_b.reshape(1, SSD_CONV_DIM), pad(dt_bias), pad(a_log),
              jnp.repeat(d_skip, SSD_HEADDIM).reshape(1, SSD_WIDTH), norm_g.reshape(1, SSD_WIDTH)]
    specs = [row_spec(SSD_WIDTH), row_spec(SSD_CONV_DIM), row_spec(DT_PAD),
             const_spec(SSD_CONV, SSD_CONV_DIM), const_spec(1, SSD_CONV_DIM), const_spec(1, DT_PAD),
             const_spec(1, DT_PAD), const_spec(1, SSD_WIDTH), const_spec(1, SSD_WIDTH)]
    scratch = [pltpu.VMEM((ROW_TILE, SSD_WIDTH), BF16),
               pltpu.VMEM((SSD_STATE, SSD_WIDTH), F32),
               pltpu.VMEM((SSD_HALO, SSD_CONV_DIM), F32),
               pltpu.VMEM((ROW_TILE // SSD_CHUNK, SSD_HALO + SSD_CHUNK, SSD_CONV_DIM), F32)]
    return arrays, specs, scratch


def _sgu_tile(uv_ref, lng_ref, lnb_ref, w_ref, bs_ref, o_ref):
    uv = uv_ref[...]
    act = 0.5 * uv * (1.0 + lax.erf(uv * (1.0 / math.sqrt(2.0))))
    u = act[:, :SGU_WIDTH]
    v = act[:, SGU_WIDTH:]
    mu = jnp.mean(v, axis=-1, keepdims=True)
    var = jnp.mean(jnp.square(v - mu), axis=-1, keepdims=True)
    vn = (v - mu) * lax.rsqrt(var + LN_EPS) * lng_ref[...] + lnb_ref[...]
    row = lax.broadcasted_iota(jnp.int32, (SGU_CHUNK, SGU_CHUNK), 0)
    col = lax.broadcasted_iota(jnp.int32, (SGU_CHUNK, SGU_CHUNK), 1)
    w = [jnp.where(row >= col, w_ref[g], 0.0).astype(BF16) for g in range(SGU_GROUPS)]
    lane = lax.broadcasted_iota(jnp.int32, (1, LANES), 1)
    first = lane < SGU_GROUP_DIM
    for c in range(uv_ref.shape[0] // SGU_CHUNK):
        rows = slice(c * SGU_CHUNK, (c + 1) * SGU_CHUNK)
        mixed = []
        for p in range(SGU_WIDTH // LANES):
            vp = vn[rows, p * LANES:(p + 1) * LANES]
            lo = jnp.where(first, vp, 0.0).astype(BF16)
            hi = jnp.where(first, 0.0, vp).astype(BF16)
            mixed.append(jnp.dot(w[2 * p], lo, preferred_element_type=F32)
                         + jnp.dot(w[2 * p + 1], hi, preferred_element_type=F32))
        mixed = jnp.concatenate(mixed, axis=1) + bs_ref[...]
        o_ref[rows, :] = (u[rows, :] * mixed).astype(o_ref.dtype)


def _sgu_operands(uv, ln_g, ln_b, w_s, b_s, row_spec, const_spec):
    bias = jnp.repeat(b_s.T, SGU_GROUP_DIM, axis=1)
    arrays = [uv, ln_g.reshape(1, SGU_WIDTH), ln_b.reshape(1, SGU_WIDTH), w_s, bias]
    specs = [row_spec(UV_WIDTH), const_spec(1, SGU_WIDTH), const_spec(1, SGU_WIDTH),
             pl.BlockSpec((SGU_GROUPS, SGU_CHUNK, SGU_CHUNK), lambda i: (0, 0, 0)),
             const_spec(SGU_CHUNK, SGU_WIDTH)]
    scratch = [pltpu.VMEM((ROW_TILE, SGU_WIDTH), BF16)]
    return arrays, specs, scratch


def _mixers(x, b, s, layer, gain, w_in):
    qkv, z, xbc, uv, dt = _inproj(x, layer, gain, w_in)
    y_att = _attention(qkv.reshape(b, s, QKV_WIDTH))
    return y_att.reshape(b * s, ATT_WIDTH), (z, xbc, dt), uv


def kernel(x, ffn1_norm, ffn1_w_gate, ffn1_w_up, ffn1_w_down, mix_norm, w_in, conv_w, conv_b, dt_bias, a_log, d_skip, ssd_norm, sgu_ln_g, sgu_ln_b, sgu_w, sgu_b, w_out, ffn2_norm, ffn2_w_gate, ffn2_w_up, ffn2_w_down, final_norm):
    b, s, d = x.shape
    depth = ffn1_norm.shape[0]
    h = x.reshape(b * s, d)
    for i in range(depth):
        h = _ffn(h, i, ffn1_norm, ffn1_w_gate, ffn1_w_up, ffn1_w_down)
        y_att, ssd_proj, uv = _mixers(h, b, s, i, mix_norm, w_in)
        ssd_args = (*ssd_proj, conv_w[i], conv_b[i], dt_bias[i], a_log[i], d_skip[i], ssd_norm[i])
        sgu_args = (uv, sgu_ln_g[i], sgu_ln_b[i], sgu_w[i], sgu_b[i])
        h = _ffn(h, i, ffn2_norm, ffn2_w_gate, ffn2_w_up, ffn2_w_down,
                 mix=(y_att, w_out, s // ROW_TILE, ssd_args, sgu_args),
                 final_gain=final_norm if i == depth - 1 else None)
    return h.reshape(b, s, d)
```

```python
import functools
import math

import numpy as np
import jax
import jax.numpy as jnp
from jax import lax
from jax.experimental import pallas as pl
from jax.experimental.pallas import tpu as pltpu

F32 = jnp.float32
BF16 = jnp.bfloat16

D_MODEL = 1024
D_FF = 2816
HEAD_DIM = 64
ATT_HEADS = 6
ATT_WIDTH = ATT_HEADS * HEAD_DIM
DILATED_PAIRS = ((128, 1), (512, 4), (2048, 16))
SSD_HEADS = 6
SSD_HEADDIM = 64
SSD_WIDTH = SSD_HEADS * SSD_HEADDIM
SSD_GROUPS = 2
SSD_STATE = 128
SSD_CONV = 4
SSD_CHUNK = 128
SSD_CONV_DIM = SSD_WIDTH + 2 * SSD_GROUPS * SSD_STATE
SGU_GROUPS = 4
SGU_GROUP_DIM = 64
SGU_WIDTH = SGU_GROUPS * SGU_GROUP_DIM
SGU_CHUNK = 128
RMS_EPS = 1e-6
LN_EPS = 1e-5

LANES = 128
DT_PAD = LANES
QKV_WIDTH = 3 * ATT_WIDTH
UV_WIDTH = 2 * SGU_WIDTH
PROJ_WIDTH = QKV_WIDTH + SSD_WIDTH + SSD_CONV_DIM + UV_WIDTH + DT_PAD

VMEM_LIMIT = 56 * 1024 * 1024

ROW_TILE = 512
FF_CHUNK = 256
PROJ_CHUNK = 512

ATT_BLOCK = 128
ATT_PIPE = 4


def _params(n_axes):
    return pltpu.CompilerParams(dimension_semantics=("arbitrary",) * n_axes,
                                vmem_limit_bytes=VMEM_LIMIT)


def _rmsnorm_f32(x, g):
    ms = jnp.mean(x * x, axis=-1, keepdims=True)
    return x * lax.rsqrt(ms + RMS_EPS) * g


def _silu(x):
    return x * jax.nn.sigmoid(x)


N_FF_CHUNKS = D_FF // FF_CHUNK
N_WO_CHUNKS = D_MODEL // FF_CHUNK
N_SSD_REFS = 9
N_SGU_REFS = 5


def _ffn_kernel(*refs, mixed, final_norm, tiles_per_seq):
    refs = list(refs)
    x_ref = refs.pop(0)
    if mixed:
        ya_ref, wo_ref = refs[:2]
        ssd_in = refs[2:2 + N_SSD_REFS]
        sgu_in = refs[2 + N_SSD_REFS:2 + N_SSD_REFS + N_SGU_REFS]
        del refs[:2 + N_SSD_REFS + N_SGU_REFS]
    g_ref, wg_ref, wu_ref, wd_ref = refs[:4]
    del refs[:4]
    fg_ref = refs.pop(0) if final_norm else None
    o_ref, wg16_ref, wu16_ref, wd16_ref, xn_ref, h_ref, res_ref = refs[:7]
    if mixed:
        wo16_ref, ys_ref, state_ref, halo_ref, ext_ref, yg_ref = refs[7:]
    step = pl.program_id(0)

    @pl.when(step < N_FF_CHUNKS)
    def _load_weights():
        wg16_ref[step] = wg_ref[...].astype(BF16)
        wu16_ref[step] = wu_ref[...].astype(BF16)
        wd16_ref[pl.ds(pl.multiple_of(step * FF_CHUNK, FF_CHUNK), FF_CHUNK), :] = wd_ref[...].astype(BF16)
        if mixed:
            @pl.when(step < N_WO_CHUNKS)
            def _():
                wo16_ref[pl.ds(pl.multiple_of(step * FF_CHUNK, FF_CHUNK), FF_CHUNK), :] = wo_ref[...].astype(BF16)

            @pl.when(step == 0)
            def _():
                ys_ref[...] = jnp.zeros_like(ys_ref)
                yg_ref[...] = jnp.zeros_like(yg_ref)

    @pl.when(step >= N_FF_CHUNKS)
    def _row_tile():
        x = x_ref[...]
        if mixed:
            a, b = ATT_WIDTH, ATT_WIDTH + SSD_WIDTH
            x = x + (jnp.dot(ya_ref[...].astype(BF16), wo16_ref[0:a, :], preferred_element_type=F32)
                     + jnp.dot(ys_ref[...], wo16_ref[a:b, :], preferred_element_type=F32)
                     + jnp.dot(yg_ref[...], wo16_ref[b:, :], preferred_element_type=F32))
        res_ref[...] = x
        xn_ref[...] = _rmsnorm_f32(x, g_ref[...]).astype(BF16)
        ssd_chunks = iter(())
        if mixed:
            _sgu_tile(*sgu_in, yg_ref)
            fresh = lax.rem(step - N_FF_CHUNKS, tiles_per_seq) == 0
            ssd_chunks = _ssd_tile(fresh, *ssd_in, ys_ref, state_ref, halo_ref, ext_ref)
        for f in range(N_FF_CHUNKS):
            xn = xn_ref[...]
            gate = jnp.dot(xn, wg16_ref[f], preferred_element_type=F32)
            up = jnp.dot(xn, wu16_ref[f], preferred_element_type=F32)
            h_ref[:, f * FF_CHUNK:(f + 1) * FF_CHUNK] = (_silu(gate) * up).astype(BF16)
        pieces = []
        for n in range(N_WO_CHUNKS):
            pieces.append(jnp.dot(h_ref[...], wd16_ref[:, n * FF_CHUNK:(n + 1) * FF_CHUNK],
                                  preferred_element_type=F32))
            next(ssd_chunks, None)
        for _ in ssd_chunks:
            pass
        y = jnp.concatenate(pieces, axis=1)
        out = res_ref[...] + 0.5 * y
        if final_norm:
            out = _rmsnorm_f32(out, fg_ref[...])
        o_ref[...] = out


def _ffn(x, layer, gain, w_gate, w_up, w_down, mix=None, final_gain=None):
    m = x.shape[0]
    n_tiles = m // ROW_TILE
    lag = 0 if mix is None else 1
    tile = lambda i: jnp.clip(i - N_FF_CHUNKS - lag, 0, n_tiles - 1)
    ahead = lambda i: jnp.clip(i - N_FF_CHUNKS, 0, n_tiles - 1)
    chunk = lambda i: jnp.minimum(i, N_FF_CHUNKS - 1)
    row = lambda n: pl.BlockSpec((ROW_TILE, n), lambda i: (tile(i), 0))
    full = lambda r, c: pl.BlockSpec((r, c), lambda i: (0, 0))
    in_specs, args = [row(D_MODEL)], [x]
    scratch = [pltpu.VMEM((N_FF_CHUNKS, D_MODEL, FF_CHUNK), BF16), pltpu.VMEM((N_FF_CHUNKS, D_MODEL, FF_CHUNK), BF16),
               pltpu.VMEM((D_FF, D_MODEL), BF16),
               pltpu.VMEM((ROW_TILE, D_MODEL), BF16), pltpu.VMEM((ROW_TILE, D_FF), BF16),
               pltpu.VMEM((ROW_TILE, D_MODEL), F32)]
    tiles_per_seq = None
    if mix is not None:
        y_att, w_out, tiles_per_seq, ssd_args, sgu_args = mix
        next_row = lambda n: pl.BlockSpec((ROW_TILE, n), lambda i: (ahead(i), 0))
        ssd_arrays, ssd_specs, ssd_scratch = _ssd_operands(*ssd_args, row_spec=next_row, const_spec=full)
        sgu_arrays, sgu_specs, sgu_scratch = _sgu_operands(*sgu_args, row_spec=next_row, const_spec=full)
        in_specs += [row(ATT_WIDTH),
                     pl.BlockSpec((None, FF_CHUNK, D_MODEL), lambda i: (layer, jnp.minimum(i, N_WO_CHUNKS - 1), 0))]
        in_specs += ssd_specs + sgu_specs
        args += [y_att, w_out] + ssd_arrays + sgu_arrays
        scratch += [pltpu.VMEM((D_MODEL, D_MODEL), BF16)] + ssd_scratch + sgu_scratch
    in_specs += [full(1, D_MODEL),
                 pl.BlockSpec((None, D_MODEL, FF_CHUNK), lambda i: (layer, 0, chunk(i))),
                 pl.BlockSpec((None, D_MODEL, FF_CHUNK), lambda i: (layer, 0, chunk(i))),
                 pl.BlockSpec((None, FF_CHUNK, D_MODEL), lambda i: (layer, chunk(i), 0))]
    args += [gain[layer].reshape(1, D_MODEL), w_gate, w_up, w_down]
    if final_gain is not None:
        in_specs.append(full(1, D_MODEL))
        args.append(final_gain.reshape(1, D_MODEL))
    return pl.pallas_call(
        functools.partial(_ffn_kernel, mixed=mix is not None, final_norm=final_gain is not None,
                          tiles_per_seq=tiles_per_seq),
        grid=(N_FF_CHUNKS + n_tiles + lag,),
        in_specs=in_specs,
        out_specs=row(D_MODEL),
        out_shape=jax.ShapeDtypeStruct((m, D_MODEL), F32),
        scratch_shapes=scratch,
        compiler_params=_params(1),
        name="ffn",
    )(*args)


_PROJ_PIECES = (("qkv", QKV_WIDTH, F32),("z", SSD_WIDTH, F32), ("xbc", SSD_CONV_DIM, F32),
                ("uv", UV_WIDTH, F32), ("dt", DT_PAD, F32))


RAW_WIDTH = QKV_WIDTH + SSD_WIDTH + SSD_CONV_DIM
W_ROWS = 128
N_W_STEPS = D_MODEL // W_ROWS
N_PROJ_CHUNKS = PROJ_WIDTH // PROJ_CHUNK


def _inproj_kernel(x_ref, g_ref, w_ref, qkv_ref, z_ref, xbc_ref, uv_ref, dt_ref, w16_ref, xn_ref):
    outs = (qkv_ref, z_ref, xbc_ref, uv_ref, dt_ref)
    step = pl.program_id(0)

    @pl.when(step < N_W_STEPS)
    def _load_weights():
        rows = pl.ds(pl.multiple_of(step * W_ROWS, W_ROWS), W_ROWS)
        w = w_ref[...]
        uv0 = RAW_WIDTH + SSD_HEADS
        lane = lax.broadcasted_iota(jnp.int32, (1, LANES), 1)
        w16_ref[rows, 0:RAW_WIDTH] = w[:, 0:RAW_WIDTH].astype(BF16)
        w16_ref[rows, RAW_WIDTH:RAW_WIDTH + UV_WIDTH] = w[:, uv0:uv0 + UV_WIDTH].astype(BF16)
        dt_cols = jnp.where(lane < SSD_HEADS, w[:, RAW_WIDTH:RAW_WIDTH + LANES], 0.0)
        w16_ref[rows, RAW_WIDTH + UV_WIDTH:] = dt_cols.astype(BF16)

    @pl.when(step >= N_W_STEPS)
    def _row_tile():
        xn_ref[...] = _rmsnorm_f32(x_ref[...], g_ref[...]).astype(BF16)
        starts = np.cumsum([0] + [p[1] for p in _PROJ_PIECES])
        for c in range(N_PROJ_CHUNKS):
            lo, hi = c * PROJ_CHUNK, (c + 1) * PROJ_CHUNK
            r = jnp.dot(xn_ref[...], w16_ref[:, lo:hi], preferred_element_type=F32)
            for k, o_ref in enumerate(outs):
                a, b = max(lo, int(starts[k])), min(hi, int(starts[k + 1]))
                if a < b:
                    o_ref[:, a - int(starts[k]):b - int(starts[k])] = r[:, a - lo:b - lo].astype(o_ref.dtype)


def _inproj(x, layer, gain, w_in):
    m = x.shape[0]
    d_in = w_in.shape[-1]
    assert d_in == RAW_WIDTH + SSD_HEADS + UV_WIDTH and RAW_WIDTH % LANES == 0
    row = lambda w: pl.BlockSpec((ROW_TILE, w), lambda i: (jnp.maximum(i - N_W_STEPS, 0), 0))
    return pl.pallas_call(
        _inproj_kernel,
        grid=(N_W_STEPS + m // ROW_TILE,),
        in_specs=[row(D_MODEL), pl.BlockSpec((1, D_MODEL), lambda i: (0, 0)),
                  pl.BlockSpec((W_ROWS, d_in), lambda i: (layer * N_W_STEPS + jnp.minimum(i, N_W_STEPS - 1), 0))],
        out_specs=[row(w) for _, w, _ in _PROJ_PIECES],
        out_shape=[jax.ShapeDtypeStruct((m, w), dt) for _, w, dt in _PROJ_PIECES],
        scratch_shapes=[pltpu.VMEM((D_MODEL, PROJ_WIDTH), BF16), pltpu.VMEM((ROW_TILE, D_MODEL), BF16)],
        compiler_params=_params(1),
        name="inproj",
    )(x, gain[layer].reshape(1, D_MODEL), w_in.reshape(-1, d_in))


NAT, P4, P16 = 0, 1, 2


def _att_kernel(q_ref, k_ref, v_ref, o_ref, qa_ref, qb_ref, kk_ref, ve_ref,
                acc1_ref, m1_ref, l1_ref, acc3_ref, m3_ref, l3_ref, q4_ref, k4_ref, v4_ref, band_ref, cur_ref):
    seq = q_ref.shape[0]
    T = ATT_BLOCK
    d4, d16 = DILATED_PAIRS[1][1], DILATED_PAIRS[2][1]
    sub4 = seq // d4
    lane = lax.broadcasted_iota(jnp.int32, (1, LANES), 1)
    first = lane < HEAD_DIM
    qi = lax.broadcasted_iota(jnp.int32, (T, T), 0)
    kj = lax.broadcasted_iota(jnp.int32, (T, T), 1)
    cur_bias = jnp.where(kj <= qi, 0.0, -jnp.inf).astype(F32)
    prev_bias = jnp.where(kj >= qi, 0.0, -jnp.inf).astype(F32)
    for half in range(2):
        cur_ref[half * T:(half + 1) * T, :] = cur_bias
        band_ref[half * T:(half + 1) * T, 0:T] = prev_bias
        band_ref[half * T:(half + 1) * T, T:2 * T] = cur_bias
    q_scale = HEAD_DIM ** -0.5 * math.log2(math.e)

    def prep(layout, dst, q, k, v):
        q = q * q_scale
        qa_ref[layout, dst, :] = jnp.where(first, q, 0.0).astype(BF16)
        qb_ref[layout, dst, :] = jnp.where(first, 0.0, q).astype(BF16)
        kk_ref[layout, dst, :] = k.astype(BF16)
        ve_ref[layout, dst, 0:LANES] = v.astype(BF16)
        ve_ref[layout, dst, LANES:2 * LANES] = jnp.ones((T, LANES), BF16)

    for c in range(seq // T):
        rows = pl.ds(c * T, T)
        prep(NAT, rows, q_ref[rows, :], k_ref[rows, :], v_ref[rows, :])
        src = pl.ds(c // d4 + (c % d4) * (T * d4), T, stride=d4)
        q, k, v = q_ref[src, :], k_ref[src, :], v_ref[src, :]
        q4_ref[rows, :] = q
        k4_ref[rows, :] = k
        v4_ref[rows, :] = v
        prep(P4, rows, q, k, v)

    for r16 in range(d16):
        rows = pl.ds(r16 * T, T)
        src = pl.ds((r16 % d4) * sub4 + r16 // d4, T, stride=d4)
        prep(P16, rows, q4_ref[src, :], k4_ref[src, :], v4_ref[src, :])

    def block(layout, qrows, krows, bias_ref):
        def scores():
            q2 = jnp.concatenate([qa_ref[layout, qrows, :], qb_ref[layout, qrows, :]], axis=0)
            s = lax.dot_general(q2, kk_ref[layout, krows, :], (((1,), (1,)), ((), ())),
                                preferred_element_type=F32) + bias_ref[...]
            m = jnp.max(s, axis=-1, keepdims=True)
            return jnp.exp2(s - m).astype(BF16), m

        def values(p, m):
            r = jnp.dot(p, ve_ref[layout, krows, :], preferred_element_type=F32)
            acc = jnp.where(first, r[0:T, 0:LANES], r[T:2 * T, 0:LANES])
            lsum = jnp.where(first, r[0:T, LANES:2 * LANES], r[T:2 * T, LANES:2 * LANES])
            return acc, jnp.where(first, m[0:T], m[T:2 * T]), lsum

        return scores, values

    def rows_of(start, n=T):
        return pl.ds(start, n)

    work = []

    def sink1(rows):
        def store(acc, mb, lsum):
            acc1_ref[rows, :] = acc
            m1_ref[rows, :] = mb
            l1_ref[rows, :] = lsum
        return store

    work.append((*block(NAT, rows_of(0), rows_of(0), cur_ref), sink1(rows_of(0))))
    for n in range(1, seq // T):
        work.append((*block(NAT, rows_of(n * T), rows_of((n - 1) * T, 2 * T), band_ref), sink1(rows_of(n * T))))

    def sink3(r16):
        def store(acc, mb, lsum):
            dst = pl.ds((r16 % d4) * sub4 + r16 // d4, T, stride=d4)
            acc3_ref[dst, :] = acc
            m3_ref[dst, :] = mb
            l3_ref[dst, :] = lsum
        return store

    for r16 in range(d16):
        work.append((*block(P16, rows_of(r16 * T), rows_of(r16 * T), cur_ref), sink3(r16)))

    def sink2(r4, n):
        def merge(acc2, mb2, l2):
            prow = rows_of(r4 * sub4 + n * T)
            trow = pl.ds(r4 + n * (T * d4), T, stride=d4)
            acc1, mb1, l1 = acc1_ref[trow, :], m1_ref[trow, :], l1_ref[trow, :]
            acc3, mb3, l3 = acc3_ref[prow, :], m3_ref[prow, :], l3_ref[prow, :]
            m = jnp.maximum(mb1, jnp.maximum(mb2, mb3))
            w1, w2, w3 = jnp.exp2(mb1 - m), jnp.exp2(mb2 - m), jnp.exp2(mb3 - m)
            num = w1 * acc1 + w2 * acc2 + w3 * acc3
            den = w1 * l1 + w2 * l2 + w3 * l3
            o_ref[trow, :] = num / den
        return merge

    for r4 in range(d4):
        work.append((*block(P4, rows_of(r4 * sub4), rows_of(r4 * sub4), cur_ref), sink2(r4, 0)))
        for n in range(1, sub4 // T):
            work.append((*block(P4, rows_of(r4 * sub4 + n * T), rows_of(r4 * sub4 + (n - 1) * T, 2 * T), band_ref),
                         sink2(r4, n)))

    staged = []
    for scores, values, sink in work:
        staged.append((values, sink, scores()))
        if len(staged) > ATT_PIPE:
            values0, sink0, pm = staged.pop(0)
            sink0(*values0(*pm))
    for values0, sink0, pm in staged:
        sink0(*values0(*pm))


def _attention(qkv):
    b, s, _ = qkv.shape
    for window, dil in DILATED_PAIRS:
        assert window // dil == ATT_BLOCK and s % (ATT_BLOCK * dil) == 0
    assert DILATED_PAIRS[0][1] == 1 and DILATED_PAIRS[2][1] == DILATED_PAIRS[1][1] ** 2
    n_pairs = ATT_WIDTH // LANES
    spec = lambda part: pl.BlockSpec((None, s, LANES), lambda bi, hp: (bi, 0, part * n_pairs + hp))
    return pl.pallas_call(
        _att_kernel,
        grid=(b, n_pairs),
        in_specs=[spec(0), spec(1), spec(2)],
        out_specs=pl.BlockSpec((None, s, LANES), lambda bi, hp: (bi, 0, hp)),
        out_shape=jax.ShapeDtypeStruct((b, s, ATT_WIDTH), F32),
        scratch_shapes=[pltpu.VMEM((3, s, LANES), BF16)] * 3 + [pltpu.VMEM((3, s, 2 * LANES), BF16)]
        + [pltpu.VMEM((s, LANES), F32)] * 9
        + [pltpu.VMEM((2 * ATT_BLOCK, 2 * ATT_BLOCK), F32), pltpu.VMEM((2 * ATT_BLOCK, ATT_BLOCK), F32)],
        compiler_params=_params(2),
        name="dilated_attention",
    )(qkv, qkv, qkv)


SSD_HALO = 8
HEADS_PER_GROUP = SSD_HEADS // SSD_GROUPS
GROUP_LANES = HEADS_PER_GROUP * SSD_HEADDIM


def _ssd_tile(fresh, z_ref, xbc_ref, dt_ref, cw_ref, cb_ref, dtb_ref, alog_ref, dsk_ref, ng_ref,
              o_ref, state_ref, halo_ref, ext_ref):
    n_chunks = z_ref.shape[0] // SSD_CHUNK
    L = SSD_CHUNK
    row = lax.broadcasted_iota(jnp.int32, (L, L), 0)
    col = lax.broadcasted_iota(jnp.int32, (L, L), 1)
    tril = row >= col
    cumsum_mat = tril.astype(F32)
    lane = lax.broadcasted_iota(jnp.int32, (1, LANES), 1)
    lane_w = lax.broadcasted_iota(jnp.int32, (1, SSD_WIDTH), 1)
    first_group = lane_w < GROUP_LANES
    first_head = lane < SSD_HEADDIM
    a_neg = -jnp.exp(alog_ref[...])
    n_b = SSD_GROUPS * SSD_STATE

    state = jnp.where(fresh, 0.0, state_ref[...])
    for c in range(n_chunks):
        rows = slice(c * L, (c + 1) * L)
        if c == 0:
            halo = jnp.where(fresh, 0.0, halo_ref[...])
        else:
            halo = xbc_ref[c * L - SSD_HALO:c * L, :]
        ext_ref[c, 0:SSD_HALO, :] = halo
        ext_ref[c, SSD_HALO:, :] = xbc_ref[rows, :]
        conv = cb_ref[...]
        for w in range(SSD_CONV):
            o = SSD_HALO - (SSD_CONV - 1) + w
            conv = conv + cw_ref[w:w + 1, :] * ext_ref[c, o:o + L, :]
        xact = _silu(conv)
        xs = xact[:, :SSD_WIDTH]
        bm = [xact[:, SSD_WIDTH + g * SSD_STATE:SSD_WIDTH + (g + 1) * SSD_STATE] for g in range(SSD_GROUPS)]
        cm = [xact[:, SSD_WIDTH + n_b + g * SSD_STATE:SSD_WIDTH + n_b + (g + 1) * SSD_STATE]
              for g in range(SSD_GROUPS)]
        bmt16 = [t.T.astype(BF16) for t in bm]
        cm16 = [t.astype(BF16) for t in cm]

        dt = jax.nn.softplus(dt_ref[rows, :] + dtb_ref[...])
        a = dt * a_neg
        acs = jnp.dot(cumsum_mat, a, precision=lax.Precision.HIGHEST, preferred_element_type=F32)
        acs_t = acs.T
        dt_t = dt.T
        acs_last = acs[L - 1:L, :]
        exp_acs_h = jnp.exp(acs)
        to_end_h = jnp.exp(acs_last - acs) * dt
        chunk_decay_h = jnp.exp(acs_last)
        cb = [jnp.dot(cm16[g], bmt16[g], preferred_element_type=F32) for g in range(SSD_GROUPS)]

        y_diag, e_pairs, w_pairs, d_pairs = [], [], [], []
        for p in range(SSD_HEADS // 2):
            xs_pair = xs[:, p * LANES:(p + 1) * LANES].astype(BF16)
            yd, ecol, wcol, dcol = [], [], [], []
            for h in (2 * p, 2 * p + 1):
                g = h // HEADS_PER_GROUP
                acs_col = jnp.broadcast_to(acs[:, h:h + 1], (L, L))
                seg = acs_col - acs_t[h:h + 1, :]
                decay = jnp.exp(jnp.where(tril, seg, -jnp.inf))
                mix = (cb[g] * decay * dt_t[h:h + 1, :]).astype(BF16)
                yd.append(jnp.dot(mix, xs_pair, preferred_element_type=F32))
                ecol.append(jnp.broadcast_to(exp_acs_h[:, h:h + 1], (L, LANES)))
                wcol.append(jnp.broadcast_to(to_end_h[:, h:h + 1], (L, LANES)))
                dcol.append(jnp.broadcast_to(chunk_decay_h[:, h:h + 1], (1, LANES)))
            y_diag.append(jnp.where(first_head, yd[0], yd[1]))
            e_pairs.append(jnp.where(first_head, ecol[0], ecol[1]))
            w_pairs.append(jnp.where(first_head, wcol[0], wcol[1]))
            d_pairs.append(jnp.where(first_head, dcol[0], dcol[1]))
        y_diag = jnp.concatenate(y_diag, axis=1)
        exp_acs = jnp.concatenate(e_pairs, axis=1)
        to_end = jnp.concatenate(w_pairs, axis=1)
        chunk_decay = jnp.concatenate(d_pairs, axis=1)

        st16 = state.astype(BF16)
        y_off = jnp.where(first_group,
                          jnp.dot(cm16[0], st16, preferred_element_type=F32),
                          jnp.dot(cm16[1], st16, preferred_element_type=F32)) * exp_acs
        xdd = (xs * to_end).astype(BF16)
        new = jnp.where(first_group,
                        jnp.dot(bmt16[0], xdd, preferred_element_type=F32),
                        jnp.dot(bmt16[1], xdd, preferred_element_type=F32))
        state = state * chunk_decay + new

        y = y_diag + y_off + dsk_ref[...] * xs
        y = y * _silu(z_ref[rows, :])
        ysq = y * y
        s0 = jnp.sum(jnp.where(first_group, ysq, 0.0), axis=-1, keepdims=True)
        s1 = jnp.sum(jnp.where(first_group, 0.0, ysq), axis=-1, keepdims=True)
        ms = jnp.where(first_group, s0, s1) * (1.0 / GROUP_LANES)
        o_ref[rows, :] = (y * lax.rsqrt(ms + RMS_EPS) * ng_ref[...]).astype(o_ref.dtype)
        if c == n_chunks - 1:
            state_ref[...] = state
            halo_ref[...] = xbc_ref[n_chunks * L - SSD_HALO:n_chunks * L, :]
        yield


def _ssd_operands(z, xbc, dt, conv_w, conv_b, dt_bias, a_log, d_skip, norm_g, row_spec, const_spec):
    pad = lambda v: jnp.pad(v, (0, DT_PAD - SSD_HEADS)).reshape(1, DT_PAD)
    arrays = [z, xbc, dt, conv_w, conv_b.reshape(1, SSD_CONV_DIM), pad(dt_bias), pad(a_log),
              jnp.repeat(d_skip, SSD_HEADDIM).reshape(1, SSD_WIDTH), norm_g.reshape(1, SSD_WIDTH)]
    specs = [row_spec(SSD_WIDTH), row_spec(SSD_CONV_DIM), row_spec(DT_PAD),
             const_spec(SSD_CONV, SSD_CONV_DIM), const_spec(1, SSD_CONV_DIM), const_spec(1, DT_PAD),
             const_spec(1, DT_PAD), const_spec(1, SSD_WIDTH), const_spec(1, SSD_WIDTH)]
    scratch = [pltpu.VMEM((ROW_TILE, SSD_WIDTH), BF16),
               pltpu.VMEM((SSD_STATE, SSD_WIDTH), F32),
               pltpu.VMEM((SSD_HALO, SSD_CONV_DIM), F32),
               pltpu.VMEM((ROW_TILE // SSD_CHUNK, SSD_HALO + SSD_CHUNK, SSD_CONV_DIM), F32)]
    return arrays, specs, scratch


def _sgu_tile(uv_ref, lng_ref, lnb_ref, w_ref, bs_ref, o_ref):
    uv = uv_ref[...]
    act = 0.5 * uv * (1.0 + lax.erf(uv * (1.0 / math.sqrt(2.0))))
    u = act[:, :SGU_WIDTH]
    v = act[:, SGU_WIDTH:]
    mu = jnp.mean(v, axis=-1, keepdims=True)
    var = jnp.mean(jnp.square(v - mu), axis=-1, keepdims=True)
    vn = (v - mu) * lax.rsqrt(var + LN_EPS) * lng_ref[...] + lnb_ref[...]
    row = lax.broadcasted_iota(jnp.int32, (SGU_CHUNK, SGU_CHUNK), 0)
    col = lax.broadcasted_iota(jnp.int32, (SGU_CHUNK, SGU_CHUNK), 1)
    w = [jnp.where(row >= col, w_ref[g], 0.0).astype(BF16) for g in range(SGU_GROUPS)]
    lane = lax.broadcasted_iota(jnp.int32, (1, LANES), 1)
    first = lane < SGU_GROUP_DIM
    for c in range(uv_ref.shape[0] // SGU_CHUNK):
        rows = slice(c * SGU_CHUNK, (c + 1) * SGU_CHUNK)
        mixed = []
        for p in range(SGU_WIDTH // LANES):
            vp = vn[rows, p * LANES:(p + 1) * LANES]
            lo = jnp.where(first, vp, 0.0).astype(BF16)
            hi = jnp.where(first, 0.0, vp).astype(BF16)
            mixed.append(jnp.dot(w[2 * p], lo, preferred_element_type=F32)
                         + jnp.dot(w[2 * p + 1], hi, preferred_element_type=F32))
        mixed = jnp.concatenate(mixed, axis=1) + bs_ref[...]
        o_ref[rows, :] = (u[rows, :] * mixed).astype(o_ref.dtype)


def _sgu_operands(uv, ln_g, ln_b, w_s, b_s, row_spec, const_spec):
    bias = jnp.repeat(b_s.T, SGU_GROUP_DIM, axis=1)
    arrays = [uv, ln_g.reshape(1, SGU_WIDTH), ln_b.reshape(1, SGU_WIDTH), w_s, bias]
    specs = [row_spec(UV_WIDTH), const_spec(1, SGU_WIDTH), const_spec(1, SGU_WIDTH),
             pl.BlockSpec((SGU_GROUPS, SGU_CHUNK, SGU_CHUNK), lambda i: (0, 0, 0)),
             const_spec(SGU_CHUNK, SGU_WIDTH)]
    scratch = [pltpu.VMEM((ROW_TILE, SGU_WIDTH), BF16)]
    return arrays, specs, scratch


def _mixers(x, b, s, layer, gain, w_in):
    qkv, z, xbc, uv, dt = _inproj(x, layer, gain, w_in)
    y_att = _attention(qkv.reshape(b, s, QKV_WIDTH))
    return y_att.reshape(b * s, ATT_WIDTH), (z, xbc, dt), uv


def kernel(x, ffn1_norm, ffn1_w_gate, ffn1_w_up, ffn1_w_down, mix_norm, w_in, conv_w, conv_b, dt_bias, a_log, d_skip, ssd_norm, sgu_ln_g, sgu_ln_b, sgu_w, sgu_b, w_out, ffn2_norm, ffn2_w_gate, ffn2_w_up, ffn2_w_down, final_norm):
    b, s, d = x.shape
    depth = ffn1_norm.shape[0]
    h = x.reshape(b * s, d)
    for i in range(depth):
        h = _ffn(h, i, ffn1_norm, ffn1_w_gate, ffn1_w_up, ffn1_w_down)
        y_att, ssd_proj, uv = _mixers(h, b, s, i, mix_norm, w_in)
        ssd_args = (*ssd_proj, conv_w[i], conv_b[i], dt_bias[i], a_log[i], d_skip[i], ssd_norm[i])
        sgu_args = (uv, sgu_ln_g[i], sgu_ln_b[i], sgu_w[i], sgu_b[i])
        h = _ffn(h, i, ffn2_norm, ffn2_w_gate, ffn2_w_up, ffn2_w_down,
                 mix=(y_att, w_out, s // ROW_TILE, ssd_args, sgu_args),
                 final_gain=final_norm if i == depth - 1 else None)
    return h.reshape(b, s, d)
```

```python
import functools
import math

import numpy as np
import jax
import jax.numpy as jnp
from jax import lax
from jax.experimental import pallas as pl
from jax.experimental.pallas import tpu as pltpu

F32 = jnp.float32
BF16 = jnp.bfloat16

D_MODEL = 1024
D_FF = 2816
HEAD_DIM = 64
ATT_HEADS = 6
ATT_WIDTH = ATT_HEADS * HEAD_DIM
DILATED_PAIRS = ((128, 1), (512, 4), (2048, 16))
SSD_HEADS = 6
SSD_HEADDIM = 64
SSD_WIDTH = SSD_HEADS * SSD_HEADDIM
SSD_GROUPS = 2
SSD_STATE = 128
SSD_CONV = 4
SSD_CHUNK = 128
SSD_CONV_DIM = SSD_WIDTH + 2 * SSD_GROUPS * SSD_STATE
SGU_GROUPS = 4
SGU_GROUP_DIM = 64
SGU_WIDTH = SGU_GROUPS * SGU_GROUP_DIM
SGU_CHUNK = 128
RMS_EPS = 1e-6
LN_EPS = 1e-5

LANES = 128
DT_PAD = LANES
QKV_WIDTH = 3 * ATT_WIDTH
UV_WIDTH = 2 * SGU_WIDTH
PROJ_WIDTH = QKV_WIDTH + SSD_WIDTH + SSD_CONV_DIM + UV_WIDTH + DT_PAD

VMEM_LIMIT = 56 * 1024 * 1024

ROW_TILE = 512
FF_CHUNK = 256
PROJ_CHUNK = 512

ATT_BLOCK = 128
ATT_PIPE = 4


def _params(n_axes):
    return pltpu.CompilerParams(dimension_semantics=("arbitrary",) * n_axes,
                                vmem_limit_bytes=VMEM_LIMIT)


def _rmsnorm_f32(x, g):
    ms = jnp.mean(x * x, axis=-1, keepdims=True)
    return x * lax.rsqrt(ms + RMS_EPS) * g


def _silu(x):
    return x * jax.nn.sigmoid(x)


N_FF_CHUNKS = D_FF // FF_CHUNK
N_WO_CHUNKS = D_MODEL // FF_CHUNK
N_SSD_REFS = 9
N_SGU_REFS = 5


def _ffn_kernel(*refs, mixed, final_norm, tiles_per_seq):
    refs = list(refs)
    x_ref = refs.pop(0)
    if mixed:
        ya_ref, wo_ref = refs[:2]
        ssd_in = refs[2:2 + N_SSD_REFS]
        sgu_in = refs[2 + N_SSD_REFS:2 + N_SSD_REFS + N_SGU_REFS]
        del refs[:2 + N_SSD_REFS + N_SGU_REFS]
    g_ref, wg_ref, wu_ref, wd_ref = refs[:4]
    del refs[:4]
    fg_ref = refs.pop(0) if final_norm else None
    o_ref, wg16_ref, wu16_ref, wd16_ref, xn_ref, h_ref, res_ref = refs[:7]
    if mixed:
        wo16_ref, ys_ref, state_ref, halo_ref, ext_ref, yg_ref = refs[7:]
    step = pl.program_id(0)

    @pl.when(step < N_FF_CHUNKS)
    def _load_weights():
        wg16_ref[step] = wg_ref[...].astype(BF16)
        wu16_ref[step] = wu_ref[...].astype(BF16)
        wd16_ref[pl.ds(pl.multiple_of(step * FF_CHUNK, FF_CHUNK), FF_CHUNK), :] = wd_ref[...].astype(BF16)
        if mixed:
            @pl.when(step < N_WO_CHUNKS)
            def _():
                wo16_ref[pl.ds(pl.multiple_of(step * FF_CHUNK, FF_CHUNK), FF_CHUNK), :] = wo_ref[...].astype(BF16)

            @pl.when(step == 0)
            def _():
                ys_ref[...] = jnp.zeros_like(ys_ref)
                yg_ref[...] = jnp.zeros_like(yg_ref)

    @pl.when(step >= N_FF_CHUNKS)
    def _row_tile():
        x = x_ref[...]
        if mixed:
            a, b = ATT_WIDTH, ATT_WIDTH + SSD_WIDTH
            x = x + (jnp.dot(ya_ref[...].astype(BF16), wo16_ref[0:a, :], preferred_element_type=F32)
                     + jnp.dot(ys_ref[...], wo16_ref[a:b, :], preferred_element_type=F32)
                     + jnp.dot(yg_ref[...], wo16_ref[b:, :], preferred_element_type=F32))
        res_ref[...] = x
        xn_ref[...] = _rmsnorm_f32(x, g_ref[...]).astype(BF16)
        ssd_chunks = iter(())
        if mixed:
            _sgu_tile(*sgu_in, yg_ref)
            fresh = lax.rem(step - N_FF_CHUNKS, tiles_per_seq) == 0
            ssd_chunks = _ssd_tile(fresh, *ssd_in, ys_ref, state_ref, halo_ref, ext_ref)
        for f in range(N_FF_CHUNKS):
            xn = xn_ref[...]
            gate = jnp.dot(xn, wg16_ref[f], preferred_element_type=F32)
            up = jnp.dot(xn, wu16_ref[f], preferred_element_type=F32)
            h_ref[:, f * FF_CHUNK:(f + 1) * FF_CHUNK] = (_silu(gate) * up).astype(BF16)
        pieces = []
        for n in range(N_WO_CHUNKS):
            pieces.append(jnp.dot(h_ref[...], wd16_ref[:, n * FF_CHUNK:(n + 1) * FF_CHUNK],
                                  preferred_element_type=F32))
            next(ssd_chunks, None)
        for _ in ssd_chunks:
            pass
        y = jnp.concatenate(pieces, axis=1)
        out = res_ref[...] + 0.5 * y
        if final_norm:
            out = _rmsnorm_f32(out, fg_ref[...])
        o_ref[...] = out


def _ffn(x, layer, gain, w_gate, w_up, w_down, mix=None, final_gain=None):
    m = x.shape[0]
    n_tiles = m // ROW_TILE
    lag = 0 if mix is None else 1
    tile = lambda i: jnp.clip(i - N_FF_CHUNKS - lag, 0, n_tiles - 1)
    ahead = lambda i: jnp.clip(i - N_FF_CHUNKS, 0, n_tiles - 1)
    chunk = lambda i: jnp.minimum(i, N_FF_CHUNKS - 1)
    row = lambda n: pl.BlockSpec((ROW_TILE, n), lambda i: (tile(i), 0))
    full = lambda r, c: pl.BlockSpec((r, c), lambda i: (0, 0))
    in_specs, args = [row(D_MODEL)], [x]
    scratch = [pltpu.VMEM((N_FF_CHUNKS, D_MODEL, FF_CHUNK), BF16), pltpu.VMEM((N_FF_CHUNKS, D_MODEL, FF_CHUNK), BF16),
               pltpu.VMEM((D_FF, D_MODEL), BF16),
               pltpu.VMEM((ROW_TILE, D_MODEL), BF16), pltpu.VMEM((ROW_TILE, D_FF), BF16),
               pltpu.VMEM((ROW_TILE, D_MODEL), F32)]
    tiles_per_seq = None
    if mix is not None:
        y_att, w_out, tiles_per_seq, ssd_args, sgu_args = mix
        next_row = lambda n: pl.BlockSpec((ROW_TILE, n), lambda i: (ahead(i), 0))
        ssd_arrays, ssd_specs, ssd_scratch = _ssd_operands(*ssd_args, row_spec=next_row, const_spec=full)
        sgu_arrays, sgu_specs, sgu_scratch = _sgu_operands(*sgu_args, row_spec=next_row, const_spec=full)
        in_specs += [row(ATT_WIDTH),
                     pl.BlockSpec((None, FF_CHUNK, D_MODEL), lambda i: (layer, jnp.minimum(i, N_WO_CHUNKS - 1), 0))]
        in_specs += ssd_specs + sgu_specs
        args += [y_att, w_out] + ssd_arrays + sgu_arrays
        scratch += [pltpu.VMEM((D_MODEL, D_MODEL), BF16)] + ssd_scratch + sgu_scratch
    in_specs += [full(1, D_MODEL),
                 pl.BlockSpec((None, D_MODEL, FF_CHUNK), lambda i: (layer, 0, chunk(i))),
                 pl.BlockSpec((None, D_MODEL, FF_CHUNK), lambda i: (layer, 0, chunk(i))),
                 pl.BlockSpec((None, FF_CHUNK, D_MODEL), lambda i: (layer, chunk(i), 0))]
    args += [gain[layer].reshape(1, D_MODEL), w_gate, w_up, w_down]
    if final_gain is not None:
        in_specs.append(full(1, D_MODEL))
        args.append(final_gain.reshape(1, D_MODEL))
    return pl.pallas_call(
        functools.partial(_ffn_kernel, mixed=mix is not None, final_norm=final_gain is not None,
                          tiles_per_seq=tiles_per_seq),
        grid=(N_FF_CHUNKS + n_tiles + lag,),
        in_specs=in_specs,
        out_specs=row(D_MODEL),
        out_shape=jax.ShapeDtypeStruct((m, D_MODEL), F32),
        scratch_shapes=scratch,
        compiler_params=_params(1),
        name="ffn",
    )(*args)


_PROJ_PIECES = (("qkv", QKV_WIDTH, F32),("z", SSD_WIDTH, F32), ("xbc", SSD_CONV_DIM, F32),
                ("uv", UV_WIDTH, F32), ("dt", DT_PAD, F32))


RAW_WIDTH = QKV_WIDTH + SSD_WIDTH + SSD_CONV_DIM
W_ROWS = 128
N_W_STEPS = D_MODEL // W_ROWS
N_PROJ_CHUNKS = PROJ_WIDTH // PROJ_CHUNK


def _inproj_kernel(x_ref, g_ref, w_ref, qkv_ref, z_ref, xbc_ref, uv_ref, dt_ref, w16_ref, xn_ref):
    outs = (qkv_ref, z_ref, xbc_ref, uv_ref, dt_ref)
    step = pl.program_id(0)

    @pl.when(step < N_W_STEPS)
    def _load_weights():
        rows = pl.ds(pl.multiple_of(step * W_ROWS, W_ROWS), W_ROWS)
        w = w_ref[...]
        uv0 = RAW_WIDTH + SSD_HEADS
        lane = lax.broadcasted_iota(jnp.int32, (1, LANES), 1)
        w16_ref[rows, 0:RAW_WIDTH] = w[:, 0:RAW_WIDTH].astype(BF16)
        w16_ref[rows, RAW_WIDTH:RAW_WIDTH + UV_WIDTH] = w[:, uv0:uv0 + UV_WIDTH].astype(BF16)
        dt_cols = jnp.where(lane < SSD_HEADS, w[:, RAW_WIDTH:RAW_WIDTH + LANES], 0.0)
        w16_ref[rows, RAW_WIDTH + UV_WIDTH:] = dt_cols.astype(BF16)

    @pl.when(step >= N_W_STEPS)
    def _row_tile():
        xn_ref[...] = _rmsnorm_f32(x_ref[...], g_ref[...]).astype(BF16)
        starts = np.cumsum([0] + [p[1] for p in _PROJ_PIECES])
        for c in range(N_PROJ_CHUNKS):
            lo, hi = c * PROJ_CHUNK, (c + 1) * PROJ_CHUNK
            r = jnp.dot(xn_ref[...], w16_ref[:, lo:hi], preferred_element_type=F32)
            for k, o_ref in enumerate(outs):
                a, b = max(lo, int(starts[k])), min(hi, int(starts[k + 1]))
                if a < b:
                    o_ref[:, a - int(starts[k]):b - int(starts[k])] = r[:, a - lo:b - lo].astype(o_ref.dtype)


def _inproj(x, layer, gain, w_in):
    m = x.shape[0]
    d_in = w_in.shape[-1]
    assert d_in == RAW_WIDTH + SSD_HEADS + UV_WIDTH and RAW_WIDTH % LANES == 0
    row = lambda w: pl.BlockSpec((ROW_TILE, w), lambda i: (jnp.maximum(i - N_W_STEPS, 0), 0))
    return pl.pallas_call(
        _inproj_kernel,
        grid=(N_W_STEPS + m // ROW_TILE,),
        in_specs=[row(D_MODEL), pl.BlockSpec((1, D_MODEL), lambda i: (0, 0)),
                  pl.BlockSpec((W_ROWS, d_in), lambda i: (layer * N_W_STEPS + jnp.minimum(i, N_W_STEPS - 1), 0))],
        out_specs=[row(w) for _, w, _ in _PROJ_PIECES],
        out_shape=[jax.ShapeDtypeStruct((m, w), dt) for _, w, dt in _PROJ_PIECES],
        scratch_shapes=[pltpu.VMEM((D_MODEL, PROJ_WIDTH), BF16), pltpu.VMEM((ROW_TILE, D_MODEL), BF16)],
        compiler_params=_params(1),
        name="inproj",
    )(x, gain[layer].reshape(1, D_MODEL), w_in.reshape(-1, d_in))


NAT, P4, P16 = 0, 1, 2


def _att_kernel(q_ref, k_ref, v_ref, o_ref, qa_ref, qb_ref, kk_ref, ve_ref,
                acc1_ref, m1_ref, l1_ref, acc3_ref, m3_ref, l3_ref, q4_ref, k4_ref, v4_ref, band_ref, cur_ref):
    seq = q_ref.shape[0]
    T = ATT_BLOCK
    d4, d16 = DILATED_PAIRS[1][1], DILATED_PAIRS[2][1]
    sub4 = seq // d4
    lane = lax.broadcasted_iota(jnp.int32, (1, LANES), 1)
    first = lane < HEAD_DIM
    qi = lax.broadcasted_iota(jnp.int32, (T, T), 0)
    kj = lax.broadcasted_iota(jnp.int32, (T, T), 1)
    cur_bias = jnp.where(kj <= qi, 0.0, -jnp.inf).astype(F32)
    prev_bias = jnp.where(kj >= qi, 0.0, -jnp.inf).astype(F32)
    for half in range(2):
        cur_ref[half * T:(half + 1) * T, :] = cur_bias
        band_ref[half * T:(half + 1) * T, 0:T] = prev_bias
        band_ref[half * T:(half + 1) * T, T:2 * T] = cur_bias
    q_scale = HEAD_DIM ** -0.5 * math.log2(math.e)

    def prep(layout, dst, q, k, v):
        q = q * q_scale
        qa_ref[layout, dst, :] = jnp.where(first, q, 0.0).astype(BF16)
        qb_ref[layout, dst, :] = jnp.where(first, 0.0, q).astype(BF16)
        kk_ref[layout, dst, :] = k.astype(BF16)
        ve_ref[layout, dst, 0:LANES] = v.astype(BF16)
        ve_ref[layout, dst, LANES:2 * LANES] = jnp.ones((T, LANES), BF16)

    for c in range(seq // T):
        rows = pl.ds(c * T, T)
        prep(NAT, rows, q_ref[rows, :], k_ref[rows, :], v_ref[rows, :])
        src = pl.ds(c // d4 + (c % d4) * (T * d4), T, stride=d4)
        q, k, v = q_ref[src, :], k_ref[src, :], v_ref[src, :]
        q4_ref[rows, :] = q
        k4_ref[rows, :] = k
        v4_ref[rows, :] = v
        prep(P4, rows, q, k, v)

    for r16 in range(d16):
        rows = pl.ds(r16 * T, T)
        src = pl.ds((r16 % d4) * sub4 + r16 // d4, T, stride=d4)
        prep(P16, rows, q4_ref[src, :], k4_ref[src, :], v4_ref[src, :])

    def block(layout, qrows, krows, bias_ref):
        def scores():
            q2 = jnp.concatenate([qa_ref[layout, qrows, :], qb_ref[layout, qrows, :]], axis=0)
            s = lax.dot_general(q2, kk_ref[layout, krows, :], (((1,), (1,)), ((), ())),
                                preferred_element_type=F32) + bias_ref[...]
            m = jnp.max(s, axis=-1, keepdims=True)
            return jnp.exp2(s - m).astype(BF16), m

        def values(p, m):
            r = jnp.dot(p, ve_ref[layout, krows, :], preferred_element_type=F32)
            acc = jnp.where(first, r[0:T, 0:LANES], r[T:2 * T, 0:LANES])
            lsum = jnp.where(first, r[0:T, LANES:2 * LANES], r[T:2 * T, LANES:2 * LANES])
            return acc, jnp.where(first, m[0:T], m[T:2 * T]), lsum

        return scores, values

    def rows_of(start, n=T):
        return pl.ds(start, n)

    work = []

    def sink1(rows):
        def store(acc, mb, lsum):
            acc1_ref[rows, :] = acc
            m1_ref[rows, :] = mb
            l1_ref[rows, :] = lsum
        return store

    work.append((*block(NAT, rows_of(0), rows_of(0), cur_ref), sink1(rows_of(0))))
    for n in range(1, seq // T):
        work.append((*block(NAT, rows_of(n * T), rows_of((n - 1) * T, 2 * T), band_ref), sink1(rows_of(n * T))))

    def sink3(r16):
        def store(acc, mb, lsum):
            dst = pl.ds((r16 % d4) * sub4 + r16 // d4, T, stride=d4)
            acc3_ref[dst, :] = acc
            m3_ref[dst, :] = mb
            l3_ref[dst, :] = lsum
        return store

    for r16 in range(d16):
        work.append((*block(P16, rows_of(r16 * T), rows_of(r16 * T), cur_ref), sink3(r16)))

    def sink2(r4, n):
        def merge(acc2, mb2, l2):
            prow = rows_of(r4 * sub4 + n * T)
            trow = pl.ds(r4 + n * (T * d4), T, stride=d4)
            acc1, mb1, l1 = acc1_ref[trow, :], m1_ref[trow, :], l1_ref[trow, :]
            acc3, mb3, l3 = acc3_ref[prow, :], m3_ref[prow, :], l3_ref[prow, :]
            m = jnp.maximum(mb1, jnp.maximum(mb2, mb3))
            w1, w2, w3 = jnp.exp2(mb1 - m), jnp.exp2(mb2 - m), jnp.exp2(mb3 - m)
            num = w1 * acc1 + w2 * acc2 + w3 * acc3
            den = w1 * l1 + w2 * l2 + w3 * l3
            o_ref[trow, :] = num / den
        return merge

    for r4 in range(d4):
        work.append((*block(P4, rows_of(r4 * sub4), rows_of(r4 * sub4), cur_ref), sink2(r4, 0)))
        for n in range(1, sub4 // T):
            work.append((*block(P4, rows_of(r4 * sub4 + n * T), rows_of(r4 * sub4 + (n - 1) * T, 2 * T), band_ref),
                         sink2(r4, n)))

    staged = []
    for scores, values, sink in work:
        staged.append((values, sink, scores()))
        if len(staged) > ATT_PIPE:
            values0, sink0, pm = staged.pop(0)
            sink0(*values0(*pm))
    for values0, sink0, pm in staged:
        sink0(*values0(*pm))


def _attention(qkv):
    b, s, _ = qkv.shape
    for window, dil in DILATED_PAIRS:
        assert window // dil == ATT_BLOCK and s % (ATT_BLOCK * dil) == 0
    assert DILATED_PAIRS[0][1] == 1 and DILATED_PAIRS[2][1] == DILATED_PAIRS[1][1] ** 2
    n_pairs = ATT_WIDTH // LANES
    spec = lambda part: pl.BlockSpec((None, s, LANES), lambda bi, hp: (bi, 0, part * n_pairs + hp))
    return pl.pallas_call(
        _att_kernel,
        grid=(b, n_pairs),
        in_specs=[spec(0), spec(1), spec(2)],
        out_specs=pl.BlockSpec((None, s, LANES), lambda bi, hp: (bi, 0, hp)),
        out_shape=jax.ShapeDtypeStruct((b, s, ATT_WIDTH), F32),
        scratch_shapes=[pltpu.VMEM((3, s, LANES), BF16)] * 3 + [pltpu.VMEM((3, s, 2 * LANES), BF16)]
        + [pltpu.VMEM((s, LANES), F32)] * 9
        + [pltpu.VMEM((2 * ATT_BLOCK, 2 * ATT_BLOCK), F32), pltpu.VMEM((2 * ATT_BLOCK, ATT_BLOCK), F32)],
        compiler_params=_params(2),
        name="dilated_attention",
    )(qkv, qkv, qkv)


SSD_HALO = 8
HEADS_PER_GROUP = SSD_HEADS // SSD_GROUPS
GROUP_LANES = HEADS_PER_GROUP * SSD_HEADDIM


def _ssd_tile(fresh, z_ref, xbc_ref, dt_ref, cw_ref, cb_ref, dtb_ref, alog_ref, dsk_ref, ng_ref,
              o_ref, state_ref, halo_ref, ext_ref):
    n_chunks = z_ref.shape[0] // SSD_CHUNK
    L = SSD_CHUNK
    row = lax.broadcasted_iota(jnp.int32, (L, L), 0)
    col = lax.broadcasted_iota(jnp.int32, (L, L), 1)
    tril = row >= col
    cumsum_mat = tril.astype(F32)
    lane = lax.broadcasted_iota(jnp.int32, (1, LANES), 1)
    lane_w = lax.broadcasted_iota(jnp.int32, (1, SSD_WIDTH), 1)
    first_group = lane_w < GROUP_LANES
    first_head = lane < SSD_HEADDIM
    a_neg = -jnp.exp(alog_ref[...])
    n_b = SSD_GROUPS * SSD_STATE

    state = jnp.where(fresh, 0.0, state_ref[...])
    for c in range(n_chunks):
        rows = slice(c * L, (c + 1) * L)
        if c == 0:
            halo = jnp.where(fresh, 0.0, halo_ref[...])
        else:
            halo = xbc_ref[c * L - SSD_HALO:c * L, :]
        ext_ref[c, 0:SSD_HALO, :] = halo
        ext_ref[c, SSD_HALO:, :] = xbc_ref[rows, :]
        conv = cb_ref[...]
        for w in range(SSD_CONV):
            o = SSD_HALO - (SSD_CONV - 1) + w
            conv = conv + cw_ref[w:w + 1, :] * ext_ref[c, o:o + L, :]
        xact = _silu(conv)
        xs = xact[:, :SSD_WIDTH]
        bm = [xact[:, SSD_WIDTH + g * SSD_STATE:SSD_WIDTH + (g + 1) * SSD_STATE] for g in range(SSD_GROUPS)]
        cm = [xact[:, SSD_WIDTH + n_b + g * SSD_STATE:SSD_WIDTH + n_b + (g + 1) * SSD_STATE]
              for g in range(SSD_GROUPS)]
        bmt16 = [t.T.astype(BF16) for t in bm]
        cm16 = [t.astype(BF16) for t in cm]

        dt = jax.nn.softplus(dt_ref[rows, :] + dtb_ref[...])
        a = dt * a_neg
        acs = jnp.dot(cumsum_mat, a, precision=lax.Precision.HIGHEST, preferred_element_type=F32)
        acs_t = acs.T
        dt_t = dt.T
        acs_last = acs[L - 1:L, :]
        exp_acs_h = jnp.exp(acs)
        to_end_h = jnp.exp(acs_last - acs) * dt
        chunk_decay_h = jnp.exp(acs_last)
        cb = [jnp.dot(cm16[g], bmt16[g], preferred_element_type=F32) for g in range(SSD_GROUPS)]

        y_diag, e_pairs, w_pairs, d_pairs = [], [], [], []
        for p in range(SSD_HEADS // 2):
            xs_pair = xs[:, p * LANES:(p + 1) * LANES].astype(BF16)
            yd, ecol, wcol, dcol = [], [], [], []
            for h in (2 * p, 2 * p + 1):
                g = h // HEADS_PER_GROUP
                acs_col = jnp.broadcast_to(acs[:, h:h + 1], (L, L))
                seg = acs_col - acs_t[h:h + 1, :]
                decay = jnp.exp(jnp.where(tril, seg, -jnp.inf))
                mix = (cb[g] * decay * dt_t[h:h + 1, :]).astype(BF16)
                yd.append(jnp.dot(mix, xs_pair, preferred_element_type=F32))
                ecol.append(jnp.broadcast_to(exp_acs_h[:, h:h + 1], (L, LANES)))
                wcol.append(jnp.broadcast_to(to_end_h[:, h:h + 1], (L, LANES)))
                dcol.append(jnp.broadcast_to(chunk_decay_h[:, h:h + 1], (1, LANES)))
            y_diag.append(jnp.where(first_head, yd[0], yd[1]))
            e_pairs.append(jnp.where(first_head, ecol[0], ecol[1]))
            w_pairs.append(jnp.where(first_head, wcol[0], wcol[1]))
            d_pairs.append(jnp.where(first_head, dcol[0], dcol[1]))
        y_diag = jnp.concatenate(y_diag, axis=1)
        exp_acs = jnp.concatenate(e_pairs, axis=1)
        to_end = jnp.concatenate(w_pairs, axis=1)
        chunk_decay = jnp.concatenate(d_pairs, axis=1)

        st16 = state.astype(BF16)
        y_off = jnp.where(first_group,
                          jnp.dot(cm16[0], st16, preferred_element_type=F32),
                          jnp.dot(cm16[1], st16, preferred_element_type=F32)) * exp_acs
        xdd = (xs * to_end).astype(BF16)
        new = jnp.where(first_group,
                        jnp.dot(bmt16[0], xdd, preferred_element_type=F32),
                        jnp.dot(bmt16[1], xdd, preferred_element_type=F32))
        state = state * chunk_decay + new

        y = y_diag + y_off + dsk_ref[...] * xs
        y = y * _silu(z_ref[rows, :])
        ysq = y * y
        s0 = jnp.sum(jnp.where(first_group, ysq, 0.0), axis=-1, keepdims=True)
        s1 = jnp.sum(jnp.where(first_group, 0.0, ysq), axis=-1, keepdims=True)
        ms = jnp.where(first_group, s0, s1) * (1.0 / GROUP_LANES)
        o_ref[rows, :] = (y * lax.rsqrt(ms + RMS_EPS) * ng_ref[...]).astype(o_ref.dtype)
        if c == n_chunks - 1:
            state_ref[...] = state
            halo_ref[...] = xbc_ref[n_chunks * L - SSD_HALO:n_chunks * L, :]
        yield


def _ssd_operands(z, xbc, dt, conv_w, conv_b, dt_bias, a_log, d_skip, norm_g, row_spec, const_spec):
    pad = lambda v: jnp.pad(v, (0, DT_PAD - SSD_HEADS)).reshape(1, DT_PAD)
    arrays = [z, xbc, dt, conv_w, conv_b.reshape(1, SSD_CONV_DIM), pad(dt_bias), pad(a_log),
              jnp.repeat(d_skip, SSD_HEADDIM).reshape(1, SSD_WIDTH), norm_g.reshape(1, SSD_WIDTH)]
    specs = [row_spec(SSD_WIDTH), row_spec(SSD_CONV_DIM), row_spec(DT_PAD),
             const_spec(SSD_CONV, SSD_CONV_DIM), const_spec(1, SSD_CONV_DIM), const_spec(1, DT_PAD),
             const_spec(1, DT_PAD), const_spec(1, SSD_WIDTH), const_spec(1, SSD_WIDTH)]
    scratch = [pltpu.VMEM((ROW_TILE, SSD_WIDTH), BF16),
               pltpu.VMEM((SSD_STATE, SSD_WIDTH), F32),
               pltpu.VMEM((SSD_HALO, SSD_CONV_DIM), F32),
               pltpu.VMEM((ROW_TILE // SSD_CHUNK, SSD_HALO + SSD_CHUNK, SSD_CONV_DIM), F32)]
    return arrays, specs, scratch


def _sgu_tile(uv_ref, lng_ref, lnb_ref, w_ref, bs_ref, o_ref):
    row = lax.broadcasted_iota(jnp.int32, (SGU_CHUNK, SGU_CHUNK), 0)
    col = lax.broadcasted_iota(jnp.int32, (SGU_CHUNK, SGU_CHUNK), 1)
    w = [jnp.where(row >= col, w_ref[g], 0.0).astype(BF16) for g in range(SGU_GROUPS)]
    lane = lax.broadcasted_iota(jnp.int32, (1, LANES), 1)
    first = lane < SGU_GROUP_DIM
    for c in range(uv_ref.shape[0] // SGU_CHUNK):
        rows = slice(c * SGU_CHUNK, (c + 1) * SGU_CHUNK)
        uv = uv_ref[rows, :]
        act = 0.5 * uv * (1.0 + lax.erf(uv * (1.0 / math.sqrt(2.0))))
        u = act[:, :SGU_WIDTH]
        v = act[:, SGU_WIDTH:]
        mu = jnp.mean(v, axis=-1, keepdims=True)
        var = jnp.mean(jnp.square(v - mu), axis=-1, keepdims=True)
        vn = (v - mu) * lax.rsqrt(var + LN_EPS) * lng_ref[...] + lnb_ref[...]
        mixed = []
        for p in range(SGU_WIDTH // LANES):
            vp = vn[:, p * LANES:(p + 1) * LANES]
            lo = jnp.where(first, vp, 0.0).astype(BF16)
            hi = jnp.where(first, 0.0, vp).astype(BF16)
            mixed.append(jnp.dot(w[2 * p], lo, preferred_element_type=F32)
                         + jnp.dot(w[2 * p + 1], hi, preferred_element_type=F32))
        mixed = jnp.concatenate(mixed, axis=1) + bs_ref[...]
        o_ref[rows, :] = (u * mixed).astype(o_ref.dtype)


def _sgu_operands(uv, ln_g, ln_b, w_s, b_s, row_spec, const_spec):
    bias = jnp.repeat(b_s.T, SGU_GROUP_DIM, axis=1)
    arrays = [uv, ln_g.reshape(1, SGU_WIDTH), ln_b.reshape(1, SGU_WIDTH), w_s, bias]
    specs = [row_spec(UV_WIDTH), const_spec(1, SGU_WIDTH), const_spec(1, SGU_WIDTH),
             pl.BlockSpec((SGU_GROUPS, SGU_CHUNK, SGU_CHUNK), lambda i: (0, 0, 0)),
             const_spec(SGU_CHUNK, SGU_WIDTH)]
    scratch = [pltpu.VMEM((ROW_TILE, SGU_WIDTH), BF16)]
    return arrays, specs, scratch


def _mixers(x, b, s, layer, gain, w_in):
    qkv, z, xbc, uv, dt = _inproj(x, layer, gain, w_in)
    y_att = _attention(qkv.reshape(b, s, QKV_WIDTH))
    return y_att.reshape(b * s, ATT_WIDTH), (z, xbc, dt), uv


def kernel(x, ffn1_norm, ffn1_w_gate, ffn1_w_up, ffn1_w_down, mix_norm, w_in, conv_w, conv_b, dt_bias, a_log, d_skip, ssd_norm, sgu_ln_g, sgu_ln_b, sgu_w, sgu_b, w_out, ffn2_norm, ffn2_w_gate, ffn2_w_up, ffn2_w_down, final_norm):
    b, s, d = x.shape
    depth = ffn1_norm.shape[0]
    h = x.reshape(b * s, d)
    for i in range(depth):
        h = _ffn(h, i, ffn1_norm, ffn1_w_gate, ffn1_w_up, ffn1_w_down)
        y_att, ssd_proj, uv = _mixers(h, b, s, i, mix_norm, w_in)
        ssd_args = (*ssd_proj, conv_w[i], conv_b[i], dt_bias[i], a_log[i], d_skip[i], ssd_norm[i])
        sgu_args = (uv, sgu_ln_g[i], sgu_ln_b[i], sgu_w[i], sgu_b[i])
        h = _ffn(h, i, ffn2_norm, ffn2_w_gate, ffn2_w_up, ffn2_w_down,
                 mix=(y_att, w_out, s // ROW_TILE, ssd_args, sgu_args),
                 final_gain=final_norm if i == depth - 1 else None)
    return h.reshape(b, s, d)
```
